```python
import math
import jax
import jax.numpy as jnp
from jax import lax
import numpy as np

D_MODEL = 1024
BATCH = 2
SEQ = 8192
DEPTH = 4

GRID_W = 64
CTX_LEN = 256
N_MIXERS = 3
N_ATTN_LAYERS = (DEPTH + 2) // N_MIXERS
N_RET_LAYERS = (DEPTH + 1) // N_MIXERS
N_SSM_LAYERS = DEPTH // N_MIXERS
DA_HEADS = 8
DA_HEAD_DIM = D_MODEL // (2 * DA_HEADS)
QUERY_BLOCK = 128
RET_HEADS = 4
RET_KEY_DIM = D_MODEL // RET_HEADS
RET_VAL_DIM = 2 * D_MODEL // RET_HEADS
RET_CHUNK = 128
SSM_GROUP = 16
SSM_GROUPS = D_MODEL // SSM_GROUP
SSM_STATE = 64
DT_MIN = 0.001
DT_MAX = 0.1
FFN_HIDDEN = -(-8 * D_MODEL // (3 * 256)) * 256
ROPE_THETA = 10000.0
NORM_EPS = 1e-6

kernel_name = 'hybrid_diffattn_retention_s5_dit'


def rmsnorm(x, gain):
    xf = x.astype(jnp.float32)
    y = xf * lax.rsqrt(jnp.mean(xf * xf, axis=-1, keepdims=True) + NORM_EPS)
    return (y * gain.astype(jnp.float32)).astype(x.dtype)


def modulate(h, shift, scale):
    return h * (1.0 + scale) + shift


def axial_rope_tables(rows, head_dim):
    quarter = head_dim // 4
    freqs = ROPE_THETA ** (-jnp.arange(quarter, dtype=jnp.float32) / quarter)
    row = jnp.repeat(jnp.arange(rows, dtype=jnp.float32), GRID_W)
    col = jnp.tile(jnp.arange(GRID_W, dtype=jnp.float32), rows)
    ang_r = row[:, None] * freqs
    ang_c = col[:, None] * freqs
    ang = jnp.concatenate([ang_r, ang_r, ang_c, ang_c], axis=-1)
    return jnp.cos(ang), jnp.sin(ang)


def apply_rope(x, cos, sin):
    q4 = x.shape[-1] // 4
    xr = x.reshape(x.shape[:-1] + (2, 2, q4))
    rot = jnp.stack([-xr[..., 1, :], xr[..., 0, :]], axis=-2).reshape(x.shape)
    return x * cos.astype(x.dtype) + rot * sin.astype(x.dtype)


def swiglu(h, w_gate, w_up, w_down):
    return (jax.nn.silu(h @ w_gate) * (h @ w_up)) @ w_down


def diff_softmax_attend(q, k, v, lam):
    s = jnp.einsum('bhiqd,bhikd->bhiqk', q, k, preferred_element_type=jnp.float32)
    p = jax.nn.softmax(s, axis=-1)
    a = p[:, :, 0] - lam * p[:, :, 1]
    return jnp.einsum('bhqk,bhkv->bhqv', a.astype(v.dtype), v, preferred_element_type=jnp.float32)


def diff_attention(hc, hx, w_in, w_out, lam_vec, subln, lam_init, cos, sin, ctx_out):
    bsz, seq, _ = hx.shape
    scale = DA_HEAD_DIM ** -0.5

    def project(h):
        p = h @ w_in
        n = h.shape[1]
        q = p[..., :D_MODEL].reshape(bsz, n, DA_HEADS, 2, DA_HEAD_DIM).transpose(0, 2, 3, 1, 4)
        k = p[..., D_MODEL:2 * D_MODEL].reshape(bsz, n, DA_HEADS, 2, DA_HEAD_DIM).transpose(0, 2, 3, 1, 4)
        v = p[..., 2 * D_MODEL:].reshape(bsz, n, DA_HEADS, 2 * DA_HEAD_DIM).transpose(0, 2, 1, 3)
        return q * scale, k, v

    qc, kc, vc = project(hc)
    qx, kx, vx = project(hx)
    qx = apply_rope(qx, cos, sin)
    kx = apply_rope(kx, cos, sin)
    lv = lam_vec.astype(jnp.float32)
    lam = jnp.exp(jnp.sum(lv[0] * lv[1])) - jnp.exp(jnp.sum(lv[2] * lv[3])) + lam_init

    k_all = jnp.concatenate([kc, kx], axis=3)
    v_all = jnp.concatenate([vc, vx], axis=2)
    nb = seq // QUERY_BLOCK
    qb = jnp.moveaxis(qx.reshape(bsz, DA_HEADS, 2, nb, QUERY_BLOCK, DA_HEAD_DIM), 3, 0)
    ox = lax.map(lambda q: diff_softmax_attend(q, k_all, v_all, lam), qb)
    ox = jnp.moveaxis(ox, 0, 2).reshape(bsz, DA_HEADS, seq, 2 * DA_HEAD_DIM)

    def finish(o):
        n = o.shape[2]
        o = rmsnorm(o, subln) * (1.0 - lam_init)
        return o.transpose(0, 2, 1, 3).reshape(bsz, n, D_MODEL).astype(hx.dtype) @ w_out

    yx = finish(ox)
    yc = finish(diff_softmax_attend(qc, kc, vc, lam)) if ctx_out else None
    return yc, yx


def retention_chunked(q, k, v, log_g, r0):
    bsz, nh, t, _ = q.shape
    dv = v.shape[-1]
    n = t // RET_CHUNK
    idx = jnp.arange(RET_CHUNK, dtype=jnp.float32)
    diff = idx[:, None] - idx[None, :]
    inner_decay = jnp.where(diff >= 0, jnp.exp(log_g[:, None, None] * jnp.maximum(diff, 0.0)), 0.0)
    q_decay = jnp.exp(log_g[:, None] * (idx + 1.0))[..., None]
    k_decay = jnp.exp(log_g[:, None] * (RET_CHUNK - 1.0 - idx))[..., None]
    chunk_decay = jnp.exp(log_g * RET_CHUNK)[:, None, None]

    def to_chunks(z):
        return jnp.moveaxis(z.astype(jnp.float32).reshape(bsz, nh, n, RET_CHUNK, z.shape[-1]), 2, 0)

    def step(r, qkv):
        qc, kc, vc = qkv
        s = jnp.einsum('bhqd,bhkd->bhqk', qc, kc) * inner_decay
        inner = jnp.einsum('bhqk,bhkv->bhqv', s, vc)
        cross = jnp.einsum('bhqd,bhdv->bhqv', qc * q_decay, r)
        r_new = chunk_decay * r + jnp.einsum('bhkd,bhkv->bhdv', kc * k_decay, vc)
        return r_new, inner + cross

    r_fin, out = lax.scan(step, r0, (to_chunks(q), to_chunks(k), to_chunks(v)))
    return jnp.moveaxis(out, 0, 2).reshape(bsz, nh, t, dv), r_fin


def retention(hc, hx, w_in, w_out, decay_logit, cos, sin, ctx_out):
    bsz = hx.shape[0]

    def project(h):
        p = h @ w_in
        n = h.shape[1]
        def heads(z, dh):
            return z.reshape(bsz, n, RET_HEADS, dh).transpose(0, 2, 1, 3)
        q = heads(p[..., :D_MODEL], RET_KEY_DIM)
        k = heads(p[..., D_MODEL:2 * D_MODEL], RET_KEY_DIM) * (RET_KEY_DIM ** -0.5)
        v = heads(p[..., 2 * D_MODEL:4 * D_MODEL], RET_VAL_DIM)
        g = p[..., 4 * D_MODEL:]
        return q, k, v, g

    qc, kc, vc, gc = project(hc)
    qx, kx, vx, gx = project(hx)
    qx = apply_rope(qx, cos, sin)
    kx = apply_rope(kx, cos, sin)
    log_g = jax.nn.log_sigmoid(decay_logit.astype(jnp.float32))
    zero = jnp.zeros((bsz, RET_HEADS, RET_KEY_DIM, RET_VAL_DIM), jnp.float32)

    def flip(z):
        return jnp.flip(z, axis=2)

    oc_f, rc_f = retention_chunked(qc, kc, vc, log_g[0], zero)
    oc_b, rc_b = retention_chunked(flip(qc), flip(kc), flip(vc), log_g[1], zero)
    ox_f, _ = retention_chunked(qx, kx, vx, log_g[0], rc_f)
    ox_b, _ = retention_chunked(flip(qx), flip(kx), flip(vx), log_g[1], rc_b)

    def finish(o, g):
        n = o.shape[2]
        o = o * lax.rsqrt(jnp.mean(o * o, axis=-1, keepdims=True) + NORM_EPS)
        o = o.transpose(0, 2, 1, 3).reshape(bsz, n, RET_HEADS * RET_VAL_DIM).astype(g.dtype)
        return (jax.nn.silu(g) * o) @ w_out

    yx = finish(ox_f + flip(ox_b), gx)
    yc = finish(oc_f + flip(oc_b), gc) if ctx_out else None
    return yc, yx


def ssm_combine(e1, e2):
    a1, b1 = e1
    a2, b2 = e2
    return a1 * a2, a2 * b1 + b2


def ssm_direction(u, lam_bar, b_bar, c_mat, h0, reverse):
    start, end = (-1, 0) if reverse else (0, -1)
    bu = jnp.einsum('btgh,gph->btgp', u, b_bar)
    bu = bu.at[:, start].add(lam_bar * h0)
    a = jnp.broadcast_to(lam_bar, bu.shape)
    _, h = lax.associative_scan(ssm_combine, (a, bu), axis=1, reverse=reverse)
    y = jnp.einsum('btgp,ghp->btgh', h, c_mat).real
    return y, h[:, end]


def s5_mixer(hc, hx, lam_re, lam_im, log_dt, b_re, b_im, c_re, c_im, d_skip, w_glu, ctx_out):
    f32 = jnp.float32
    lam = lax.complex(lam_re.astype(f32), lam_im.astype(f32))
    lam_bar = jnp.exp(lam * jnp.exp(log_dt.astype(f32))[..., None])
    b_bar = ((lam_bar - 1.0) / lam)[..., None] * lax.complex(b_re.astype(f32), b_im.astype(f32))
    c_mat = lax.complex(c_re.astype(f32), c_im.astype(f32))

    def to_groups(h):
        return h.astype(f32).reshape(h.shape[0], h.shape[1], SSM_GROUPS, SSM_GROUP).astype(jnp.complex64)

    u_c, u_x = to_groups(hc), to_groups(hx)
    h0 = jnp.zeros((hx.shape[0], SSM_GROUPS, SSM_STATE), jnp.complex64)
    yc_f, st_f = ssm_direction(u_c, lam_bar[0], b_bar[0], c_mat[0], h0, False)
    yc_b, st_b = ssm_direction(u_c, lam_bar[1], b_bar[1], c_mat[1], h0, True)
    yx_f, _ = ssm_direction(u_x, lam_bar[0], b_bar[0], c_mat[0], st_f, False)
    yx_b, _ = ssm_direction(u_x, lam_bar[1], b_bar[1], c_mat[1], st_b, True)

    def finish(h, y):
        y = y.reshape(h.shape) + d_skip.astype(f32) * h.astype(f32)
        g = jax.nn.gelu(y).astype(h.dtype)
        return (g @ w_glu[:, :D_MODEL]) * jax.nn.sigmoid(g @ w_glu[:, D_MODEL:])

    yx = finish(hx, yx_f + yx_b)
    yc = finish(hc, yc_f + yc_b) if ctx_out else None
    return yc, yx


def setup_inputs(seed: int = 0) -> dict:
    key = jax.random.key(seed)
    ks = jax.random.split(key, 32)
    f32 = jnp.float32
    D = D_MODEL

    def nrm(k, shape, s):
        return jax.random.normal(k, shape, f32) * s

    gamma0 = 1.0 - 2.0 ** (-5.0 - np.arange(RET_HEADS))
    logit0 = jnp.asarray(np.log(gamma0 / (1.0 - gamma0)).astype(np.float32))
    return {
        'x': nrm(ks[0], (BATCH, SEQ, D), 1.0),
        'c': nrm(ks[1], (BATCH, D), 1.0),
        'ctx': nrm(ks[2], (BATCH, CTX_LEN, D), 1.0),
        'c_ctx': nrm(ks[3], (D,), 1.0),
        'mod_w': nrm(ks[4], (DEPTH, D, 6 * D), 0.5 * D ** -0.5),
        'mod_b': nrm(ks[5], (DEPTH, 6 * D), 0.02),
        'norm_g': 1.0 + nrm(ks[6], (DEPTH, 4, D), 0.02),
        'attn_w_in': nrm(ks[7], (N_ATTN_LAYERS, D, 3 * D), D ** -0.5),
        'attn_w_out': nrm(ks[8], (N_ATTN_LAYERS, D, D), D ** -0.5),
        'attn_lambda': nrm(ks[9], (N_ATTN_LAYERS, 4, DA_HEAD_DIM), 0.1),
        'attn_subln': 1.0 + nrm(ks[10], (N_ATTN_LAYERS, 2 * DA_HEAD_DIM), 0.02),
        'ret_w_in': nrm(ks[11], (N_RET_LAYERS, D, 6 * D), D ** -0.5),
        'ret_w_out': nrm(ks[12], (N_RET_LAYERS, 2 * D, D), (2 * D) ** -0.5),
        'ret_decay_logit': logit0 + nrm(ks[13], (N_RET_LAYERS, 2, RET_HEADS), 0.1),
        'ssm_lambda_re': -0.5 + nrm(ks[14], (N_SSM_LAYERS, 2, SSM_GROUPS, SSM_STATE), 0.01),
        'ssm_lambda_im': math.pi * jnp.arange(SSM_STATE, dtype=f32) + nrm(ks[15], (N_SSM_LAYERS, 2, SSM_GROUPS, SSM_STATE), 0.01),
        'ssm_log_dt': jax.random.uniform(ks[16], (N_SSM_LAYERS, 2, SSM_GROUPS), f32, math.log(DT_MIN), math.log(DT_MAX)),
        'ssm_b_re': nrm(ks[17], (N_SSM_LAYERS, 2, SSM_GROUPS, SSM_STATE, SSM_GROUP), (2 * SSM_GROUP) ** -0.5),
        'ssm_b_im': nrm(ks[18], (N_SSM_LAYERS, 2, SSM_GROUPS, SSM_STATE, SSM_GROUP), (2 * SSM_GROUP) ** -0.5),
        'ssm_c_re': nrm(ks[19], (N_SSM_LAYERS, 2, SSM_GROUPS, SSM_GROUP, SSM_STATE), SSM_STATE ** -0.5),
        'ssm_c_im': nrm(ks[20], (N_SSM_LAYERS, 2, SSM_GROUPS, SSM_GROUP, SSM_STATE), SSM_STATE ** -0.5),
        'ssm_d': nrm(ks[21], (N_SSM_LAYERS, D), 1.0),
        'ssm_w_glu': nrm(ks[22], (N_SSM_LAYERS, D, 2 * D), D ** -0.5),
        'ffn_w_gate': nrm(ks[23], (DEPTH, D, FFN_HIDDEN), D ** -0.5),
        'ffn_w_up': nrm(ks[24], (DEPTH, D, FFN_HIDDEN), D ** -0.5),
        'ffn_w_down': nrm(ks[25], (DEPTH, FFN_HIDDEN, D), FFN_HIDDEN ** -0.5),
    }


def reference(x, c, ctx, c_ctx, mod_w, mod_b, norm_g, attn_w_in, attn_w_out, attn_lambda, attn_subln,
              ret_w_in, ret_w_out, ret_decay_logit, ssm_lambda_re, ssm_lambda_im, ssm_log_dt,
              ssm_b_re, ssm_b_im, ssm_c_re, ssm_c_im, ssm_d, ssm_w_glu, ffn_w_gate, ffn_w_up, ffn_w_down):
    bsz, seq, _ = x.shape
    rows = seq // GRID_W
    cos_a, sin_a = axial_rope_tables(rows, DA_HEAD_DIM)
    cos_r, sin_r = axial_rope_tables(rows, RET_KEY_DIM)
    s_lat = jax.nn.silu(c)
    s_ctx = jax.nn.silu(c_ctx)
    h, hc = x, ctx
    for l in range(DEPTH):
        last = l == DEPTH - 1
        m = (s_lat @ mod_w[l] + mod_b[l]).reshape(bsz, 6, 1, D_MODEL)
        mc = (s_ctx @ mod_w[l] + mod_b[l]).reshape(6, 1, D_MODEL)
        ux = modulate(rmsnorm(h, norm_g[l, 0]), m[:, 0], m[:, 1])
        uc = modulate(rmsnorm(hc, norm_g[l, 0]), mc[0], mc[1])
        kind, j = l % N_MIXERS, l // N_MIXERS
        if kind == 0:
            lam_init = 0.8 - 0.6 * math.exp(-0.3 * l)
            yc, yx = diff_attention(uc, ux, attn_w_in[j], attn_w_out[j], attn_lambda[j], attn_subln[j],
                                    lam_init, cos_a, sin_a, not last)
        elif kind == 1:
            yc, yx = retention(uc, ux, ret_w_in[j], ret_w_out[j], ret_decay_logit[j], cos_r, sin_r, not last)
        else:
            yc, yx = s5_mixer(uc, ux, ssm_lambda_re[j], ssm_lambda_im[j], ssm_log_dt[j], ssm_b_re[j],
                              ssm_b_im[j], ssm_c_re[j], ssm_c_im[j], ssm_d[j], ssm_w_glu[j], not last)
        h = h + m[:, 2] * rmsnorm(yx, norm_g[l, 1])
        vx = modulate(rmsnorm(h, norm_g[l, 2]), m[:, 3], m[:, 4])
        h = h + m[:, 5] * rmsnorm(swiglu(vx, ffn_w_gate[l], ffn_w_up[l], ffn_w_down[l]), norm_g[l, 3])
        if not last:
            hc = hc + mc[2] * rmsnorm(yc, norm_g[l, 1])
            vc = modulate(rmsnorm(hc, norm_g[l, 2]), mc[3], mc[4])
            hc = hc + mc[5] * rmsnorm(swiglu(vc, ffn_w_gate[l], ffn_w_up[l], ffn_w_down[l]), norm_g[l, 3])
    return h
```

```python
import functools
import math

import jax
import jax.numpy as jnp
from jax import lax
from jax.experimental import pallas as pl
from jax.experimental.pallas import tpu as pltpu

F32 = jnp.float32
BF16 = jnp.bfloat16

GRID_W = 64
DA_HEADS = 8
RET_HEADS = 4
RET_CHUNK = 128
SSM_GROUP = 16
SSM_STATE = 64
SSM_CHUNK = 16
SSM_GROUP_BLOCK = 8
ROPE_THETA = 10000.0
NORM_EPS = 1e-6
LANES = 128
TM = 256
VMEM_LIMIT = 56 * 1024 * 1024


def _cparams(*sem):
    return pltpu.CompilerParams(dimension_semantics=sem, vmem_limit_bytes=VMEM_LIMIT)


def _rms(x, gain):
    return x * lax.rsqrt(jnp.mean(x * x, axis=-1, keepdims=True) + NORM_EPS) * gain


def _silu(x):
    return x * jax.nn.sigmoid(x)


def _dot(a, b):
    return jnp.dot(a, b, preferred_element_type=F32)


def _dot_nt(a, b):
    return lax.dot_general(a, b, (((1,), (1,)), ((), ())), preferred_element_type=F32)


def _dot_tn(a, b):
    return lax.dot_general(a, b, (((0,), (0,)), ((), ())), preferred_element_type=F32)


def _mod_kernel(s_ref, w_ref, b_ref, o_ref):
    s = _silu(s_ref[...])
    o_ref[0] = jnp.dot(s, w_ref[0], preferred_element_type=F32,
                       precision=lax.Precision.HIGHEST) + b_ref[0]


def _modulation(c, c_ctx, mod_w, mod_b):
    depth, d, n = mod_w.shape
    bsz = c.shape[0]
    rows = 8
    s = jnp.zeros((rows, d), F32).at[:bsz].set(c).at[bsz].set(c_ctx)
    tn = 1536
    out = pl.pallas_call(
        _mod_kernel,
        grid=(depth, n // tn),
        in_specs=[
            pl.BlockSpec((rows, d), lambda l, j: (0, 0)),
            pl.BlockSpec((1, d, tn), lambda l, j: (l, 0, j)),
            pl.BlockSpec((1, 1, tn), lambda l, j: (l, 0, j)),
        ],
        out_specs=pl.BlockSpec((1, rows, tn), lambda l, j: (l, 0, j)),
        out_shape=jax.ShapeDtypeStruct((depth, rows, n), F32),
        compiler_params=_cparams("parallel", "parallel"),
    )(s, mod_w, mod_b.reshape(depth, 1, n))
    return out.reshape(depth, rows, 6, d)


def _mod_spec(d, bsz, nctx_tiles):
    return pl.BlockSpec((1, 6, d), lambda b, i: (jnp.where(i < nctx_tiles, bsz, b), 0, 0))


def _rope_angles(seq, head_dim):
    quarter = head_dim // 4
    freqs = ROPE_THETA ** (-jnp.arange(quarter, dtype=F32) / quarter)
    rows = seq // GRID_W
    row = jnp.repeat(jnp.arange(rows, dtype=F32), GRID_W)
    col = jnp.tile(jnp.arange(GRID_W, dtype=F32), rows)
    ang_r = row[:, None] * freqs
    ang_c = col[:, None] * freqs
    return jnp.concatenate([ang_r, ang_r, ang_c, ang_c], axis=-1)


def _rope_table(seq, ctx, head_dim):
    ang = _rope_angles(seq, head_dim)
    ang = jnp.concatenate([jnp.zeros((ctx, head_dim), F32), ang], axis=0)
    cos, sin = jnp.cos(ang), jnp.sin(ang)
    q = head_dim // 4
    first = (jnp.arange(head_dim) % (2 * q)) < q
    s_minus = jnp.where(first, -sin, 0.0)
    s_plus = jnp.where(first, 0.0, sin)
    rep = max(1, LANES // head_dim)
    return jnp.concatenate([jnp.tile(t, (1, rep)) for t in (cos, s_minus, s_plus)], axis=1)


def _apply_rope(y, tab, quarter):
    w = tab.shape[1] // 3
    cos, s_minus, s_plus = tab[:, :w], tab[:, w:2 * w], tab[:, 2 * w:]
    outs = []
    for k in range(y.shape[1] // LANES):
        yk = y[:, k * LANES:(k + 1) * LANES]
        t0 = (k * LANES) % w
        ck, mk, pk = (t[:, t0:t0 + LANES] for t in (cos, s_minus, s_plus))
        if 2 * quarter == LANES:
            outs.append(yk * ck + pltpu.roll(yk, quarter, 1) * (mk + pk))
        else:
            outs.append(yk * ck + pltpu.roll(yk, LANES - quarter, 1) * mk
                        + pltpu.roll(yk, quarter, 1) * pk)
    return jnp.concatenate(outs, axis=1)


def _proj_in_kernel(x_ref, mod_ref, g_ref, w_ref, tab_ref, o_ref, *, chunks, quarter):
    mod = mod_ref[0]
    u = (_rms(x_ref[0], g_ref[0:1, :]) * (1.0 + mod[1:2, :]) + mod[0:1, :]).astype(BF16)
    tab = tab_ref[...]
    for c0, width, mult, rope in chunks:
        y = _dot(u, w_ref[:, c0:c0 + width])
        if mult != 1.0:
            y = y * mult
        if rope:
            y = _apply_rope(y, tab, quarter)
        o_ref[0, :, c0:c0 + width] = y.astype(BF16)


def _proj_in(hs, mod_l, gains, w, tab, chunks, quarter, nctx_tiles):
    bsz, L, d = hs.shape
    n = w.shape[1]
    return pl.pallas_call(
        functools.partial(_proj_in_kernel, chunks=chunks, quarter=quarter),
        grid=(bsz, L // TM),
        in_specs=[
            pl.BlockSpec((1, TM, d), lambda b, i: (b, i, 0)),
            _mod_spec(d, bsz, nctx_tiles),
            pl.BlockSpec(gains.shape, lambda b, i: (0, 0)),
            pl.BlockSpec(w.shape, lambda b, i: (0, 0)),
            pl.BlockSpec((TM, tab.shape[1]), lambda b, i: (i, 0)),
        ],
        out_specs=pl.BlockSpec((1, TM, n), lambda b, i: (b, i, 0)),
        out_shape=jax.ShapeDtypeStruct((bsz, L, n), BF16),
        compiler_params=_cparams("parallel", "parallel"),
    )(hs, mod_l, gains, w, tab)


def _attn_kernel(lam_ref, sub_ref, q_ref, k_ref, v_ref, o_ref, *, ctx, tk, lam_init):
    i = pl.program_id(2)
    tq = q_ref.shape[1]
    L = k_ref.shape[1]
    q = q_ref[0]
    lane = lax.broadcasted_iota(jnp.int32, q.shape, 1)
    zero = jnp.zeros_like(q)
    qq = jnp.concatenate([jnp.where(lane < LANES // 2, q, zero),
                          jnp.where(lane >= LANES // 2, q, zero)], axis=0)

    def block(kb, vb, carry):
        m, l, acc = carry
        s = _dot_nt(qq, kb)
        m_new = jnp.maximum(m, jnp.max(s, axis=-1, keepdims=True))
        alpha = jnp.exp(m - m_new)
        p = jnp.exp(s - m_new)
        l = alpha * l + jnp.sum(p, axis=-1, keepdims=True)
        acc = alpha * acc + _dot(p.astype(BF16), vb)
        return m_new, l, acc

    carry = (jnp.full((2 * tq, 1), -1e30, F32), jnp.zeros((2 * tq, 1), F32),
             jnp.zeros((2 * tq, LANES), F32))
    carry = block(k_ref[0, 0:ctx, :], v_ref[0, 0:ctx, :], carry)

    def body(j, carry):
        off = pl.multiple_of(ctx + j * tk, LANES)
        return block(k_ref[0, pl.ds(off, tk), :], v_ref[0, pl.ds(off, tk), :], carry)

    nblk = jnp.where(i < ctx // tq, 0, (L - ctx) // tk)
    _, l, acc = lax.fori_loop(0, nblk, body, carry)

    lv = lam_ref[...]
    lam = (jnp.exp(jnp.sum(lv[0:1] * lv[1:2], axis=-1, keepdims=True))
           - jnp.exp(jnp.sum(lv[2:3] * lv[3:4], axis=-1, keepdims=True)) + lam_init)
    o = acc[:tq] / l[:tq] - lam * (acc[tq:] / l[tq:])
    o = _rms(o, sub_ref[...]) * (1.0 - lam_init)
    o_ref[0] = o.astype(BF16)


def _attention(p, lam_vec, subln, ctx, lam_init):
    bsz, L, n = p.shape
    d = n // 3
    ncol = d // LANES
    tk = 1024
    while (L - ctx) % tk:
        tk //= 2
    return pl.pallas_call(
        functools.partial(_attn_kernel, ctx=ctx, tk=tk, lam_init=lam_init),
        grid=(bsz, DA_HEADS, L // TM),
        in_specs=[
            pl.BlockSpec(lam_vec.shape, lambda b, h, i: (0, 0)),
            pl.BlockSpec((1, LANES), lambda b, h, i: (0, 0)),
            pl.BlockSpec((1, TM, LANES), lambda b, h, i: (b, i, h)),
            pl.BlockSpec((1, L, LANES), lambda b, h, i: (b, 0, ncol + h)),
            pl.BlockSpec((1, L, LANES), lambda b, h, i: (b, 0, 2 * ncol + h)),
        ],
        out_specs=pl.BlockSpec((1, TM, LANES), lambda b, h, i: (b, i, h)),
        out_shape=jax.ShapeDtypeStruct((bsz, L, d), BF16),
        compiler_params=_cparams("parallel", "parallel", "arbitrary"),
    )(lam_vec, subln.reshape(1, LANES), p, p, p)


def _ret_kernel(lg_ref, q_ref, k_ref, v_ref, o_ref, r_scr):
    d = pl.program_id(2)
    c = pl.program_id(3)
    C = q_ref.shape[1]

    @pl.when(c == 0)
    def _():
        r_scr[...] = jnp.zeros_like(r_scr)

    lg = lg_ref[0][:, 0:1]
    fwd = d == 0
    ii = lax.broadcasted_iota(jnp.int32, (C, C), 0)
    jj = lax.broadcasted_iota(jnp.int32, (C, C), 1)
    dd = jnp.where(fwd, ii - jj, jj - ii)
    decay = jnp.where(dd >= 0, jnp.exp(lg * jnp.maximum(dd, 0).astype(F32)), 0.0)
    idx = lax.broadcasted_iota(jnp.int32, (C, 1), 0)
    q_dec = jnp.exp(lg * jnp.where(fwd, idx + 1, C - idx).astype(F32))
    k_dec = jnp.exp(lg * jnp.where(fwd, C - 1 - idx, idx).astype(F32))
    c_dec = jnp.exp(lg * float(C))

    q = q_ref[0]
    k = k_ref[0]
    v = v_ref[0]
    r = r_scr[...]
    s = _dot_nt(q, k) * decay
    inner = _dot(s.astype(BF16), v)
    cross = _dot((q.astype(F32) * q_dec).astype(BF16), r.astype(BF16))
    o_ref[0, 0] = inner + cross
    r_scr[...] = c_dec * r + _dot_tn((k.astype(F32) * k_dec).astype(BF16), v)


def _retention(p, log_g, ctx):
    bsz, L, n = p.shape
    d = n // 6
    dk = d // RET_HEADS
    dv = 2 * d // RET_HEADS
    C = RET_CHUNK
    nc, ncc = L // C, ctx // C

    def chunk(dr, c):
        back = jnp.where(c < ncc, ncc - 1 - c, nc + ncc - 1 - c)
        return jnp.where(dr == 0, c, back)

    lg = jnp.broadcast_to(log_g.reshape(2 * RET_HEADS, 1, 1), (2 * RET_HEADS, 1, LANES))
    return pl.pallas_call(
        _ret_kernel,
        grid=(bsz, RET_HEADS, 2, nc),
        in_specs=[
            pl.BlockSpec((1, 1, LANES), lambda b, h, dr, c: (dr * RET_HEADS + h, 0, 0)),
            pl.BlockSpec((1, C, dk), lambda b, h, dr, c: (b, chunk(dr, c), h)),
            pl.BlockSpec((1, C, dk), lambda b, h, dr, c: (b, chunk(dr, c), RET_HEADS + h)),
            pl.BlockSpec((1, C, dv), lambda b, h, dr, c: (b, chunk(dr, c), RET_HEADS + h)),
        ],
        out_specs=pl.BlockSpec((1, 1, C, dv), lambda b, h, dr, c: (dr, b, chunk(dr, c), h)),
        out_shape=jax.ShapeDtypeStruct((2, bsz, L, 2 * d), F32),
        scratch_shapes=[pltpu.VMEM((dk, dv), F32)],
        compiler_params=_cparams("parallel", "parallel", "arbitrary", "arbitrary"),
    )(lg, p, p, p)


def _normmod_kernel(x_ref, mod_ref, g_ref, o_ref):
    mod = mod_ref[0]
    o_ref[0] = (_rms(x_ref[0], g_ref[0:1, :]) * (1.0 + mod[1:2, :]) + mod[0:1, :]).astype(BF16)


def _normmod(hs, mod_l, gains, nctx_tiles):
    bsz, L, d = hs.shape
    return pl.pallas_call(
        _normmod_kernel,
        grid=(bsz, L // TM),
        in_specs=[
            pl.BlockSpec((1, TM, d), lambda b, i: (b, i, 0)),
            _mod_spec(d, bsz, nctx_tiles),
            pl.BlockSpec(gains.shape, lambda b, i: (0, 0)),
        ],
        out_specs=pl.BlockSpec((1, TM, d), lambda b, i: (b, i, 0)),
        out_shape=jax.ShapeDtypeStruct((bsz, L, d), BF16),
        compiler_params=_cparams("parallel", "parallel"),
    )(hs, mod_l, gains)


def _s5_kernel(x_ref, t_ref, wb_ref, wc_ref, a_ref, y_ref, st_scr, *, nctx):
    gb = x_ref.shape[1]
    nch = x_ref.shape[2]
    for gi in range(gb):
        x = x_ref[0, gi]
        for dr in range(2):
            s = _dot(x, wb_ref[gi, dr])
            st_scr[2 * dr, pl.ds(gi, nch, stride=gb), :] = s[:, :LANES]
            st_scr[2 * dr + 1, pl.ds(gi, nch, stride=gb), :] = s[:, LANES:]

    a = a_ref[...]

    def step(dr, row, h_re, h_im):
        s_re = st_scr[2 * dr, pl.ds(row, gb), :]
        s_im = st_scr[2 * dr + 1, pl.ds(row, gb), :]
        st_scr[2 * dr, pl.ds(row, gb), :] = h_re
        st_scr[2 * dr + 1, pl.ds(row, gb), :] = h_im
        a_re, a_im = a[dr, 0], a[dr, 1]
        return a_re * h_re - a_im * h_im + s_re, a_re * h_im + a_im * h_re + s_im

    def body(t, carry):
        f_re, f_im, b_re, b_im = carry
        cb = jnp.where(t < nctx, nctx - 1 - t, nch + nctx - 1 - t)
        f_re, f_im = step(0, pl.multiple_of(t * gb, gb), f_re, f_im)
        b_re, b_im = step(1, pl.multiple_of(cb * gb, gb), b_re, b_im)
        return f_re, f_im, b_re, b_im

    z = jnp.zeros((gb, LANES), F32)
    lax.fori_loop(0, nch, body, (z, z, z, z))

    for gi in range(gb):
        y = _dot(x_ref[0, gi], t_ref[gi])
        for dr in range(2):
            h = jnp.concatenate([st_scr[2 * dr, pl.ds(gi, nch, stride=gb), :],
                                 st_scr[2 * dr + 1, pl.ds(gi, nch, stride=gb), :]], axis=1)
            y = y + _dot(h.astype(BF16), wc_ref[gi, dr])
        y_ref[0, gi] = y


def _s5_weights(lam_re, lam_im, log_dt, b_re, b_im, c_re, c_im):
    n = SSM_CHUNK
    lam = lax.complex(lam_re.astype(F32), lam_im.astype(F32))
    ldt = lam * jnp.exp(log_dt.astype(F32))[..., None]
    lam_bar = jnp.exp(ldt)
    b_bar = ((lam_bar - 1.0) / lam)[..., None] * lax.complex(b_re.astype(F32), b_im.astype(F32))
    c_mat = lax.complex(c_re.astype(F32), c_im.astype(F32))
    steps = jnp.arange(n + 1, dtype=F32)
    pw = jnp.exp(ldt[None] * steps[:, None, None, None])
    G, P, Hg = b_bar.shape[1:]

    kern = jnp.real(jnp.einsum('kdgp,dgjp,dgpi->dkgij', pw[:n], c_mat, b_bar))
    s_idx = jnp.arange(n)[:, None]
    t_idx = jnp.arange(n)[None, :]
    lag = t_idx - s_idx
    kf = jnp.where((lag >= 0)[..., None, None, None], kern[0][jnp.clip(lag, 0, n - 1)], 0.0)
    kb = jnp.where((lag <= 0)[..., None, None, None], kern[1][jnp.clip(-lag, 0, n - 1)], 0.0)
    t_mat = (kf + kb).transpose(2, 0, 3, 1, 4).reshape(G, n * Hg, n * Hg)

    zpad = jnp.zeros((G, n * Hg, LANES - P), F32)
    zrow = jnp.zeros((G, LANES - P, n * Hg), F32)
    wbs, wcs = [], []
    for dr in range(2):
        e_in = pw[:n][::-1, dr] if dr == 0 else pw[:n, dr]
        wb = (e_in[:, :, :, None] * b_bar[dr][None]).transpose(1, 0, 3, 2).reshape(G, n * Hg, P)
        wbs.append(jnp.concatenate([jnp.real(wb), zpad, jnp.imag(wb), zpad], axis=2))
        e_out = pw[1:, dr] if dr == 0 else pw[1:][::-1, dr]
        wc = (c_mat[dr][None] * e_out[:, :, None, :]).transpose(1, 3, 0, 2).reshape(G, P, n * Hg)
        wcs.append(jnp.concatenate([jnp.real(wc), zrow, -jnp.imag(wc), zrow], axis=1))
    w_b = jnp.stack(wbs, axis=1)
    w_c = jnp.stack(wcs, axis=1)
    a_n = pw[n]
    pad = jnp.zeros((2, G, LANES - P), F32)
    a = jnp.stack([jnp.concatenate([jnp.real(a_n), pad], -1),
                   jnp.concatenate([jnp.imag(a_n), pad], -1)], axis=1)
    return t_mat.astype(BF16), w_b.astype(BF16), w_c.astype(BF16), a


def _s5_core(u, weights, ctx):
    bsz, L, d = u.shape
    n, hg, gb = SSM_CHUNK, SSM_GROUP, SSM_GROUP_BLOCK
    G = d // hg
    nch = L // n
    t_mat, w_b, w_c, a = weights
    x = u.reshape(bsz, nch, n, G, hg).transpose(0, 3, 1, 2, 4).reshape(bsz, G, nch, n * hg)
    y = pl.pallas_call(
        functools.partial(_s5_kernel, nctx=ctx // n),
        grid=(bsz, G // gb),
        in_specs=[
            pl.BlockSpec((1, gb, nch, n * hg), lambda b, g: (b, g, 0, 0)),
            pl.BlockSpec((gb, n * hg, n * hg), lambda b, g: (g, 0, 0)),
            pl.BlockSpec((gb, 2, n * hg, 2 * LANES), lambda b, g: (g, 0, 0, 0)),
            pl.BlockSpec((gb, 2, 2 * LANES, n * hg), lambda b, g: (g, 0, 0, 0)),
            pl.BlockSpec((2, 2, gb, LANES), lambda b, g: (0, 0, g, 0)),
        ],
        out_specs=pl.BlockSpec((1, gb, nch, n * hg), lambda b, g: (b, g, 0, 0)),
        out_shape=jax.ShapeDtypeStruct((bsz, G, nch, n * hg), F32),
        scratch_shapes=[pltpu.VMEM((4, nch * gb, LANES), F32)],
        compiler_params=_cparams("parallel", "parallel"),
    )(x, t_mat, w_b, w_c, a)
    return y.reshape(bsz, G, nch, n, hg).transpose(0, 2, 3, 1, 4).reshape(bsz, L, d)


def _finish(h, y, mod, g_ref, o_ref):
    o_ref[0] = h + mod[2:3, :] * _rms(y, g_ref[1:2, :])


def _attn_out_kernel(h_ref, mod_ref, g_ref, o_in_ref, w_ref, o_ref):
    _finish(h_ref[0], _dot(o_in_ref[0], w_ref[...]), mod_ref[0], g_ref, o_ref)


def _ret_out_kernel(h_ref, mod_ref, g_ref, o_in_ref, gate_ref, w_ref, o_ref):
    o = o_in_ref[0, 0] + o_in_ref[1, 0]
    dv = o.shape[1] // RET_HEADS
    parts = []
    for hd in range(RET_HEADS):
        oh = o[:, hd * dv:(hd + 1) * dv]
        parts.append(oh * lax.rsqrt(jnp.mean(oh * oh, axis=-1, keepdims=True) + NORM_EPS))
    o = jnp.concatenate(parts, axis=1)
    y = _dot((_silu(gate_ref[0].astype(F32)) * o).astype(BF16), w_ref[...])
    _finish(h_ref[0], y, mod_ref[0], g_ref, o_ref)


def _s5_out_kernel(h_ref, mod_ref, g_ref, y_ref, dskip_ref, w_ref, o_ref):
    h = h_ref[0]
    mod = mod_ref[0]
    d = h.shape[1]
    u = _rms(h, g_ref[0:1, :]) * (1.0 + mod[1:2, :]) + mod[0:1, :]
    z = y_ref[0] + dskip_ref[...] * u
    gl = 0.5 * z * (1.0 + jnp.tanh(math.sqrt(2.0 / math.pi) * (z + 0.044715 * (z * z * z))))
    gl = gl.astype(BF16)
    y = _dot(gl, w_ref[:, :d]) * jax.nn.sigmoid(_dot(gl, w_ref[:, d:]))
    _finish(h, y, mod, g_ref, o_ref)


def _mixer_out(kernel, hs, mod_l, gains, nctx_tiles, extra, extra_specs):
    bsz, L, d = hs.shape
    return pl.pallas_call(
        kernel,
        grid=(bsz, L // TM),
        in_specs=[
            pl.BlockSpec((1, TM, d), lambda b, i: (b, i, 0)),
            _mod_spec(d, bsz, nctx_tiles),
            pl.BlockSpec(gains.shape, lambda b, i: (0, 0)),
        ] + extra_specs,
        out_specs=pl.BlockSpec((1, TM, d), lambda b, i: (b, i, 0)),
        out_shape=jax.ShapeDtypeStruct((bsz, L, d), F32),
        compiler_params=_cparams("parallel", "parallel"),
    )(hs, mod_l, gains, *extra)


def _full_spec(a):
    nd = a.ndim
    return pl.BlockSpec(a.shape, lambda b, i: (0,) * nd)


def _ffn_kernel(h_ref, mod_ref, g_ref, wg_ref, wu_ref, wd_ref, o_ref):
    h = h_ref[0]
    mod = mod_ref[0]
    v = (_rms(h, g_ref[2:3, :]) * (1.0 + mod[4:5, :]) + mod[3:4, :]).astype(BF16)
    act = (_silu(_dot(v, wg_ref[...])) * _dot(v, wu_ref[...])).astype(BF16)
    y = _dot(act, wd_ref[...])
    o_ref[0] = h + mod[5:6, :] * _rms(y, g_ref[3:4, :])


def _ffn(hs, mod_l, gains, wg, wu, wd, nctx_tiles):
    bsz, L, d = hs.shape
    return pl.pallas_call(
        _ffn_kernel,
        grid=(bsz, L // TM),
        in_specs=[
            pl.BlockSpec((1, TM, d), lambda b, i: (b, i, 0)),
            _mod_spec(d, bsz, nctx_tiles),
            pl.BlockSpec(gains.shape, lambda b, i: (0, 0)),
            _full_spec(wg), _full_spec(wu), _full_spec(wd),
        ],
        out_specs=pl.BlockSpec((1, TM, d), lambda b, i: (b, i, 0)),
        out_shape=jax.ShapeDtypeStruct((bsz, L, d), F32),
        compiler_params=_cparams("parallel", "parallel"),
    )(hs, mod_l, gains, wg, wu, wd)


def kernel(x, c, ctx, c_ctx, mod_w, mod_b, norm_g, attn_w_in, attn_w_out, attn_lambda, attn_subln,
           ret_w_in, ret_w_out, ret_decay_logit, ssm_lambda_re, ssm_lambda_im, ssm_log_dt,
           ssm_b_re, ssm_b_im, ssm_c_re, ssm_c_im, ssm_d, ssm_w_glu, ffn_w_gate, ffn_w_up, ffn_w_down):
    bsz, seq, d = x.shape
    nctx = ctx.shape[1]
    depth = mod_w.shape[0]
    assert nctx % TM == 0 and seq % TM == 0 and seq % GRID_W == 0
    nct = nctx // TM
    L = nctx + seq

    mods = _modulation(c, c_ctx, mod_w, mod_b)
    hs = jnp.concatenate([ctx, x], axis=1)
    da_dim = d // (2 * DA_HEADS)
    dk = d // RET_HEADS
    tab_a = _rope_table(seq, nctx, da_dim)
    tab_r = _rope_table(seq, nctx, dk)

    def tok_spec(width, col=0):
        return pl.BlockSpec((1, TM, width), lambda b, i: (b, i, col))

    for l in range(depth):
        kind, j = l % 3, l // 3
        mod_l, gains = mods[l], norm_g[l]
        if kind == 0:
            lam_init = 0.8 - 0.6 * math.exp(-0.3 * l)
            chunks = ((0, d, da_dim ** -0.5, True), (d, d, 1.0, True), (2 * d, d, 1.0, False))
            p = _proj_in(hs, mod_l, gains, attn_w_in[j].astype(BF16), tab_a, chunks, da_dim // 4, nct)
            o = _attention(p, attn_lambda[j], attn_subln[j], nctx, lam_init)
            w_out = attn_w_out[j].astype(BF16)
            hs = _mixer_out(_attn_out_kernel, hs, mod_l, gains, nct, (o, w_out),
                            [tok_spec(d), _full_spec(w_out)])
        elif kind == 1:
            chunks = ((0, d, 1.0, True), (d, d, dk ** -0.5, True)) + tuple(
                (c0, d, 1.0, False) for c0 in range(2 * d, 6 * d, d))
            p = _proj_in(hs, mod_l, gains, ret_w_in[j].astype(BF16), tab_r, chunks, dk // 4, nct)
            log_g = jax.nn.log_sigmoid(ret_decay_logit[j].astype(F32))
            o = _retention(p, log_g, nctx)
            w_out = ret_w_out[j].astype(BF16)
            hs = _mixer_out(_ret_out_kernel, hs, mod_l, gains, nct, (o, p, w_out),
                            [pl.BlockSpec((2, 1, TM, 2 * d), lambda b, i: (0, b, i, 0)),
                             tok_spec(2 * d, 2), _full_spec(w_out)])
        else:
            u = _normmod(hs, mod_l, gains, nct)
            weights = _s5_weights(ssm_lambda_re[j], ssm_lambda_im[j], ssm_log_dt[j], ssm_b_re[j],
                                  ssm_b_im[j], ssm_c_re[j], ssm_c_im[j])
            y = _s5_core(u, weights, nctx)
            w_glu = ssm_w_glu[j].astype(BF16)
            dskip = ssm_d[j].astype(F32).reshape(1, d)
            hs = _mixer_out(_s5_out_kernel, hs, mod_l, gains, nct, (y, dskip, w_glu),
                            [tok_spec(d), _full_spec(dskip), _full_spec(w_glu)])
        hs = _ffn(hs, mod_l, gains, ffn_w_gate[l].astype(BF16), ffn_w_up[l].astype(BF16),
                  ffn_w_down[l].astype(BF16), nct)
    return hs[:, nctx:, :]
```

```python
import functools
import math

import jax
import jax.numpy as jnp
from jax import lax
from jax.experimental import pallas as pl
from jax.experimental.pallas import tpu as pltpu

F32 = jnp.float32
BF16 = jnp.bfloat16

GRID_W = 64
DA_HEADS = 8
RET_HEADS = 4
RET_CHUNK = 128
SSM_GROUP = 16
SSM_STATE = 64
SSM_CHUNK = 16
SSM_GROUP_BLOCK = 8
ROPE_THETA = 10000.0
NORM_EPS = 1e-6
LANES = 128
TM = 256
VMEM_LIMIT = 56 * 1024 * 1024


def _cparams(*sem):
    return pltpu.CompilerParams(dimension_semantics=sem, vmem_limit_bytes=VMEM_LIMIT)


def _rms(x, gain):
    return x * lax.rsqrt(jnp.mean(x * x, axis=-1, keepdims=True) + NORM_EPS) * gain


def _silu(x):
    return x * jax.nn.sigmoid(x)


def _dot(a, b):
    return jnp.dot(a, b, preferred_element_type=F32)


def _dot_nt(a, b):
    return lax.dot_general(a, b, (((1,), (1,)), ((), ())), preferred_element_type=F32)


def _dot_tn(a, b):
    return lax.dot_general(a, b, (((0,), (0,)), ((), ())), preferred_element_type=F32)


def _mod_kernel(s_ref, w_ref, b_ref, o_ref):
    s = _silu(s_ref[...])
    o_ref[0] = jnp.dot(s, w_ref[0], preferred_element_type=F32,
                       precision=lax.Precision.HIGHEST) + b_ref[0]


def _modulation(c, c_ctx, mod_w, mod_b):
    depth, d, n = mod_w.shape
    bsz = c.shape[0]
    rows = 8
    s = jnp.zeros((rows, d), F32).at[:bsz].set(c).at[bsz].set(c_ctx)
    tn = 1536
    out = pl.pallas_call(
        _mod_kernel,
        grid=(depth, n // tn),
        in_specs=[
            pl.BlockSpec((rows, d), lambda l, j: (0, 0)),
            pl.BlockSpec((1, d, tn), lambda l, j: (l, 0, j)),
            pl.BlockSpec((1, 1, tn), lambda l, j: (l, 0, j)),
        ],
        out_specs=pl.BlockSpec((1, rows, tn), lambda l, j: (l, 0, j)),
        out_shape=jax.ShapeDtypeStruct((depth, rows, n), F32),
        compiler_params=_cparams("parallel", "parallel"),
    )(s, mod_w, mod_b.reshape(depth, 1, n))
    return out.reshape(depth, rows, 6, d)


def _mod_spec(d, bsz, nctx_tiles):
    return pl.BlockSpec((1, 6, d), lambda b, i: (jnp.where(i < nctx_tiles, bsz, b), 0, 0))


def _rope_angles(seq, head_dim):
    quarter = head_dim // 4
    freqs = ROPE_THETA ** (-jnp.arange(quarter, dtype=F32) / quarter)
    rows = seq // GRID_W
    row = jnp.repeat(jnp.arange(rows, dtype=F32), GRID_W)
    col = jnp.tile(jnp.arange(GRID_W, dtype=F32), rows)
    ang_r = row[:, None] * freqs
    ang_c = col[:, None] * freqs
    return jnp.concatenate([ang_r, ang_r, ang_c, ang_c], axis=-1)


def _rope_table(seq, ctx, head_dim):
    ang = _rope_angles(seq, head_dim)
    ang = jnp.concatenate([jnp.zeros((ctx, head_dim), F32), ang], axis=0)
    cos, sin = jnp.cos(ang), jnp.sin(ang)
    q = head_dim // 4
    first = (jnp.arange(head_dim) % (2 * q)) < q
    s_minus = jnp.where(first, -sin, 0.0)
    s_plus = jnp.where(first, 0.0, sin)
    rep = max(1, LANES // head_dim)
    return jnp.concatenate([jnp.tile(t, (1, rep)) for t in (cos, s_minus, s_plus)], axis=1)


def _apply_rope(y, tab, quarter):
    w = tab.shape[1] // 3
    cos, s_minus, s_plus = tab[:, :w], tab[:, w:2 * w], tab[:, 2 * w:]
    outs = []
    for k in range(y.shape[1] // LANES):
        yk = y[:, k * LANES:(k + 1) * LANES]
        t0 = (k * LANES) % w
        ck, mk, pk = (t[:, t0:t0 + LANES] for t in (cos, s_minus, s_plus))
        if 2 * quarter == LANES:
            outs.append(yk * ck + pltpu.roll(yk, quarter, 1) * (mk + pk))
        else:
            outs.append(yk * ck + pltpu.roll(yk, LANES - quarter, 1) * mk
                        + pltpu.roll(yk, quarter, 1) * pk)
    return jnp.concatenate(outs, axis=1)


def _proj_in_kernel(x_ref, mod_ref, g_ref, w_ref, tab_ref, o_ref, *, chunks, quarter):
    mod = mod_ref[0]
    u = (_rms(x_ref[0], g_ref[0:1, :]) * (1.0 + mod[1:2, :]) + mod[0:1, :]).astype(BF16)
    tab = tab_ref[...]
    for c0, width, mult, rope in chunks:
        y = _dot(u, w_ref[:, c0:c0 + width])
        if mult != 1.0:
            y = y * mult
        if rope:
            y = _apply_rope(y, tab, quarter)
        o_ref[0, :, c0:c0 + width] = y.astype(BF16)


def _proj_in(hs, mod_l, gains, w, tab, chunks, quarter, nctx_tiles):
    bsz, L, d = hs.shape
    n = w.shape[1]
    return pl.pallas_call(
        functools.partial(_proj_in_kernel, chunks=chunks, quarter=quarter),
        grid=(bsz, L // TM),
        in_specs=[
            pl.BlockSpec((1, TM, d), lambda b, i: (b, i, 0)),
            _mod_spec(d, bsz, nctx_tiles),
            pl.BlockSpec(gains.shape, lambda b, i: (0, 0)),
            pl.BlockSpec(w.shape, lambda b, i: (0, 0)),
            pl.BlockSpec((TM, tab.shape[1]), lambda b, i: (i, 0)),
        ],
        out_specs=pl.BlockSpec((1, TM, n), lambda b, i: (b, i, 0)),
        out_shape=jax.ShapeDtypeStruct((bsz, L, n), BF16),
        compiler_params=_cparams("parallel", "parallel"),
    )(hs, mod_l, gains, w, tab)


def _attn_kernel(lam_ref, sub_ref, qt_ref, k_ref, vtc_ref, vtx_ref, o_ref,
                 s_c, s_a, s_b, m_scr, l_scr, acc_scr, *, nb, lam_init):
    ctx = s_c.shape[1]
    tk = s_a.shape[1]
    qt = qt_ref[0]
    row = lax.broadcasted_iota(jnp.int32, qt.shape, 0)
    zero = jnp.zeros_like(qt)
    qts = (jnp.where(row < LANES // 2, qt, zero), jnp.where(row >= LANES // 2, qt, zero))

    def scores(kb, s_out):
        for part in range(2):
            s_out[part] = _dot(kb, qts[part])

    def latent_keys(t):
        return k_ref[0, pl.ds(pl.multiple_of(ctx + (t - 1) * tk, LANES), tk), :]

    def stage(s_in, vt, nxt):
        if nxt is not None:
            scores(*nxt)
        for part in range(2):
            s = s_in[part]
            m = m_scr[part]
            m_new = jnp.maximum(m, jnp.max(s, axis=0, keepdims=True))
            alpha = jnp.exp2(m - m_new)
            p = jnp.exp2(s - m_new)
            l_scr[part] = alpha * l_scr[part] + jnp.sum(p, axis=0, keepdims=True)
            m_scr[part] = m_new
            acc_scr[part] = alpha * acc_scr[part] + _dot(vt, p.astype(BF16))

    m_scr[...] = jnp.full(m_scr.shape, -1e30, F32)
    l_scr[...] = jnp.zeros(l_scr.shape, F32)
    acc_scr[...] = jnp.zeros(acc_scr.shape, F32)

    bufs = (s_b, s_a)
    scores(k_ref[0, 0:ctx, :], s_c)
    stage(s_c, vtc_ref[0, 0], (latent_keys(1), s_a) if nb >= 1 else None)
    n_pairs = max(0, (nb - 1) // 2 if nb % 2 else (nb - 2) // 2)

    def pair(u, carry):
        t = 2 * u + 1
        stage(s_a, vtx_ref[0, 0, t - 1], (latent_keys(t + 1), s_b))
        stage(s_b, vtx_ref[0, 0, t], (latent_keys(t + 2), s_a))
        return carry

    if n_pairs > 0:
        lax.fori_loop(0, n_pairs, pair, 0)
    for t in range(2 * n_pairs + 1, nb + 1):
        stage(bufs[t % 2], vtx_ref[0, 0, t - 1],
              (latent_keys(t + 1), bufs[(t + 1) % 2]) if t < nb else None)

    lv = lam_ref[...]
    lam = (jnp.exp(jnp.sum(lv[0:1] * lv[1:2], axis=-1, keepdims=True))
           - jnp.exp(jnp.sum(lv[2:3] * lv[3:4], axis=-1, keepdims=True)) + lam_init)
    o = (acc_scr[0] / l_scr[0] - lam * (acc_scr[1] / l_scr[1])).T
    o = _rms(o, sub_ref[...]) * (1.0 - lam_init)
    o_ref[0] = o.astype(BF16)


def _attention_call(q_t, p, vt_ctx, vt_lat, lam_vec, subln, nb, lam_init, q_off, n_out):
    bsz, d, _ = q_t.shape
    ctx = vt_ctx.shape[3]
    ncol = d // LANES
    tk = vt_lat.shape[4]
    k_rows = p.shape[1] if nb else ctx
    return pl.pallas_call(
        functools.partial(_attn_kernel, nb=nb, lam_init=lam_init),
        grid=(bsz, DA_HEADS, n_out // TM),
        in_specs=[
            pl.BlockSpec(lam_vec.shape, lambda b, h, i: (0, 0)),
            pl.BlockSpec((1, LANES), lambda b, h, i: (0, 0)),
            pl.BlockSpec((1, LANES, TM), lambda b, h, i: (b, h, i + q_off // TM)),
            pl.BlockSpec((1, k_rows, LANES), lambda b, h, i: (b, 0, ncol + h)),
            pl.BlockSpec((1, 1, LANES, ctx), lambda b, h, i: (b, h, 0, 0)),
            pl.BlockSpec((1, 1) + vt_lat.shape[2:], lambda b, h, i: (b, h, 0, 0, 0)),
        ],
        out_specs=pl.BlockSpec((1, TM, LANES), lambda b, h, i: (b, i, h)),
        out_shape=jax.ShapeDtypeStruct((bsz, n_out, d), BF16),
        scratch_shapes=[pltpu.VMEM((2, ctx, TM), F32), pltpu.VMEM((2, tk, TM), F32),
                        pltpu.VMEM((2, tk, TM), F32), pltpu.VMEM((2, 1, TM), F32),
                        pltpu.VMEM((2, 1, TM), F32), pltpu.VMEM((2, LANES, TM), F32)],
        compiler_params=_cparams("parallel", "parallel", "arbitrary"),
    )(lam_vec, subln.reshape(1, LANES), q_t, p, vt_ctx, vt_lat)


def _attention(p, lam_vec, subln, ctx, lam_init):
    bsz, L, n = p.shape
    d = n // 3
    seq = L - ctx
    tk = min(1024, seq)
    nb = seq // tk
    q_t = p[:, :, :d].transpose(0, 2, 1)
    v = p[:, :, 2 * d:]
    vt_ctx = v[:, :ctx].reshape(bsz, ctx, DA_HEADS, LANES).transpose(0, 2, 3, 1)
    vt_lat = v[:, ctx:].reshape(bsz, nb, tk, DA_HEADS, LANES).transpose(0, 3, 1, 4, 2)
    o_ctx = _attention_call(q_t, p, vt_ctx, vt_lat, lam_vec, subln, 0, lam_init, 0, ctx)
    o_lat = _attention_call(q_t, p, vt_ctx, vt_lat, lam_vec, subln, nb, lam_init, ctx, seq)
    return jnp.concatenate([o_ctx, o_lat], axis=1)


def _ret_kernel(lg_ref, q_ref, k_ref, v_ref, o_ref, r_scr):
    d = pl.program_id(2)
    c = pl.program_id(3)
    C = q_ref.shape[1]

    @pl.when(c == 0)
    def _():
        r_scr[...] = jnp.zeros_like(r_scr)

    lg = lg_ref[0][:, 0:1]
    fwd = d == 0
    ii = lax.broadcasted_iota(jnp.int32, (C, C), 0)
    jj = lax.broadcasted_iota(jnp.int32, (C, C), 1)
    dd = jnp.where(fwd, ii - jj, jj - ii)
    decay = jnp.where(dd >= 0, jnp.exp(lg * jnp.maximum(dd, 0).astype(F32)), 0.0)
    idx = lax.broadcasted_iota(jnp.int32, (C, 1), 0)
    q_dec = jnp.exp(lg * jnp.where(fwd, idx + 1, C - idx).astype(F32))
    k_dec = jnp.exp(lg * jnp.where(fwd, C - 1 - idx, idx).astype(F32))
    c_dec = jnp.exp(lg * float(C))

    q = q_ref[0]
    k = k_ref[0]
    v = v_ref[0]
    r = r_scr[...]
    s = _dot_nt(q, k) * decay
    inner = _dot(s.astype(BF16), v)
    cross = _dot((q.astype(F32) * q_dec).astype(BF16), r.astype(BF16))
    o_ref[0, 0] = inner + cross
    r_scr[...] = c_dec * r + _dot_tn((k.astype(F32) * k_dec).astype(BF16), v)


def _retention(p, log_g, ctx):
    bsz, L, n = p.shape
    d = n // 6
    dk = d // RET_HEADS
    dv = 2 * d // RET_HEADS
    C = RET_CHUNK
    nc, ncc = L // C, ctx // C

    def chunk(dr, c):
        back = jnp.where(c < ncc, ncc - 1 - c, nc + ncc - 1 - c)
        return jnp.where(dr == 0, c, back)

    lg = jnp.broadcast_to(log_g.reshape(2 * RET_HEADS, 1, 1), (2 * RET_HEADS, 1, LANES))
    return pl.pallas_call(
        _ret_kernel,
        grid=(bsz, RET_HEADS, 2, nc),
        in_specs=[
            pl.BlockSpec((1, 1, LANES), lambda b, h, dr, c: (dr * RET_HEADS + h, 0, 0)),
            pl.BlockSpec((1, C, dk), lambda b, h, dr, c: (b, chunk(dr, c), h)),
            pl.BlockSpec((1, C, dk), lambda b, h, dr, c: (b, chunk(dr, c), RET_HEADS + h)),
            pl.BlockSpec((1, C, dv), lambda b, h, dr, c: (b, chunk(dr, c), RET_HEADS + h)),
        ],
        out_specs=pl.BlockSpec((1, 1, C, dv), lambda b, h, dr, c: (dr, b, chunk(dr, c), h)),
        out_shape=jax.ShapeDtypeStruct((2, bsz, L, 2 * d), F32),
        scratch_shapes=[pltpu.VMEM((dk, dv), F32)],
        compiler_params=_cparams("parallel", "parallel", "arbitrary", "arbitrary"),
    )(lg, p, p, p)


def _normmod_kernel(x_ref, mod_ref, g_ref, o_ref):
    mod = mod_ref[0]
    o_ref[0] = (_rms(x_ref[0], g_ref[0:1, :]) * (1.0 + mod[1:2, :]) + mod[0:1, :]).astype(BF16)


def _normmod(hs, mod_l, gains, nctx_tiles):
    bsz, L, d = hs.shape
    return pl.pallas_call(
        _normmod_kernel,
        grid=(bsz, L // TM),
        in_specs=[
            pl.BlockSpec((1, TM, d), lambda b, i: (b, i, 0)),
            _mod_spec(d, bsz, nctx_tiles),
            pl.BlockSpec(gains.shape, lambda b, i: (0, 0)),
        ],
        out_specs=pl.BlockSpec((1, TM, d), lambda b, i: (b, i, 0)),
        out_shape=jax.ShapeDtypeStruct((bsz, L, d), BF16),
        compiler_params=_cparams("parallel", "parallel"),
    )(hs, mod_l, gains)


def _s5_kernel(x_ref, t_ref, wb_ref, wc_ref, a_ref, y_ref, st_scr, *, nctx):
    gb = x_ref.shape[1]
    nch = x_ref.shape[2]
    for gi in range(gb):
        x = x_ref[0, gi]
        for dr in range(2):
            s = _dot(x, wb_ref[gi, dr])
            st_scr[2 * dr, pl.ds(gi, nch, stride=gb), :] = s[:, :LANES]
            st_scr[2 * dr + 1, pl.ds(gi, nch, stride=gb), :] = s[:, LANES:]

    a = a_ref[...]

    def step(dr, row, h_re, h_im):
        s_re = st_scr[2 * dr, pl.ds(row, gb), :]
        s_im = st_scr[2 * dr + 1, pl.ds(row, gb), :]
        st_scr[2 * dr, pl.ds(row, gb), :] = h_re
        st_scr[2 * dr + 1, pl.ds(row, gb), :] = h_im
        a_re, a_im = a[dr, 0], a[dr, 1]
        return a_re * h_re - a_im * h_im + s_re, a_re * h_im + a_im * h_re + s_im

    def body(t, carry):
        f_re, f_im, b_re, b_im = carry
        cb = jnp.where(t < nctx, nctx - 1 - t, nch + nctx - 1 - t)
        f_re, f_im = step(0, pl.multiple_of(t * gb, gb), f_re, f_im)
        b_re, b_im = step(1, pl.multiple_of(cb * gb, gb), b_re, b_im)
        return f_re, f_im, b_re, b_im

    z = jnp.zeros((gb, LANES), F32)
    lax.fori_loop(0, nch, body, (z, z, z, z))

    for gi in range(gb):
        y = _dot(x_ref[0, gi], t_ref[gi])
        for dr in range(2):
            h = jnp.concatenate([st_scr[2 * dr, pl.ds(gi, nch, stride=gb), :],
                                 st_scr[2 * dr + 1, pl.ds(gi, nch, stride=gb), :]], axis=1)
            y = y + _dot(h.astype(BF16), wc_ref[gi, dr])
        y_ref[0, gi] = y


def _s5_weights(lam_re, lam_im, log_dt, b_re, b_im, c_re, c_im):
    n = SSM_CHUNK
    lam = lax.complex(lam_re.astype(F32), lam_im.astype(F32))
    ldt = lam * jnp.exp(log_dt.astype(F32))[..., None]
    lam_bar = jnp.exp(ldt)
    b_bar = ((lam_bar - 1.0) / lam)[..., None] * lax.complex(b_re.astype(F32), b_im.astype(F32))
    c_mat = lax.complex(c_re.astype(F32), c_im.astype(F32))
    steps = jnp.arange(n + 1, dtype=F32)
    pw = jnp.exp(ldt[None] * steps[:, None, None, None])
    G, P, Hg = b_bar.shape[1:]

    kern = jnp.real(jnp.einsum('kdgp,dgjp,dgpi->dkgij', pw[:n], c_mat, b_bar))
    s_idx = jnp.arange(n)[:, None]
    t_idx = jnp.arange(n)[None, :]
    lag = t_idx - s_idx
    kf = jnp.where((lag >= 0)[..., None, None, None], kern[0][jnp.clip(lag, 0, n - 1)], 0.0)
    kb = jnp.where((lag <= 0)[..., None, None, None], kern[1][jnp.clip(-lag, 0, n - 1)], 0.0)
    t_mat = (kf + kb).transpose(2, 0, 3, 1, 4).reshape(G, n * Hg, n * Hg)

    zpad = jnp.zeros((G, n * Hg, LANES - P), F32)
    zrow = jnp.zeros((G, LANES - P, n * Hg), F32)
    wbs, wcs = [], []
    for dr in range(2):
        e_in = pw[:n][::-1, dr] if dr == 0 else pw[:n, dr]
        wb = (e_in[:, :, :, None] * b_bar[dr][None]).transpose(1, 0, 3, 2).reshape(G, n * Hg, P)
        wbs.append(jnp.concatenate([jnp.real(wb), zpad, jnp.imag(wb), zpad], axis=2))
        e_out = pw[1:, dr] if dr == 0 else pw[1:][::-1, dr]
        wc = (c_mat[dr][None] * e_out[:, :, None, :]).transpose(1, 3, 0, 2).reshape(G, P, n * Hg)
        wcs.append(jnp.concatenate([jnp.real(wc), zrow, -jnp.imag(wc), zrow], axis=1))
    w_b = jnp.stack(wbs, axis=1)
    w_c = jnp.stack(wcs, axis=1)
    a_n = pw[n]
    pad = jnp.zeros((2, G, LANES - P), F32)
    a = jnp.stack([jnp.concatenate([jnp.real(a_n), pad], -1),
                   jnp.concatenate([jnp.imag(a_n), pad], -1)], axis=1)
    return t_mat.astype(BF16), w_b.astype(BF16), w_c.astype(BF16), a


def _s5_core(u, weights, ctx):
    bsz, L, d = u.shape
    n, hg, gb = SSM_CHUNK, SSM_GROUP, SSM_GROUP_BLOCK
    G = d // hg
    nch = L // n
    t_mat, w_b, w_c, a = weights
    x = u.reshape(bsz, nch, n, G, hg).transpose(0, 3, 1, 2, 4).reshape(bsz, G, nch, n * hg)
    y = pl.pallas_call(
        functools.partial(_s5_kernel, nctx=ctx // n),
        grid=(bsz, G // gb),
        in_specs=[
            pl.BlockSpec((1, gb, nch, n * hg), lambda b, g: (b, g, 0, 0)),
            pl.BlockSpec((gb, n * hg, n * hg), lambda b, g: (g, 0, 0)),
            pl.BlockSpec((gb, 2, n * hg, 2 * LANES), lambda b, g: (g, 0, 0, 0)),
            pl.BlockSpec((gb, 2, 2 * LANES, n * hg), lambda b, g: (g, 0, 0, 0)),
            pl.BlockSpec((2, 2, gb, LANES), lambda b, g: (0, 0, g, 0)),
        ],
        out_specs=pl.BlockSpec((1, gb, nch, n * hg), lambda b, g: (b, g, 0, 0)),
        out_shape=jax.ShapeDtypeStruct((bsz, G, nch, n * hg), F32),
        scratch_shapes=[pltpu.VMEM((4, nch * gb, LANES), F32)],
        compiler_params=_cparams("parallel", "parallel"),
    )(x, t_mat, w_b, w_c, a)
    return y.reshape(bsz, G, nch, n, hg).transpose(0, 2, 3, 1, 4).reshape(bsz, L, d)


def _finish(h, y, mod, g_ref, o_ref):
    o_ref[0] = h + mod[2:3, :] * _rms(y, g_ref[1:2, :])


def _attn_out_kernel(h_ref, mod_ref, g_ref, o_in_ref, w_ref, o_ref):
    _finish(h_ref[0], _dot(o_in_ref[0], w_ref[...]), mod_ref[0], g_ref, o_ref)


def _ret_out_kernel(h_ref, mod_ref, g_ref, o_in_ref, gate_ref, w_ref, o_ref):
    o = o_in_ref[0, 0] + o_in_ref[1, 0]
    dv = o.shape[1] // RET_HEADS
    parts = []
    for hd in range(RET_HEADS):
        oh = o[:, hd * dv:(hd + 1) * dv]
        parts.append(oh * lax.rsqrt(jnp.mean(oh * oh, axis=-1, keepdims=True) + NORM_EPS))
    o = jnp.concatenate(parts, axis=1)
    y = _dot((_silu(gate_ref[0].astype(F32)) * o).astype(BF16), w_ref[...])
    _finish(h_ref[0], y, mod_ref[0], g_ref, o_ref)


def _s5_out_kernel(h_ref, mod_ref, g_ref, y_ref, dskip_ref, w_ref, o_ref):
    h = h_ref[0]
    mod = mod_ref[0]
    d = h.shape[1]
    u = _rms(h, g_ref[0:1, :]) * (1.0 + mod[1:2, :]) + mod[0:1, :]
    z = y_ref[0] + dskip_ref[...] * u
    gl = 0.5 * z * (1.0 + jnp.tanh(math.sqrt(2.0 / math.pi) * (z + 0.044715 * (z * z * z))))
    gl = gl.astype(BF16)
    y = _dot(gl, w_ref[:, :d]) * jax.nn.sigmoid(_dot(gl, w_ref[:, d:]))
    _finish(h, y, mod, g_ref, o_ref)


def _mixer_out(kernel, hs, mod_l, gains, nctx_tiles, extra, extra_specs):
    bsz, L, d = hs.shape
    return pl.pallas_call(
        kernel,
        grid=(bsz, L // TM),
        in_specs=[
            pl.BlockSpec((1, TM, d), lambda b, i: (b, i, 0)),
            _mod_spec(d, bsz, nctx_tiles),
            pl.BlockSpec(gains.shape, lambda b, i: (0, 0)),
        ] + extra_specs,
        out_specs=pl.BlockSpec((1, TM, d), lambda b, i: (b, i, 0)),
        out_shape=jax.ShapeDtypeStruct((bsz, L, d), F32),
        compiler_params=_cparams("parallel", "parallel"),
    )(hs, mod_l, gains, *extra)


def _full_spec(a):
    nd = a.ndim
    return pl.BlockSpec(a.shape, lambda b, i: (0,) * nd)


def _ffn_kernel(h_ref, mod_ref, g_ref, wg_ref, wu_ref, wd_ref, o_ref):
    h = h_ref[0]
    mod = mod_ref[0]
    v = (_rms(h, g_ref[2:3, :]) * (1.0 + mod[4:5, :]) + mod[3:4, :]).astype(BF16)
    act = (_silu(_dot(v, wg_ref[...])) * _dot(v, wu_ref[...])).astype(BF16)
    y = _dot(act, wd_ref[...])
    o_ref[0] = h + mod[5:6, :] * _rms(y, g_ref[3:4, :])


def _ffn(hs, mod_l, gains, wg, wu, wd, nctx_tiles):
    bsz, L, d = hs.shape
    return pl.pallas_call(
        _ffn_kernel,
        grid=(bsz, L // TM),
        in_specs=[
            pl.BlockSpec((1, TM, d), lambda b, i: (b, i, 0)),
            _mod_spec(d, bsz, nctx_tiles),
            pl.BlockSpec(gains.shape, lambda b, i: (0, 0)),
            _full_spec(wg), _full_spec(wu), _full_spec(wd),
        ],
        out_specs=pl.BlockSpec((1, TM, d), lambda b, i: (b, i, 0)),
        out_shape=jax.ShapeDtypeStruct((bsz, L, d), F32),
        compiler_params=_cparams("parallel", "parallel"),
    )(hs, mod_l, gains, wg, wu, wd)


def kernel(x, c, ctx, c_ctx, mod_w, mod_b, norm_g, attn_w_in, attn_w_out, attn_lambda, attn_subln,
           ret_w_in, ret_w_out, ret_decay_logit, ssm_lambda_re, ssm_lambda_im, ssm_log_dt,
           ssm_b_re, ssm_b_im, ssm_c_re, ssm_c_im, ssm_d, ssm_w_glu, ffn_w_gate, ffn_w_up, ffn_w_down):
    bsz, seq, d = x.shape
    nctx = ctx.shape[1]
    depth = mod_w.shape[0]
    assert nctx % TM == 0 and seq % TM == 0 and seq % GRID_W == 0
    nct = nctx // TM
    L = nctx + seq

    mods = _modulation(c, c_ctx, mod_w, mod_b)
    hs = jnp.concatenate([ctx, x], axis=1)
    da_dim = d // (2 * DA_HEADS)
    dk = d // RET_HEADS
    tab_a = _rope_table(seq, nctx, da_dim)
    tab_r = _rope_table(seq, nctx, dk)

    def tok_spec(width, col=0):
        return pl.BlockSpec((1, TM, width), lambda b, i: (b, i, col))

    for l in range(depth):
        kind, j = l % 3, l // 3
        mod_l, gains = mods[l], norm_g[l]
        if kind == 0:
            lam_init = 0.8 - 0.6 * math.exp(-0.3 * l)
            chunks = ((0, d, da_dim ** -0.5 * math.log2(math.e), True), (d, d, 1.0, True),
                      (2 * d, d, 1.0, False))
            p = _proj_in(hs, mod_l, gains, attn_w_in[j].astype(BF16), tab_a, chunks, da_dim // 4, nct)
            o = _attention(p, attn_lambda[j], attn_subln[j], nctx, lam_init)
            w_out = attn_w_out[j].astype(BF16)
            hs = _mixer_out(_attn_out_kernel, hs, mod_l, gains, nct, (o, w_out),
                            [tok_spec(d), _full_spec(w_out)])
        elif kind == 1:
            chunks = ((0, d, 1.0, True), (d, d, dk ** -0.5, True)) + tuple(
                (c0, d, 1.0, False) for c0 in range(2 * d, 6 * d, d))
            p = _proj_in(hs, mod_l, gains, ret_w_in[j].astype(BF16), tab_r, chunks, dk // 4, nct)
            log_g = jax.nn.log_sigmoid(ret_decay_logit[j].astype(F32))
            o = _retention(p, log_g, nctx)
            w_out = ret_w_out[j].astype(BF16)
            hs = _mixer_out(_ret_out_kernel, hs, mod_l, gains, nct, (o, p, w_out),
                            [pl.BlockSpec((2, 1, TM, 2 * d), lambda b, i: (0, b, i, 0)),
                             tok_spec(2 * d, 2), _full_spec(w_out)])
        else:
            u = _normmod(hs, mod_l, gains, nct)
            weights = _s5_weights(ssm_lambda_re[j], ssm_lambda_im[j], ssm_log_dt[j], ssm_b_re[j],
                                  ssm_b_im[j], ssm_c_re[j], ssm_c_im[j])
            y = _s5_core(u, weights, nctx)
            w_glu = ssm_w_glu[j].astype(BF16)
            dskip = ssm_d[j].astype(F32).reshape(1, d)
            hs = _mixer_out(_s5_out_kernel, hs, mod_l, gains, nct, (y, dskip, w_glu),
                            [tok_spec(d), _full_spec(dskip), _full_spec(w_glu)])
        hs = _ffn(hs, mod_l, gains, ffn_w_gate[l].astype(BF16), ffn_w_up[l].astype(BF16),
                  ffn_w_down[l].astype(BF16), nct)
    return hs[:, nctx:, :]
```

```python
import functools
import math

import jax
import jax.numpy as jnp
from jax import lax
from jax.experimental import pallas as pl
from jax.experimental.pallas import tpu as pltpu

F32 = jnp.float32
BF16 = jnp.bfloat16

GRID_W = 64
DA_HEADS = 8
RET_HEADS = 4
RET_CHUNK = 256
SSM_GROUP = 16
SSM_STATE = 64
SSM_CHUNK = 16
SSM_GROUP_BLOCK = 8
ROPE_THETA = 10000.0
NORM_EPS = 1e-6
LANES = 128
TM = 256
ATTN_TQ = 512
VT_PAD = 16
VMEM_LIMIT = 56 * 1024 * 1024


def _cparams(*sem):
    return pltpu.CompilerParams(dimension_semantics=sem, vmem_limit_bytes=VMEM_LIMIT)


def _rms(x, gain):
    return x * lax.rsqrt(jnp.mean(x * x, axis=-1, keepdims=True) + NORM_EPS) * gain


def _silu(x):
    return x * jax.nn.sigmoid(x)


def _dot(a, b):
    return jnp.dot(a, b, preferred_element_type=F32)


def _dot_nt(a, b):
    return lax.dot_general(a, b, (((1,), (1,)), ((), ())), preferred_element_type=F32)


def _dot_tn(a, b):
    return lax.dot_general(a, b, (((0,), (0,)), ((), ())), preferred_element_type=F32)


def _mod_kernel(s_ref, w_ref, b_ref, o_ref):
    s = _silu(s_ref[...])
    o_ref[0] = jnp.dot(s, w_ref[0], preferred_element_type=F32,
                       precision=lax.Precision.HIGHEST) + b_ref[0]


def _modulation(c, c_ctx, mod_w, mod_b):
    depth, d, n = mod_w.shape
    bsz = c.shape[0]
    rows = 8
    s = jnp.zeros((rows, d), F32).at[:bsz].set(c).at[bsz].set(c_ctx)
    tn = 1536
    out = pl.pallas_call(
        _mod_kernel,
        grid=(depth, n // tn),
        in_specs=[
            pl.BlockSpec((rows, d), lambda l, j: (0, 0)),
            pl.BlockSpec((1, d, tn), lambda l, j: (l, 0, j)),
            pl.BlockSpec((1, 1, tn), lambda l, j: (l, 0, j)),
        ],
        out_specs=pl.BlockSpec((1, rows, tn), lambda l, j: (l, 0, j)),
        out_shape=jax.ShapeDtypeStruct((depth, rows, n), F32),
        compiler_params=_cparams("parallel", "parallel"),
    )(s, mod_w, mod_b.reshape(depth, 1, n))
    return out.reshape(depth, rows, 6, d)


def _mod_spec(d, bsz, nctx_tiles):
    return pl.BlockSpec((1, 6, d), lambda b, i: (jnp.where(i < nctx_tiles, bsz, b), 0, 0))


def _rope_angles(seq, head_dim):
    quarter = head_dim // 4
    freqs = ROPE_THETA ** (-jnp.arange(quarter, dtype=F32) / quarter)
    rows = seq // GRID_W
    row = jnp.repeat(jnp.arange(rows, dtype=F32), GRID_W)
    col = jnp.tile(jnp.arange(GRID_W, dtype=F32), rows)
    ang_r = row[:, None] * freqs
    ang_c = col[:, None] * freqs
    return jnp.concatenate([ang_r, ang_r, ang_c, ang_c], axis=-1)


def _rope_table(seq, ctx, head_dim):
    ang = _rope_angles(seq, head_dim)
    ang = jnp.concatenate([jnp.zeros((ctx, head_dim), F32), ang], axis=0)
    cos, sin = jnp.cos(ang), jnp.sin(ang)
    q = head_dim // 4
    first = (jnp.arange(head_dim) % (2 * q)) < q
    s_minus = jnp.where(first, -sin, 0.0)
    s_plus = jnp.where(first, 0.0, sin)
    rep = max(1, LANES // head_dim)
    return jnp.concatenate([jnp.tile(t, (1, rep)) for t in (cos, s_minus, s_plus)], axis=1)


def _apply_rope(y, tab, quarter):
    w = tab.shape[1] // 3
    cos, s_minus, s_plus = tab[:, :w], tab[:, w:2 * w], tab[:, 2 * w:]
    outs = []
    for k in range(y.shape[1] // LANES):
        yk = y[:, k * LANES:(k + 1) * LANES]
        t0 = (k * LANES) % w
        ck, mk, pk = (t[:, t0:t0 + LANES] for t in (cos, s_minus, s_plus))
        if 2 * quarter == LANES:
            outs.append(yk * ck + pltpu.roll(yk, quarter, 1) * (mk + pk))
        else:
            outs.append(yk * ck + pltpu.roll(yk, LANES - quarter, 1) * mk
                        + pltpu.roll(yk, quarter, 1) * pk)
    return jnp.concatenate(outs, axis=1)


def _proj_in_kernel(x_ref, mod_ref, g_ref, w_ref, tab_ref, o_ref, *, chunks, quarter):
    mod = mod_ref[0]
    u = (_rms(x_ref[0], g_ref[0:1, :]) * (1.0 + mod[1:2, :]) + mod[0:1, :]).astype(BF16)
    tab = tab_ref[...]
    for c0, width, mult, rope in chunks:
        y = _dot(u, w_ref[:, c0:c0 + width])
        if mult != 1.0:
            y = y * mult
        if rope:
            y = _apply_rope(y, tab, quarter)
        o_ref[0, :, c0:c0 + width] = y.astype(BF16)


def _proj_in(hs, mod_l, gains, w, tab, chunks, quarter, nctx_tiles):
    bsz, L, d = hs.shape
    n = w.shape[1]
    return pl.pallas_call(
        functools.partial(_proj_in_kernel, chunks=chunks, quarter=quarter),
        grid=(bsz, L // TM),
        in_specs=[
            pl.BlockSpec((1, TM, d), lambda b, i: (b, i, 0)),
            _mod_spec(d, bsz, nctx_tiles),
            pl.BlockSpec(gains.shape, lambda b, i: (0, 0)),
            pl.BlockSpec(w.shape, lambda b, i: (0, 0)),
            pl.BlockSpec((TM, tab.shape[1]), lambda b, i: (i, 0)),
        ],
        out_specs=pl.BlockSpec((1, TM, n), lambda b, i: (b, i, 0)),
        out_shape=jax.ShapeDtypeStruct((bsz, L, n), BF16),
        compiler_params=_cparams("parallel", "parallel"),
    )(hs, mod_l, gains, w, tab)


def _attn_proj_kernel(x_ref, mod_ref, g_ref, w_ref, tab_ref, qt_ref, k_ref, vt_ref, *, scale, quarter):
    mod = mod_ref[0]
    u = (_rms(x_ref[0], g_ref[0:1, :]) * (1.0 + mod[1:2, :]) + mod[0:1, :]).astype(BF16)
    d = x_ref.shape[2]
    tab = tab_ref[...]
    heads = vt_ref.shape[1]
    qt_ref[0] = _apply_rope(_dot(u, w_ref[:, :d]) * scale, tab, quarter).T.astype(BF16)
    k_ref[0] = _apply_rope(_dot(u, w_ref[:, d:2 * d]), tab, quarter).astype(BF16)
    vt = _dot(u, w_ref[:, 2 * d:]).T.astype(BF16)
    vt_ref[0, :, 0, 0:LANES, :] = vt.reshape(heads, LANES, vt.shape[1])
    vt_ref[0, :, 0, LANES:, :] = jnp.ones((heads, VT_PAD, vt.shape[1]), BF16)


def _attn_proj(hs, mod_l, gains, w, tab, scale, quarter, nctx_tiles):
    bsz, L, d = hs.shape
    nlat = L // TM - nctx_tiles
    return pl.pallas_call(
        functools.partial(_attn_proj_kernel, scale=scale, quarter=quarter),
        grid=(bsz, L // TM),
        in_specs=[
            pl.BlockSpec((1, TM, d), lambda b, i: (b, i, 0)),
            _mod_spec(d, bsz, nctx_tiles),
            pl.BlockSpec(gains.shape, lambda b, i: (0, 0)),
            pl.BlockSpec(w.shape, lambda b, i: (0, 0)),
            pl.BlockSpec((TM, tab.shape[1]), lambda b, i: (i, 0)),
        ],
        out_specs=[
            pl.BlockSpec((1, d, TM), lambda b, i: (b, 0, jnp.where(i < nctx_tiles, i + nlat, i - nctx_tiles))),
            pl.BlockSpec((1, TM, d), lambda b, i: (b, i, 0)),
            pl.BlockSpec((1, DA_HEADS, 1, LANES + VT_PAD, TM), lambda b, i: (b, 0, i, 0, 0)),
        ],
        out_shape=[
            jax.ShapeDtypeStruct((bsz, d, L), BF16),
            jax.ShapeDtypeStruct((bsz, L, d), BF16),
            jax.ShapeDtypeStruct((bsz, DA_HEADS, L // TM, LANES + VT_PAD, TM), BF16),
        ],
        compiler_params=_cparams("parallel", "parallel"),
    )(hs, mod_l, gains, w, tab)


def _attn_kernel(lam_ref, sub_ref, qt_ref, k_ref, vt_ref, o_ref,
                 s_c, s_a, s_b, mb_scr, m_scr, acc_scr, *, nb, lam_init):
    ctx = s_c.shape[1]
    tk = s_a.shape[1]
    kv = vt_ref.shape[4]
    qt = qt_ref[0]
    row = lax.broadcasted_iota(jnp.int32, qt.shape, 0)
    zero = jnp.zeros_like(qt)
    qts = (jnp.where(row < LANES // 2, qt, zero), jnp.where(row >= LANES // 2, qt, zero))

    slot = {id(s_c): 0, id(s_a): 1, id(s_b): 2}

    def scores(kb, s_out):
        for part in range(2):
            s = _dot(kb, qts[part])
            s_out[part] = s
            mb_scr[slot[id(s_out)], part] = jnp.max(s, axis=0, keepdims=True)

    def stage(s_in, vt0, nxt):
        if nxt is not None:
            scores(*nxt)
        for part in range(2):
            s = s_in[part]
            m = m_scr[part]
            m_new = jnp.maximum(m, mb_scr[slot[id(s_in)], part])
            alpha = jnp.exp2(m - m_new)
            p = jnp.exp2(s - m_new).astype(BF16)
            pv = _dot(vt_ref[0, 0, vt0], p[0:kv])
            for n in range(1, s.shape[0] // kv):
                pv = pv + _dot(vt_ref[0, 0, vt0 + n], p[n * kv:(n + 1) * kv])
            m_scr[part] = m_new
            acc_scr[part] = alpha * acc_scr[part] + pv

    m_scr[...] = jnp.full(m_scr.shape, -1e30, F32)
    acc_scr[...] = jnp.zeros(acc_scr.shape, F32)

    bufs = (s_b, s_a)

    def keys(t):
        return k_ref[0, pl.ds(pl.multiple_of(ctx + (t - 1) * tk, LANES), tk), :]

    def vt_index(t):
        return ctx // kv + (t - 1) * (tk // kv)

    scores(k_ref[0, 0:ctx, :], s_c)
    stage(s_c, 0, (keys(1), s_a) if nb >= 1 else None)
    n_pairs = max(0, (nb - 1) // 2 if nb % 2 else (nb - 2) // 2)

    def pair(u, carry):
        t = 2 * u + 1
        stage(s_a, vt_index(t), (keys(t + 1), s_b))
        stage(s_b, vt_index(t + 1), (keys(t + 2), s_a))
        return carry

    if n_pairs > 0:
        lax.fori_loop(0, n_pairs, pair, 0)
    for t in range(2 * n_pairs + 1, nb + 1):
        stage(bufs[t % 2], vt_index(t), (keys(t + 1), bufs[(t + 1) % 2]) if t < nb else None)

    lv = lam_ref[...]
    lam = (jnp.exp(jnp.sum(lv[0:1] * lv[1:2], axis=-1, keepdims=True))
           - jnp.exp(jnp.sum(lv[2:3] * lv[3:4], axis=-1, keepdims=True)) + lam_init)
    a1, a2 = acc_scr[0], acc_scr[1]
    o = (a1[:LANES] / a1[LANES:LANES + 1] - lam * (a2[:LANES] / a2[LANES:LANES + 1])).T
    o = _rms(o, sub_ref[...]) * (1.0 - lam_init)
    o_ref[0] = o.astype(BF16)


def _attention_call(q_t, k, v_t, lam_vec, subln, ctx, tk, nb, lam_init, tq, q_off, n_out):
    bsz, d, _ = q_t.shape
    kv = v_t.shape[4]
    k_rows = ctx + nb * tk
    return pl.pallas_call(
        functools.partial(_attn_kernel, nb=nb, lam_init=lam_init),
        grid=(bsz, DA_HEADS, n_out // tq),
        in_specs=[
            pl.BlockSpec(lam_vec.shape, lambda b, h, i: (0, 0)),
            pl.BlockSpec((1, LANES), lambda b, h, i: (0, 0)),
            pl.BlockSpec((1, LANES, tq), lambda b, h, i: (b, h, i + q_off // tq)),
            pl.BlockSpec((1, k_rows, LANES), lambda b, h, i: (b, 0, h)),
            pl.BlockSpec((1, 1, k_rows // kv) + v_t.shape[3:], lambda b, h, i: (b, h, 0, 0, 0)),
        ],
        out_specs=pl.BlockSpec((1, tq, LANES), lambda b, h, i: (b, i, h)),
        out_shape=jax.ShapeDtypeStruct((bsz, n_out, d), BF16),
        scratch_shapes=[pltpu.VMEM((2, ctx, tq), F32), pltpu.VMEM((2, tk, tq), F32),
                        pltpu.VMEM((2, tk, tq), F32), pltpu.VMEM((3, 2, 1, tq), F32),
                        pltpu.VMEM((2, 1, tq), F32), pltpu.VMEM((2, LANES + VT_PAD, tq), F32)],
        compiler_params=_cparams("parallel", "parallel", "arbitrary"),
    )(lam_vec, subln.reshape(1, LANES), q_t, k, v_t)


def _attention(q_t, k, v_t, lam_vec, subln, ctx, lam_init):
    L = k.shape[1]
    seq = L - ctx
    tk = min(1024, seq)
    nb = seq // tk
    tq = min(ATTN_TQ, seq)
    o_ctx = _attention_call(q_t, k, v_t, lam_vec, subln, ctx, tk, 0, lam_init, TM, seq, ctx)
    o_lat = _attention_call(q_t, k, v_t, lam_vec, subln, ctx, tk, nb, lam_init, tq, 0, seq)
    return jnp.concatenate([o_ctx, o_lat], axis=1)


def _ret_kernel(lg_ref, q_ref, k_ref, v_ref, o_ref, r_scr):
    d = pl.program_id(2)
    c = pl.program_id(3)
    C = q_ref.shape[1]

    @pl.when(c == 0)
    def _():
        r_scr[...] = jnp.zeros_like(r_scr)

    lg = lg_ref[0][:, 0:1]
    fwd = d == 0
    ii = lax.broadcasted_iota(jnp.int32, (C, C), 0)
    jj = lax.broadcasted_iota(jnp.int32, (C, C), 1)
    dd = jnp.where(fwd, ii - jj, jj - ii)
    decay = jnp.where(dd >= 0, jnp.exp(lg * jnp.maximum(dd, 0).astype(F32)), 0.0)
    idx = lax.broadcasted_iota(jnp.int32, (C, 1), 0)
    q_dec = jnp.exp(lg * jnp.where(fwd, idx + 1, C - idx).astype(F32))
    k_dec = jnp.exp(lg * jnp.where(fwd, C - 1 - idx, idx).astype(F32))
    c_dec = jnp.exp(lg * float(C))

    q = q_ref[0]
    k = k_ref[0]
    v = v_ref[0]
    r = r_scr[...]
    s = _dot_nt(q, k) * decay
    inner = _dot(s.astype(BF16), v)
    cross = _dot((q.astype(F32) * q_dec).astype(BF16), r.astype(BF16))
    o_ref[0, 0] = (inner + cross).astype(BF16)
    r_scr[...] = c_dec * r + _dot_tn((k.astype(F32) * k_dec).astype(BF16), v)


def _retention(p, log_g, ctx):
    bsz, L, n = p.shape
    d = n // 6
    dk = d // RET_HEADS
    dv = 2 * d // RET_HEADS
    C = RET_CHUNK
    nc, ncc = L // C, ctx // C

    def chunk(dr, c):
        back = jnp.where(c < ncc, ncc - 1 - c, nc + ncc - 1 - c)
        return jnp.where(dr == 0, c, back)

    lg = jnp.broadcast_to(log_g.reshape(2 * RET_HEADS, 1, 1), (2 * RET_HEADS, 1, LANES))
    return pl.pallas_call(
        _ret_kernel,
        grid=(bsz, RET_HEADS, 2, nc),
        in_specs=[
            pl.BlockSpec((1, 1, LANES), lambda b, h, dr, c: (dr * RET_HEADS + h, 0, 0)),
            pl.BlockSpec((1, C, dk), lambda b, h, dr, c: (b, chunk(dr, c), h)),
            pl.BlockSpec((1, C, dk), lambda b, h, dr, c: (b, chunk(dr, c), RET_HEADS + h)),
            pl.BlockSpec((1, C, dv), lambda b, h, dr, c: (b, chunk(dr, c), RET_HEADS + h)),
        ],
        out_specs=pl.BlockSpec((1, 1, C, dv), lambda b, h, dr, c: (dr, b, chunk(dr, c), h)),
        out_shape=jax.ShapeDtypeStruct((2, bsz, L, 2 * d), BF16),
        scratch_shapes=[pltpu.VMEM((dk, dv), F32)],
        compiler_params=_cparams("parallel", "parallel", "arbitrary", "arbitrary"),
    )(lg, p, p, p)


def _normmod_kernel(x_ref, mod_ref, g_ref, o_ref):
    mod = mod_ref[0]
    o_ref[0] = (_rms(x_ref[0], g_ref[0:1, :]) * (1.0 + mod[1:2, :]) + mod[0:1, :]).astype(BF16)


def _normmod(hs, mod_l, gains, nctx_tiles):
    bsz, L, d = hs.shape
    return pl.pallas_call(
        _normmod_kernel,
        grid=(bsz, L // TM),
        in_specs=[
            pl.BlockSpec((1, TM, d), lambda b, i: (b, i, 0)),
            _mod_spec(d, bsz, nctx_tiles),
            pl.BlockSpec(gains.shape, lambda b, i: (0, 0)),
        ],
        out_specs=pl.BlockSpec((1, TM, d), lambda b, i: (b, i, 0)),
        out_shape=jax.ShapeDtypeStruct((bsz, L, d), BF16),
        compiler_params=_cparams("parallel", "parallel"),
    )(hs, mod_l, gains)


def _s5_kernel(x_ref, t_ref, wb_ref, wc_ref, a_ref, y_ref, st_scr, *, nctx):
    gb = x_ref.shape[1]
    nch = x_ref.shape[2]
    for gi in range(gb):
        x = x_ref[0, gi]
        for dr in range(2):
            s = _dot(x, wb_ref[gi, dr])
            st_scr[2 * dr, pl.ds(gi, nch, stride=gb), :] = s[:, :LANES]
            st_scr[2 * dr + 1, pl.ds(gi, nch, stride=gb), :] = s[:, LANES:]

    a = a_ref[...]

    def step(dr, row, h_re, h_im):
        s_re = st_scr[2 * dr, pl.ds(row, gb), :]
        s_im = st_scr[2 * dr + 1, pl.ds(row, gb), :]
        st_scr[2 * dr, pl.ds(row, gb), :] = h_re
        st_scr[2 * dr + 1, pl.ds(row, gb), :] = h_im
        a_re, a_im = a[dr, 0], a[dr, 1]
        return a_re * h_re - a_im * h_im + s_re, a_re * h_im + a_im * h_re + s_im

    def body(t, carry):
        f_re, f_im, b_re, b_im = carry
        cb = jnp.where(t < nctx, nctx - 1 - t, nch + nctx - 1 - t)
        f_re, f_im = step(0, pl.multiple_of(t * gb, gb), f_re, f_im)
        b_re, b_im = step(1, pl.multiple_of(cb * gb, gb), b_re, b_im)
        return f_re, f_im, b_re, b_im

    z = jnp.zeros((gb, LANES), F32)
    lax.fori_loop(0, nch, body, (z, z, z, z))

    for gi in range(gb):
        y = _dot(x_ref[0, gi], t_ref[gi])
        for dr in range(2):
            h = jnp.concatenate([st_scr[2 * dr, pl.ds(gi, nch, stride=gb), :],
                                 st_scr[2 * dr + 1, pl.ds(gi, nch, stride=gb), :]], axis=1)
            y = y + _dot(h.astype(BF16), wc_ref[gi, dr])
        y_ref[0, gi] = y


def _s5_weights(lam_re, lam_im, log_dt, b_re, b_im, c_re, c_im):
    n = SSM_CHUNK
    lam = lax.complex(lam_re.astype(F32), lam_im.astype(F32))
    ldt = lam * jnp.exp(log_dt.astype(F32))[..., None]
    lam_bar = jnp.exp(ldt)
    b_bar = ((lam_bar - 1.0) / lam)[..., None] * lax.complex(b_re.astype(F32), b_im.astype(F32))
    c_mat = lax.complex(c_re.astype(F32), c_im.astype(F32))
    steps = jnp.arange(n + 1, dtype=F32)
    pw = jnp.exp(ldt[None] * steps[:, None, None, None])
    G, P, Hg = b_bar.shape[1:]

    kern = jnp.real(jnp.einsum('kdgp,dgjp,dgpi->dkgij', pw[:n], c_mat, b_bar))
    s_idx = jnp.arange(n)[:, None]
    t_idx = jnp.arange(n)[None, :]
    lag = t_idx - s_idx
    kf = jnp.where((lag >= 0)[..., None, None, None], kern[0][jnp.clip(lag, 0, n - 1)], 0.0)
    kb = jnp.where((lag <= 0)[..., None, None, None], kern[1][jnp.clip(-lag, 0, n - 1)], 0.0)
    t_mat = (kf + kb).transpose(2, 0, 3, 1, 4).reshape(G, n * Hg, n * Hg)

    zpad = jnp.zeros((G, n * Hg, LANES - P), F32)
    zrow = jnp.zeros((G, LANES - P, n * Hg), F32)
    wbs, wcs = [], []
    for dr in range(2):
        e_in = pw[:n][::-1, dr] if dr == 0 else pw[:n, dr]
        wb = (e_in[:, :, :, None] * b_bar[dr][None]).transpose(1, 0, 3, 2).reshape(G, n * Hg, P)
        wbs.append(jnp.concatenate([jnp.real(wb), zpad, jnp.imag(wb), zpad], axis=2))
        e_out = pw[1:, dr] if dr == 0 else pw[1:][::-1, dr]
        wc = (c_mat[dr][None] * e_out[:, :, None, :]).transpose(1, 3, 0, 2).reshape(G, P, n * Hg)
        wcs.append(jnp.concatenate([jnp.real(wc), zrow, -jnp.imag(wc), zrow], axis=1))
    w_b = jnp.stack(wbs, axis=1)
    w_c = jnp.stack(wcs, axis=1)
    a_n = pw[n]
    pad = jnp.zeros((2, G, LANES - P), F32)
    a = jnp.stack([jnp.concatenate([jnp.real(a_n), pad], -1),
                   jnp.concatenate([jnp.imag(a_n), pad], -1)], axis=1)
    return t_mat.astype(BF16), w_b.astype(BF16), w_c.astype(BF16), a


def _s5_core(u, weights, ctx):
    bsz, L, d = u.shape
    n, hg, gb = SSM_CHUNK, SSM_GROUP, SSM_GROUP_BLOCK
    G = d // hg
    nch = L // n
    t_mat, w_b, w_c, a = weights
    x = u.reshape(bsz, nch, n, G, hg).transpose(0, 3, 1, 2, 4).reshape(bsz, G, nch, n * hg)
    y = pl.pallas_call(
        functools.partial(_s5_kernel, nctx=ctx // n),
        grid=(bsz, G // gb),
        in_specs=[
            pl.BlockSpec((1, gb, nch, n * hg), lambda b, g: (b, g, 0, 0)),
            pl.BlockSpec((gb, n * hg, n * hg), lambda b, g: (g, 0, 0)),
            pl.BlockSpec((gb, 2, n * hg, 2 * LANES), lambda b, g: (g, 0, 0, 0)),
            pl.BlockSpec((gb, 2, 2 * LANES, n * hg), lambda b, g: (g, 0, 0, 0)),
            pl.BlockSpec((2, 2, gb, LANES), lambda b, g: (0, 0, g, 0)),
        ],
        out_specs=pl.BlockSpec((1, gb, nch, n * hg), lambda b, g: (b, g, 0, 0)),
        out_shape=jax.ShapeDtypeStruct((bsz, G, nch, n * hg), F32),
        scratch_shapes=[pltpu.VMEM((4, nch * gb, LANES), F32)],
        compiler_params=_cparams("parallel", "parallel"),
    )(x, t_mat, w_b, w_c, a)
    return y.reshape(bsz, G, nch, n, hg).transpose(0, 2, 3, 1, 4).reshape(bsz, L, d)


def _finish(h, y, mod, g_ref, o_ref):
    o_ref[0] = h + mod[2:3, :] * _rms(y, g_ref[1:2, :])


def _attn_out_kernel(h_ref, mod_ref, g_ref, o_in_ref, w_ref, o_ref):
    _finish(h_ref[0], _dot(o_in_ref[0], w_ref[...]), mod_ref[0], g_ref, o_ref)


def _ret_out_kernel(h_ref, mod_ref, g_ref, o_in_ref, gate_ref, w_ref, o_ref):
    o = o_in_ref[0, 0].astype(F32) + o_in_ref[1, 0].astype(F32)
    dv = o.shape[1] // RET_HEADS
    parts = []
    for hd in range(RET_HEADS):
        oh = o[:, hd * dv:(hd + 1) * dv]
        parts.append(oh * lax.rsqrt(jnp.mean(oh * oh, axis=-1, keepdims=True) + NORM_EPS))
    o = jnp.concatenate(parts, axis=1)
    y = _dot((_silu(gate_ref[0].astype(F32)) * o).astype(BF16), w_ref[...])
    _finish(h_ref[0], y, mod_ref[0], g_ref, o_ref)


def _s5_out_kernel(h_ref, mod_ref, g_ref, y_ref, dskip_ref, w_ref, o_ref):
    h = h_ref[0]
    mod = mod_ref[0]
    d = h.shape[1]
    u = _rms(h, g_ref[0:1, :]) * (1.0 + mod[1:2, :]) + mod[0:1, :]
    z = y_ref[0] + dskip_ref[...] * u
    gl = 0.5 * z * (1.0 + jnp.tanh(math.sqrt(2.0 / math.pi) * (z + 0.044715 * (z * z * z))))
    gl = gl.astype(BF16)
    y = _dot(gl, w_ref[:, :d]) * jax.nn.sigmoid(_dot(gl, w_ref[:, d:]))
    _finish(h, y, mod, g_ref, o_ref)


def _mixer_out(kernel, hs, mod_l, gains, nctx_tiles, extra, extra_specs):
    bsz, L, d = hs.shape
    return pl.pallas_call(
        kernel,
        grid=(bsz, L // TM),
        in_specs=[
            pl.BlockSpec((1, TM, d), lambda b, i: (b, i, 0)),
            _mod_spec(d, bsz, nctx_tiles),
            pl.BlockSpec(gains.shape, lambda b, i: (0, 0)),
        ] + extra_specs,
        out_specs=pl.BlockSpec((1, TM, d), lambda b, i: (b, i, 0)),
        out_shape=jax.ShapeDtypeStruct((bsz, L, d), F32),
        compiler_params=_cparams("parallel", "parallel"),
    )(hs, mod_l, gains, *extra)


def _full_spec(a):
    nd = a.ndim
    return pl.BlockSpec(a.shape, lambda b, i: (0,) * nd)


def _ffn_kernel(h_ref, mod_ref, g_ref, wg_ref, wu_ref, wd_ref, o_ref):
    h = h_ref[0]
    mod = mod_ref[0]
    v = (_rms(h, g_ref[2:3, :]) * (1.0 + mod[4:5, :]) + mod[3:4, :]).astype(BF16)
    act = (_silu(_dot(v, wg_ref[...])) * _dot(v, wu_ref[...])).astype(BF16)
    y = _dot(act, wd_ref[...])
    o_ref[0] = h + mod[5:6, :] * _rms(y, g_ref[3:4, :])


def _ffn(hs, mod_l, gains, wg, wu, wd, nctx_tiles):
    bsz, L, d = hs.shape
    return pl.pallas_call(
        _ffn_kernel,
        grid=(bsz, L // TM),
        in_specs=[
            pl.BlockSpec((1, TM, d), lambda b, i: (b, i, 0)),
            _mod_spec(d, bsz, nctx_tiles),
            pl.BlockSpec(gains.shape, lambda b, i: (0, 0)),
            _full_spec(wg), _full_spec(wu), _full_spec(wd),
        ],
        out_specs=pl.BlockSpec((1, TM, d), lambda b, i: (b, i, 0)),
        out_shape=jax.ShapeDtypeStruct((bsz, L, d), F32),
        compiler_params=_cparams("parallel", "parallel"),
    )(hs, mod_l, gains, wg, wu, wd)


def kernel(x, c, ctx, c_ctx, mod_w, mod_b, norm_g, attn_w_in, attn_w_out, attn_lambda, attn_subln,
           ret_w_in, ret_w_out, ret_decay_logit, ssm_lambda_re, ssm_lambda_im, ssm_log_dt,
           ssm_b_re, ssm_b_im, ssm_c_re, ssm_c_im, ssm_d, ssm_w_glu, ffn_w_gate, ffn_w_up, ffn_w_down):
    bsz, seq, d = x.shape
    nctx = ctx.shape[1]
    depth = mod_w.shape[0]
    assert nctx % TM == 0 and seq % TM == 0 and seq % GRID_W == 0
    nct = nctx // TM
    L = nctx + seq

    mods = _modulation(c, c_ctx, mod_w, mod_b)
    hs = jnp.concatenate([ctx, x], axis=1)
    da_dim = d // (2 * DA_HEADS)
    dk = d // RET_HEADS
    tab_a = _rope_table(seq, nctx, da_dim)
    tab_r = _rope_table(seq, nctx, dk)

    def tok_spec(width, col=0):
        return pl.BlockSpec((1, TM, width), lambda b, i: (b, i, col))

    for l in range(depth):
        kind, j = l % 3, l // 3
        mod_l, gains = mods[l], norm_g[l]
        if kind == 0:
            lam_init = 0.8 - 0.6 * math.exp(-0.3 * l)
            q_t, k, v_t = _attn_proj(hs, mod_l, gains, attn_w_in[j].astype(BF16), tab_a,
                                     da_dim ** -0.5 * math.log2(math.e), da_dim // 4, nct)
            o = _attention(q_t, k, v_t, attn_lambda[j], attn_subln[j], nctx, lam_init)
            w_out = attn_w_out[j].astype(BF16)
            hs = _mixer_out(_attn_out_kernel, hs, mod_l, gains, nct, (o, w_out),
                            [tok_spec(d), _full_spec(w_out)])
        elif kind == 1:
            chunks = ((0, d, 1.0, True), (d, d, dk ** -0.5, True)) + tuple(
                (c0, d, 1.0, False) for c0 in range(2 * d, 6 * d, d))
            p = _proj_in(hs, mod_l, gains, ret_w_in[j].astype(BF16), tab_r, chunks, dk // 4, nct)
            log_g = jax.nn.log_sigmoid(ret_decay_logit[j].astype(F32))
            o = _retention(p, log_g, nctx)
            w_out = ret_w_out[j].astype(BF16)
            hs = _mixer_out(_ret_out_kernel, hs, mod_l, gains, nct, (o, p, w_out),
                            [pl.BlockSpec((2, 1, TM, 2 * d), lambda b, i: (0, b, i, 0)),
                             tok_spec(2 * d, 2), _full_spec(w_out)])
        else:
            u = _normmod(hs, mod_l, gains, nct)
            weights = _s5_weights(ssm_lambda_re[j], ssm_lambda_im[j], ssm_log_dt[j], ssm_b_re[j],
                                  ssm_b_im[j], ssm_c_re[j], ssm_c_im[j])
            y = _s5_core(u, weights, nctx)
            w_glu = ssm_w_glu[j].astype(BF16)
            dskip = ssm_d[j].astype(F32).reshape(1, d)
            hs = _mixer_out(_s5_out_kernel, hs, mod_l, gains, nct, (y, dskip, w_glu),
                            [tok_spec(d), _full_spec(dskip), _full_spec(w_glu)])
        hs = _ffn(hs, mod_l, gains, ffn_w_gate[l].astype(BF16), ffn_w_up[l].astype(BF16),
                  ffn_w_down[l].astype(BF16), nct)
    return hs[:, nctx:, :]
```

```python
import functools
import math

import jax
import jax.numpy as jnp
from jax import lax
from jax.experimental import pallas as pl
from jax.experimental.pallas import tpu as pltpu

F32 = jnp.float32
BF16 = jnp.bfloat16

GRID_W = 64
DA_HEADS = 8
RET_HEADS = 4
RET_CHUNK = 256
SSM_GROUP = 16
SSM_STATE = 64
SSM_CHUNK = 16
SSM_GROUP_BLOCK = 8
ROPE_THETA = 10000.0
NORM_EPS = 1e-6
LANES = 128
TM = 256
ATTN_TQ = 512
VT_PAD = 16
VMEM_LIMIT = 56 * 1024 * 1024


def _cparams(*sem):
    return pltpu.CompilerParams(dimension_semantics=sem, vmem_limit_bytes=VMEM_LIMIT)


def _rms(x, gain):
    return x * lax.rsqrt(jnp.mean(x * x, axis=-1, keepdims=True) + NORM_EPS) * gain


def _silu(x):
    return x * jax.nn.sigmoid(x)


def _dot(a, b):
    return jnp.dot(a, b, preferred_element_type=F32)


def _dot_nt(a, b):
    return lax.dot_general(a, b, (((1,), (1,)), ((), ())), preferred_element_type=F32)


def _dot_tn(a, b):
    return lax.dot_general(a, b, (((0,), (0,)), ((), ())), preferred_element_type=F32)


def _mod_kernel(s_ref, w_ref, b_ref, o_ref):
    s = _silu(s_ref[...])
    o_ref[0] = jnp.dot(s, w_ref[0], preferred_element_type=F32,
                       precision=lax.Precision.HIGHEST) + b_ref[0]


def _modulation(c, c_ctx, mod_w, mod_b):
    depth, d, n = mod_w.shape
    bsz = c.shape[0]
    rows = 8
    s = jnp.zeros((rows, d), F32).at[:bsz].set(c).at[bsz].set(c_ctx)
    tn = 1536
    out = pl.pallas_call(
        _mod_kernel,
        grid=(depth, n // tn),
        in_specs=[
            pl.BlockSpec((rows, d), lambda l, j: (0, 0)),
            pl.BlockSpec((1, d, tn), lambda l, j: (l, 0, j)),
            pl.BlockSpec((1, 1, tn), lambda l, j: (l, 0, j)),
        ],
        out_specs=pl.BlockSpec((1, rows, tn), lambda l, j: (l, 0, j)),
        out_shape=jax.ShapeDtypeStruct((depth, rows, n), F32),
        compiler_params=_cparams("parallel", "parallel"),
    )(s, mod_w, mod_b.reshape(depth, 1, n))
    return out.reshape(depth, rows, 6, d)


def _mod_spec(d, bsz, nctx_tiles):
    return pl.BlockSpec((1, 6, d), lambda b, i: (jnp.where(i < nctx_tiles, bsz, b), 0, 0))


def _rope_angles(seq, head_dim):
    quarter = head_dim // 4
    freqs = ROPE_THETA ** (-jnp.arange(quarter, dtype=F32) / quarter)
    rows = seq // GRID_W
    row = jnp.repeat(jnp.arange(rows, dtype=F32), GRID_W)
    col = jnp.tile(jnp.arange(GRID_W, dtype=F32), rows)
    ang_r = row[:, None] * freqs
    ang_c = col[:, None] * freqs
    return jnp.concatenate([ang_r, ang_r, ang_c, ang_c], axis=-1)


def _rope_table(seq, ctx, head_dim):
    ang = _rope_angles(seq, head_dim)
    ang = jnp.concatenate([jnp.zeros((ctx, head_dim), F32), ang], axis=0)
    cos, sin = jnp.cos(ang), jnp.sin(ang)
    q = head_dim // 4
    first = (jnp.arange(head_dim) % (2 * q)) < q
    s_minus = jnp.where(first, -sin, 0.0)
    s_plus = jnp.where(first, 0.0, sin)
    rep = max(1, LANES // head_dim)
    return jnp.concatenate([jnp.tile(t, (1, rep)) for t in (cos, s_minus, s_plus)], axis=1)


def _apply_rope(y, tab, quarter):
    w = tab.shape[1] // 3
    cos, s_minus, s_plus = tab[:, :w], tab[:, w:2 * w], tab[:, 2 * w:]
    outs = []
    for k in range(y.shape[1] // LANES):
        yk = y[:, k * LANES:(k + 1) * LANES]
        t0 = (k * LANES) % w
        ck, mk, pk = (t[:, t0:t0 + LANES] for t in (cos, s_minus, s_plus))
        if 2 * quarter == LANES:
            outs.append(yk * ck + pltpu.roll(yk, quarter, 1) * (mk + pk))
        else:
            outs.append(yk * ck + pltpu.roll(yk, LANES - quarter, 1) * mk
                        + pltpu.roll(yk, quarter, 1) * pk)
    return jnp.concatenate(outs, axis=1)


def _proj_in_kernel(x_ref, mod_ref, g_ref, w_ref, tab_ref, o_ref, *, chunks, quarter):
    mod = mod_ref[0]
    u = (_rms(x_ref[0], g_ref[0:1, :]) * (1.0 + mod[1:2, :]) + mod[0:1, :]).astype(BF16)
    tab = tab_ref[...]
    for c0, width, mult, rope in chunks:
        y = _dot(u, w_ref[:, c0:c0 + width])
        if mult != 1.0:
            y = y * mult
        if rope:
            y = _apply_rope(y, tab, quarter)
        o_ref[0, :, c0:c0 + width] = y.astype(BF16)


def _proj_in(hs, mod_l, gains, w, tab, chunks, quarter, nctx_tiles):
    bsz, L, d = hs.shape
    n = w.shape[1]
    return pl.pallas_call(
        functools.partial(_proj_in_kernel, chunks=chunks, quarter=quarter),
        grid=(bsz, L // TM),
        in_specs=[
            pl.BlockSpec((1, TM, d), lambda b, i: (b, i, 0)),
            _mod_spec(d, bsz, nctx_tiles),
            pl.BlockSpec(gains.shape, lambda b, i: (0, 0)),
            pl.BlockSpec(w.shape, lambda b, i: (0, 0)),
            pl.BlockSpec((TM, tab.shape[1]), lambda b, i: (i, 0)),
        ],
        out_specs=pl.BlockSpec((1, TM, n), lambda b, i: (b, i, 0)),
        out_shape=jax.ShapeDtypeStruct((bsz, L, n), BF16),
        compiler_params=_cparams("parallel", "parallel"),
    )(hs, mod_l, gains, w, tab)


def _attn_proj_kernel(x_ref, mod_ref, g_ref, w_ref, tab_ref, qt_ref, k_ref, vt_ref, *, scale, quarter):
    mod = mod_ref[0]
    u = (_rms(x_ref[0], g_ref[0:1, :]) * (1.0 + mod[1:2, :]) + mod[0:1, :]).astype(BF16)
    d = x_ref.shape[2]
    tab = tab_ref[...]
    heads = vt_ref.shape[1]
    qt_ref[0] = _apply_rope(_dot(u, w_ref[:, :d]) * scale, tab, quarter).T.astype(BF16)
    k_ref[0] = _apply_rope(_dot(u, w_ref[:, d:2 * d]), tab, quarter).astype(BF16)
    vt = _dot(u, w_ref[:, 2 * d:]).T.astype(BF16)
    vt_ref[0, :, 0, 0:LANES, :] = vt.reshape(heads, LANES, vt.shape[1])
    vt_ref[0, :, 0, LANES:, :] = jnp.ones((heads, VT_PAD, vt.shape[1]), BF16)


def _attn_proj(hs, mod_l, gains, w, tab, scale, quarter, nctx_tiles):
    bsz, L, d = hs.shape
    nlat = L // TM - nctx_tiles
    return pl.pallas_call(
        functools.partial(_attn_proj_kernel, scale=scale, quarter=quarter),
        grid=(bsz, L // TM),
        in_specs=[
            pl.BlockSpec((1, TM, d), lambda b, i: (b, i, 0)),
            _mod_spec(d, bsz, nctx_tiles),
            pl.BlockSpec(gains.shape, lambda b, i: (0, 0)),
            pl.BlockSpec(w.shape, lambda b, i: (0, 0)),
            pl.BlockSpec((TM, tab.shape[1]), lambda b, i: (i, 0)),
        ],
        out_specs=[
            pl.BlockSpec((1, d, TM), lambda b, i: (b, 0, jnp.where(i < nctx_tiles, i + nlat, i - nctx_tiles))),
            pl.BlockSpec((1, TM, d), lambda b, i: (b, i, 0)),
            pl.BlockSpec((1, DA_HEADS, 1, LANES + VT_PAD, TM), lambda b, i: (b, 0, i, 0, 0)),
        ],
        out_shape=[
            jax.ShapeDtypeStruct((bsz, d, L), BF16),
            jax.ShapeDtypeStruct((bsz, L, d), BF16),
            jax.ShapeDtypeStruct((bsz, DA_HEADS, L // TM, LANES + VT_PAD, TM), BF16),
        ],
        compiler_params=_cparams("parallel", "parallel"),
    )(hs, mod_l, gains, w, tab)


def _attn_kernel(lam_ref, sub_ref, qt_ref, k_ref, vt_ref, o_ref,
                 s_c, s_a, s_b, mb_scr, m_scr, acc_scr, *, nb, lam_init):
    ctx = s_c.shape[1]
    tk = s_a.shape[1]
    kv = vt_ref.shape[4]
    qt = qt_ref[0]
    row = lax.broadcasted_iota(jnp.int32, qt.shape, 0)
    zero = jnp.zeros_like(qt)
    qts = (jnp.where(row < LANES // 2, qt, zero), jnp.where(row >= LANES // 2, qt, zero))

    slot = {id(s_c): 0, id(s_a): 1, id(s_b): 2}

    def scores(kb, s_out):
        for part in range(2):
            s = _dot(kb, qts[part])
            s_out[part] = s
            mb_scr[slot[id(s_out)], part] = jnp.max(s, axis=0, keepdims=True)

    def stage(s_in, vt0, nxt):
        if nxt is not None:
            scores(*nxt)
        for part in range(2):
            s = s_in[part]
            m = m_scr[part]
            m_new = jnp.maximum(m, mb_scr[slot[id(s_in)], part])
            alpha = jnp.exp2(m - m_new)
            p = jnp.exp2(s - m_new).astype(BF16)
            pv = _dot(vt_ref[0, 0, vt0], p[0:kv])
            for n in range(1, s.shape[0] // kv):
                pv = pv + _dot(vt_ref[0, 0, vt0 + n], p[n * kv:(n + 1) * kv])
            m_scr[part] = m_new
            acc_scr[part] = alpha * acc_scr[part] + pv

    m_scr[...] = jnp.full(m_scr.shape, -1e30, F32)
    acc_scr[...] = jnp.zeros(acc_scr.shape, F32)

    bufs = (s_b, s_a)

    def keys(t):
        return k_ref[0, pl.ds(pl.multiple_of(ctx + (t - 1) * tk, LANES), tk), :]

    def vt_index(t):
        return ctx // kv + (t - 1) * (tk // kv)

    scores(k_ref[0, 0:ctx, :], s_c)
    stage(s_c, 0, (keys(1), s_a) if nb >= 1 else None)
    n_pairs = max(0, (nb - 1) // 2 if nb % 2 else (nb - 2) // 2)

    def pair(u, carry):
        t = 2 * u + 1
        stage(s_a, vt_index(t), (keys(t + 1), s_b))
        stage(s_b, vt_index(t + 1), (keys(t + 2), s_a))
        return carry

    if n_pairs > 0:
        lax.fori_loop(0, n_pairs, pair, 0)
    for t in range(2 * n_pairs + 1, nb + 1):
        stage(bufs[t % 2], vt_index(t), (keys(t + 1), bufs[(t + 1) % 2]) if t < nb else None)

    lv = lam_ref[...]
    lam = (jnp.exp(jnp.sum(lv[0:1] * lv[1:2], axis=-1, keepdims=True))
           - jnp.exp(jnp.sum(lv[2:3] * lv[3:4], axis=-1, keepdims=True)) + lam_init)
    a1, a2 = acc_scr[0], acc_scr[1]
    o = (a1[:LANES] / a1[LANES:LANES + 1] - lam * (a2[:LANES] / a2[LANES:LANES + 1])).T
    o = _rms(o, sub_ref[...]) * (1.0 - lam_init)
    o_ref[0] = o.astype(BF16)


def _attention_call(q_t, k, v_t, lam_vec, subln, ctx, tk, nb, lam_init, tq, q_off, n_out):
    bsz, d, _ = q_t.shape
    kv = v_t.shape[4]
    k_rows = ctx + nb * tk
    return pl.pallas_call(
        functools.partial(_attn_kernel, nb=nb, lam_init=lam_init),
        grid=(bsz, DA_HEADS, n_out // tq),
        in_specs=[
            pl.BlockSpec(lam_vec.shape, lambda b, h, i: (0, 0)),
            pl.BlockSpec((1, LANES), lambda b, h, i: (0, 0)),
            pl.BlockSpec((1, LANES, tq), lambda b, h, i: (b, h, i + q_off // tq)),
            pl.BlockSpec((1, k_rows, LANES), lambda b, h, i: (b, 0, h)),
            pl.BlockSpec((1, 1, k_rows // kv) + v_t.shape[3:], lambda b, h, i: (b, h, 0, 0, 0)),
        ],
        out_specs=pl.BlockSpec((1, tq, LANES), lambda b, h, i: (b, i, h)),
        out_shape=jax.ShapeDtypeStruct((bsz, n_out, d), BF16),
        scratch_shapes=[pltpu.VMEM((2, ctx, tq), F32), pltpu.VMEM((2, tk, tq), F32),
                        pltpu.VMEM((2, tk, tq), F32), pltpu.VMEM((3, 2, 1, tq), F32),
                        pltpu.VMEM((2, 1, tq), F32), pltpu.VMEM((2, LANES + VT_PAD, tq), F32)],
        compiler_params=_cparams("parallel", "parallel", "arbitrary"),
    )(lam_vec, subln.reshape(1, LANES), q_t, k, v_t)


def _attention(q_t, k, v_t, lam_vec, subln, ctx, lam_init):
    L = k.shape[1]
    seq = L - ctx
    tk = min(1024, seq)
    nb = seq // tk
    tq = min(ATTN_TQ, seq)
    o_ctx = _attention_call(q_t, k, v_t, lam_vec, subln, ctx, tk, 0, lam_init, TM, seq, ctx)
    o_lat = _attention_call(q_t, k, v_t, lam_vec, subln, ctx, tk, nb, lam_init, tq, 0, seq)
    return jnp.concatenate([o_ctx, o_lat], axis=1)


def _ret_kernel(lg_ref, q_ref, k_ref, v_ref, o_ref, r_scr):
    d = pl.program_id(2)
    c = pl.program_id(3)
    C = q_ref.shape[1]

    @pl.when(c == 0)
    def _():
        r_scr[...] = jnp.zeros_like(r_scr)

    lg = lg_ref[0][:, 0:1]
    fwd = d == 0
    ii = lax.broadcasted_iota(jnp.int32, (C, C), 0)
    jj = lax.broadcasted_iota(jnp.int32, (C, C), 1)
    dd = jnp.where(fwd, ii - jj, jj - ii)
    decay = jnp.where(dd >= 0, jnp.exp(lg * jnp.maximum(dd, 0).astype(F32)), 0.0)
    idx = lax.broadcasted_iota(jnp.int32, (C, 1), 0)
    q_dec = jnp.exp(lg * jnp.where(fwd, idx + 1, C - idx).astype(F32))
    k_dec = jnp.exp(lg * jnp.where(fwd, C - 1 - idx, idx).astype(F32))
    c_dec = jnp.exp(lg * float(C))

    q = q_ref[0]
    k = k_ref[0]
    v = v_ref[0]
    r = r_scr[...]
    s = _dot_nt(q, k) * decay
    inner = _dot(s.astype(BF16), v)
    cross = _dot((q.astype(F32) * q_dec).astype(BF16), r.astype(BF16))
    o_ref[0, 0] = (inner + cross).astype(BF16)
    r_scr[...] = c_dec * r + _dot_tn((k.astype(F32) * k_dec).astype(BF16), v)


def _retention(p, log_g, ctx):
    bsz, L, n = p.shape
    d = n // 6
    dk = d // RET_HEADS
    dv = 2 * d // RET_HEADS
    C = RET_CHUNK
    nc, ncc = L // C, ctx // C

    def chunk(dr, c):
        back = jnp.where(c < ncc, ncc - 1 - c, nc + ncc - 1 - c)
        return jnp.where(dr == 0, c, back)

    lg = jnp.broadcast_to(log_g.reshape(2 * RET_HEADS, 1, 1), (2 * RET_HEADS, 1, LANES))
    return pl.pallas_call(
        _ret_kernel,
        grid=(bsz, RET_HEADS, 2, nc),
        in_specs=[
            pl.BlockSpec((1, 1, LANES), lambda b, h, dr, c: (dr * RET_HEADS + h, 0, 0)),
            pl.BlockSpec((1, C, dk), lambda b, h, dr, c: (b, chunk(dr, c), h)),
            pl.BlockSpec((1, C, dk), lambda b, h, dr, c: (b, chunk(dr, c), RET_HEADS + h)),
            pl.BlockSpec((1, C, dv), lambda b, h, dr, c: (b, chunk(dr, c), RET_HEADS + h)),
        ],
        out_specs=pl.BlockSpec((1, 1, C, dv), lambda b, h, dr, c: (dr, b, chunk(dr, c), h)),
        out_shape=jax.ShapeDtypeStruct((2, bsz, L, 2 * d), BF16),
        scratch_shapes=[pltpu.VMEM((dk, dv), F32)],
        compiler_params=_cparams("parallel", "parallel", "arbitrary", "arbitrary"),
    )(lg, p, p, p)


def _piece_masks(rows):
    lane = lax.broadcasted_iota(jnp.int32, (rows, LANES), 1)
    return [(lane >= SSM_GROUP * p) & (lane < SSM_GROUP * (p + 1)) for p in range(LANES // SSM_GROUP)]


def _to_group_major(u_scr, o_ref):
    n, hg = SSM_CHUNK, SSM_GROUP
    per = LANES // hg
    ntile, tm, _ = u_scr.shape
    rows = tm // n
    masks = _piece_masks(rows)
    for lt in range(ntile):
        for th in range(n * hg // LANES):
            src = [u_scr[lt, pl.ds(th * per + tp, rows, stride=n), :] for tp in range(per)]
            for j in range(per):
                acc = None
                for tp in range(per):
                    shift = (hg * (tp - j)) % LANES
                    piece = src[tp] if shift == 0 else pltpu.roll(src[tp], shift, 1)
                    acc = piece if acc is None else jnp.where(masks[tp], piece, acc)
                o_ref[0, lt * per + j, :, th * LANES:(th + 1) * LANES] = acc.astype(o_ref.dtype)


def _from_group_major(y_ref, o_scr):
    n, hg = SSM_CHUNK, SSM_GROUP
    per = LANES // hg
    ntile, tm, _ = o_scr.shape
    rows = tm // n
    masks = _piece_masks(rows)
    for t in range(n):
        th, tp = divmod(t, per)
        for lt in range(ntile):
            acc = None
            for j in range(per):
                piece = y_ref[0, lt * per + j, :, th * LANES:(th + 1) * LANES]
                shift = (hg * (j - tp)) % LANES
                piece = piece if shift == 0 else pltpu.roll(piece, shift, 1)
                acc = piece if acc is None else jnp.where(masks[j], piece, acc)
            o_scr[lt, pl.ds(t, rows, stride=n), :] = acc


def _normmod_kernel(x_ref, mod_ref, g_ref, o_ref, u_scr):
    mod = mod_ref[0]
    u = _rms(x_ref[0], g_ref[0:1, :]) * (1.0 + mod[1:2, :]) + mod[0:1, :]
    for lt in range(u_scr.shape[0]):
        u_scr[lt] = u[:, lt * LANES:(lt + 1) * LANES]
    _to_group_major(u_scr, o_ref)


def _normmod(hs, mod_l, gains, nctx_tiles):
    bsz, L, d = hs.shape
    G, rows, width = d // SSM_GROUP, TM // SSM_CHUNK, SSM_CHUNK * SSM_GROUP
    return pl.pallas_call(
        _normmod_kernel,
        grid=(bsz, L // TM),
        in_specs=[
            pl.BlockSpec((1, TM, d), lambda b, i: (b, i, 0)),
            _mod_spec(d, bsz, nctx_tiles),
            pl.BlockSpec(gains.shape, lambda b, i: (0, 0)),
        ],
        out_specs=pl.BlockSpec((1, G, rows, width), lambda b, i: (b, 0, i, 0)),
        out_shape=jax.ShapeDtypeStruct((bsz, G, L // SSM_CHUNK, width), BF16),
        scratch_shapes=[pltpu.VMEM((d // LANES, TM, LANES), F32)],
        compiler_params=_cparams("parallel", "parallel"),
    )(hs, mod_l, gains)


def _s5_kernel(x_ref, t_ref, wb_ref, wc_ref, a_ref, y_ref, st_scr, *, nctx):
    gb = x_ref.shape[1]
    nch = x_ref.shape[2]
    for gi in range(gb):
        x = x_ref[0, gi]
        for dr in range(2):
            s = _dot(x, wb_ref[gi, dr])
            st_scr[2 * dr, pl.ds(gi, nch, stride=gb), :] = s[:, :LANES]
            st_scr[2 * dr + 1, pl.ds(gi, nch, stride=gb), :] = s[:, LANES:]

    a = a_ref[...]

    def step(dr, row, h_re, h_im):
        s_re = st_scr[2 * dr, pl.ds(row, gb), :]
        s_im = st_scr[2 * dr + 1, pl.ds(row, gb), :]
        st_scr[2 * dr, pl.ds(row, gb), :] = h_re
        st_scr[2 * dr + 1, pl.ds(row, gb), :] = h_im
        a_re, a_im = a[dr, 0], a[dr, 1]
        return a_re * h_re - a_im * h_im + s_re, a_re * h_im + a_im * h_re + s_im

    def body(t, carry):
        f_re, f_im, b_re, b_im = carry
        cb = jnp.where(t < nctx, nctx - 1 - t, nch + nctx - 1 - t)
        f_re, f_im = step(0, pl.multiple_of(t * gb, gb), f_re, f_im)
        b_re, b_im = step(1, pl.multiple_of(cb * gb, gb), b_re, b_im)
        return f_re, f_im, b_re, b_im

    z = jnp.zeros((gb, LANES), F32)
    lax.fori_loop(0, nch, body, (z, z, z, z))

    for gi in range(gb):
        y = _dot(x_ref[0, gi], t_ref[gi])
        for dr in range(2):
            h = jnp.concatenate([st_scr[2 * dr, pl.ds(gi, nch, stride=gb), :],
                                 st_scr[2 * dr + 1, pl.ds(gi, nch, stride=gb), :]], axis=1)
            y = y + _dot(h.astype(BF16), wc_ref[gi, dr])
        y_ref[0, gi] = y


def _s5_weights(lam_re, lam_im, log_dt, b_re, b_im, c_re, c_im):
    n = SSM_CHUNK
    lam = lax.complex(lam_re.astype(F32), lam_im.astype(F32))
    ldt = lam * jnp.exp(log_dt.astype(F32))[..., None]
    lam_bar = jnp.exp(ldt)
    b_bar = ((lam_bar - 1.0) / lam)[..., None] * lax.complex(b_re.astype(F32), b_im.astype(F32))
    c_mat = lax.complex(c_re.astype(F32), c_im.astype(F32))
    steps = jnp.arange(n + 1, dtype=F32)
    pw = jnp.exp(ldt[None] * steps[:, None, None, None])
    G, P, Hg = b_bar.shape[1:]

    kern = jnp.real(jnp.einsum('kdgp,dgjp,dgpi->dkgij', pw[:n], c_mat, b_bar))
    s_idx = jnp.arange(n)[:, None]
    t_idx = jnp.arange(n)[None, :]
    lag = t_idx - s_idx
    kf = jnp.where((lag >= 0)[..., None, None, None], kern[0][jnp.clip(lag, 0, n - 1)], 0.0)
    kb = jnp.where((lag <= 0)[..., None, None, None], kern[1][jnp.clip(-lag, 0, n - 1)], 0.0)
    t_mat = (kf + kb).transpose(2, 0, 3, 1, 4).reshape(G, n * Hg, n * Hg)

    zpad = jnp.zeros((G, n * Hg, LANES - P), F32)
    zrow = jnp.zeros((G, LANES - P, n * Hg), F32)
    wbs, wcs = [], []
    for dr in range(2):
        e_in = pw[:n][::-1, dr] if dr == 0 else pw[:n, dr]
        wb = (e_in[:, :, :, None] * b_bar[dr][None]).transpose(1, 0, 3, 2).reshape(G, n * Hg, P)
        wbs.append(jnp.concatenate([jnp.real(wb), zpad, jnp.imag(wb), zpad], axis=2))
        e_out = pw[1:, dr] if dr == 0 else pw[1:][::-1, dr]
        wc = (c_mat[dr][None] * e_out[:, :, None, :]).transpose(1, 3, 0, 2).reshape(G, P, n * Hg)
        wcs.append(jnp.concatenate([jnp.real(wc), zrow, -jnp.imag(wc), zrow], axis=1))
    w_b = jnp.stack(wbs, axis=1)
    w_c = jnp.stack(wcs, axis=1)
    a_n = pw[n]
    pad = jnp.zeros((2, G, LANES - P), F32)
    a = jnp.stack([jnp.concatenate([jnp.real(a_n), pad], -1),
                   jnp.concatenate([jnp.imag(a_n), pad], -1)], axis=1)
    return t_mat.astype(BF16), w_b.astype(BF16), w_c.astype(BF16), a


def _s5_core(x, weights, ctx):
    bsz, G, nch, width = x.shape
    n, gb = SSM_CHUNK, SSM_GROUP_BLOCK
    t_mat, w_b, w_c, a = weights
    return pl.pallas_call(
        functools.partial(_s5_kernel, nctx=ctx // n),
        grid=(bsz, G // gb),
        in_specs=[
            pl.BlockSpec((1, gb, nch, width), lambda b, g: (b, g, 0, 0)),
            pl.BlockSpec((gb, width, width), lambda b, g: (g, 0, 0)),
            pl.BlockSpec((gb, 2, width, 2 * LANES), lambda b, g: (g, 0, 0, 0)),
            pl.BlockSpec((gb, 2, 2 * LANES, width), lambda b, g: (g, 0, 0, 0)),
            pl.BlockSpec((2, 2, gb, LANES), lambda b, g: (0, 0, g, 0)),
        ],
        out_specs=pl.BlockSpec((1, gb, nch, width), lambda b, g: (b, g, 0, 0)),
        out_shape=jax.ShapeDtypeStruct((bsz, G, nch, width), F32),
        scratch_shapes=[pltpu.VMEM((4, nch * gb, LANES), F32)],
        compiler_params=_cparams("parallel", "parallel"),
    )(x, t_mat, w_b, w_c, a)


def _finish(h, y, mod, g_ref, o_ref):
    o_ref[0] = h + mod[2:3, :] * _rms(y, g_ref[1:2, :])


def _attn_out_kernel(h_ref, mod_ref, g_ref, o_in_ref, w_ref, o_ref):
    _finish(h_ref[0], _dot(o_in_ref[0], w_ref[...]), mod_ref[0], g_ref, o_ref)


def _ret_out_kernel(h_ref, mod_ref, g_ref, o_in_ref, gate_ref, w_ref, o_ref):
    o = o_in_ref[0, 0].astype(F32) + o_in_ref[1, 0].astype(F32)
    dv = o.shape[1] // RET_HEADS
    parts = []
    for hd in range(RET_HEADS):
        oh = o[:, hd * dv:(hd + 1) * dv]
        parts.append(oh * lax.rsqrt(jnp.mean(oh * oh, axis=-1, keepdims=True) + NORM_EPS))
    o = jnp.concatenate(parts, axis=1)
    y = _dot((_silu(gate_ref[0].astype(F32)) * o).astype(BF16), w_ref[...])
    _finish(h_ref[0], y, mod_ref[0], g_ref, o_ref)


def _s5_out_kernel(h_ref, mod_ref, g_ref, y_ref, dskip_ref, w_ref, o_ref, y_scr):
    h = h_ref[0]
    mod = mod_ref[0]
    d = h.shape[1]
    _from_group_major(y_ref, y_scr)
    u = _rms(h, g_ref[0:1, :]) * (1.0 + mod[1:2, :]) + mod[0:1, :]
    y_ssm = jnp.concatenate([y_scr[lt] for lt in range(y_scr.shape[0])], axis=1)
    z = y_ssm + dskip_ref[...] * u
    gl = 0.5 * z * (1.0 + jnp.tanh(math.sqrt(2.0 / math.pi) * (z + 0.044715 * (z * z * z))))
    gl = gl.astype(BF16)
    y = _dot(gl, w_ref[:, :d]) * jax.nn.sigmoid(_dot(gl, w_ref[:, d:]))
    _finish(h, y, mod, g_ref, o_ref)


def _mixer_out(kernel, hs, mod_l, gains, nctx_tiles, extra, extra_specs, scratch=()):
    bsz, L, d = hs.shape
    return pl.pallas_call(
        kernel,
        grid=(bsz, L // TM),
        in_specs=[
            pl.BlockSpec((1, TM, d), lambda b, i: (b, i, 0)),
            _mod_spec(d, bsz, nctx_tiles),
            pl.BlockSpec(gains.shape, lambda b, i: (0, 0)),
        ] + extra_specs,
        out_specs=pl.BlockSpec((1, TM, d), lambda b, i: (b, i, 0)),
        out_shape=jax.ShapeDtypeStruct((bsz, L, d), F32),
        scratch_shapes=list(scratch),
        compiler_params=_cparams("parallel", "parallel"),
    )(hs, mod_l, gains, *extra)


def _full_spec(a):
    nd = a.ndim
    return pl.BlockSpec(a.shape, lambda b, i: (0,) * nd)


def _ffn_kernel(h_ref, mod_ref, g_ref, wg_ref, wu_ref, wd_ref, o_ref):
    h = h_ref[0]
    mod = mod_ref[0]
    v = (_rms(h, g_ref[2:3, :]) * (1.0 + mod[4:5, :]) + mod[3:4, :]).astype(BF16)
    act = (_silu(_dot(v, wg_ref[...])) * _dot(v, wu_ref[...])).astype(BF16)
    y = _dot(act, wd_ref[...])
    o_ref[0] = h + mod[5:6, :] * _rms(y, g_ref[3:4, :])


def _ffn(hs, mod_l, gains, wg, wu, wd, nctx_tiles):
    bsz, L, d = hs.shape
    return pl.pallas_call(
        _ffn_kernel,
        grid=(bsz, L // TM),
        in_specs=[
            pl.BlockSpec((1, TM, d), lambda b, i: (b, i, 0)),
            _mod_spec(d, bsz, nctx_tiles),
            pl.BlockSpec(gains.shape, lambda b, i: (0, 0)),
            _full_spec(wg), _full_spec(wu), _full_spec(wd),
        ],
        out_specs=pl.BlockSpec((1, TM, d), lambda b, i: (b, i, 0)),
        out_shape=jax.ShapeDtypeStruct((bsz, L, d), F32),
        compiler_params=_cparams("parallel", "parallel"),
    )(hs, mod_l, gains, wg, wu, wd)


def kernel(x, c, ctx, c_ctx, mod_w, mod_b, norm_g, attn_w_in, attn_w_out, attn_lambda, attn_subln,
           ret_w_in, ret_w_out, ret_decay_logit, ssm_lambda_re, ssm_lambda_im, ssm_log_dt,
           ssm_b_re, ssm_b_im, ssm_c_re, ssm_c_im, ssm_d, ssm_w_glu, ffn_w_gate, ffn_w_up, ffn_w_down):
    bsz, seq, d = x.shape
    nctx = ctx.shape[1]
    depth = mod_w.shape[0]
    assert nctx % TM == 0 and seq % TM == 0 and seq % GRID_W == 0
    nct = nctx // TM
    L = nctx + seq

    mods = _modulation(c, c_ctx, mod_w, mod_b)
    hs = jnp.concatenate([ctx, x], axis=1)
    da_dim = d // (2 * DA_HEADS)
    dk = d // RET_HEADS
    tab_a = _rope_table(seq, nctx, da_dim)
    tab_r = _rope_table(seq, nctx, dk)

    def tok_spec(width, col=0):
        return pl.BlockSpec((1, TM, width), lambda b, i: (b, i, col))

    for l in range(depth):
        kind, j = l % 3, l // 3
        mod_l, gains = mods[l], norm_g[l]
        if kind == 0:
            lam_init = 0.8 - 0.6 * math.exp(-0.3 * l)
            q_t, k, v_t = _attn_proj(hs, mod_l, gains, attn_w_in[j].astype(BF16), tab_a,
                                     da_dim ** -0.5 * math.log2(math.e), da_dim // 4, nct)
            o = _attention(q_t, k, v_t, attn_lambda[j], attn_subln[j], nctx, lam_init)
            w_out = attn_w_out[j].astype(BF16)
            hs = _mixer_out(_attn_out_kernel, hs, mod_l, gains, nct, (o, w_out),
                            [tok_spec(d), _full_spec(w_out)])
        elif kind == 1:
            chunks = ((0, d, 1.0, True), (d, d, dk ** -0.5, True)) + tuple(
                (c0, d, 1.0, False) for c0 in range(2 * d, 6 * d, d))
            p = _proj_in(hs, mod_l, gains, ret_w_in[j].astype(BF16), tab_r, chunks, dk // 4, nct)
            log_g = jax.nn.log_sigmoid(ret_decay_logit[j].astype(F32))
            o = _retention(p, log_g, nctx)
            w_out = ret_w_out[j].astype(BF16)
            hs = _mixer_out(_ret_out_kernel, hs, mod_l, gains, nct, (o, p, w_out),
                            [pl.BlockSpec((2, 1, TM, 2 * d), lambda b, i: (0, b, i, 0)),
                             tok_spec(2 * d, 2), _full_spec(w_out)])
        else:
            u = _normmod(hs, mod_l, gains, nct)
            weights = _s5_weights(ssm_lambda_re[j], ssm_lambda_im[j], ssm_log_dt[j], ssm_b_re[j],
                                  ssm_b_im[j], ssm_c_re[j], ssm_c_im[j])
            y = _s5_core(u, weights, nctx)
            w_glu = ssm_w_glu[j].astype(BF16)
            dskip = ssm_d[j].astype(F32).reshape(1, d)
            y_spec = pl.BlockSpec((1,) + y.shape[1:2] + (TM // SSM_CHUNK,) + y.shape[3:],
                                  lambda b, i: (b, 0, i, 0))
            hs = _mixer_out(_s5_out_kernel, hs, mod_l, gains, nct, (y, dskip, w_glu),
                            [y_spec, _full_spec(dskip), _full_spec(w_glu)],
                            scratch=[pltpu.VMEM((d // LANES, TM, LANES), F32)])
        hs = _ffn(hs, mod_l, gains, ffn_w_gate[l].astype(BF16), ffn_w_up[l].astype(BF16),
                  ffn_w_down[l].astype(BF16), nct)
    return hs[:, nctx:, :]
```

```python
import functools
import math

import jax
import jax.numpy as jnp
from jax import lax
from jax.experimental import pallas as pl
from jax.experimental.pallas import tpu as pltpu

F32 = jnp.float32
BF16 = jnp.bfloat16

GRID_W = 64
DA_HEADS = 8
RET_HEADS = 4
RET_CHUNK = 256
SSM_GROUP = 16
SSM_STATE = 64
SSM_CHUNK = 16
SSM_GROUP_BLOCK = 8
ROPE_THETA = 10000.0
NORM_EPS = 1e-6
LANES = 128
TM = 256
ATTN_TQ = 512
VT_PAD = 16
VMEM_LIMIT = 56 * 1024 * 1024


def _cparams(*sem):
    return pltpu.CompilerParams(dimension_semantics=sem, vmem_limit_bytes=VMEM_LIMIT)


def _rms(x, gain):
    return x * lax.rsqrt(jnp.mean(x * x, axis=-1, keepdims=True) + NORM_EPS) * gain


def _silu(x):
    return x * jax.nn.sigmoid(x)


def _dot(a, b):
    return jnp.dot(a, b, preferred_element_type=F32)


def _dot_nt(a, b):
    return lax.dot_general(a, b, (((1,), (1,)), ((), ())), preferred_element_type=F32)


def _dot_tn(a, b):
    return lax.dot_general(a, b, (((0,), (0,)), ((), ())), preferred_element_type=F32)


def _mod_kernel(s_ref, w_ref, b_ref, o_ref):
    s = _silu(s_ref[...])
    o_ref[0] = jnp.dot(s, w_ref[0], preferred_element_type=F32,
                       precision=lax.Precision.HIGHEST) + b_ref[0]


def _modulation(c, c_ctx, mod_w, mod_b):
    depth, d, n = mod_w.shape
    bsz = c.shape[0]
    rows = 8
    s = jnp.zeros((rows, d), F32).at[:bsz].set(c).at[bsz].set(c_ctx)
    tn = 1536
    out = pl.pallas_call(
        _mod_kernel,
        grid=(depth, n // tn),
        in_specs=[
            pl.BlockSpec((rows, d), lambda l, j: (0, 0)),
            pl.BlockSpec((1, d, tn), lambda l, j: (l, 0, j)),
            pl.BlockSpec((1, 1, tn), lambda l, j: (l, 0, j)),
        ],
        out_specs=pl.BlockSpec((1, rows, tn), lambda l, j: (l, 0, j)),
        out_shape=jax.ShapeDtypeStruct((depth, rows, n), F32),
        compiler_params=_cparams("parallel", "parallel"),
    )(s, mod_w, mod_b.reshape(depth, 1, n))
    return out.reshape(depth, rows, 6, d)


def _mod_spec(d, bsz, nctx_tiles):
    return pl.BlockSpec((1, 6, d), lambda b, i: (jnp.where(i < nctx_tiles, bsz, b), 0, 0))


def _rope_trig(fn, seq, head_dim):
    quarter = head_dim // 4
    freqs = ROPE_THETA ** (-jnp.arange(quarter, dtype=F32) / quarter)
    rows = seq // GRID_W
    by_row = jnp.repeat(fn(jnp.arange(rows, dtype=F32)[:, None] * freqs), GRID_W, axis=0)
    by_col = jnp.tile(fn(jnp.arange(GRID_W, dtype=F32)[:, None] * freqs), (rows, 1))
    return jnp.concatenate([by_row, by_row, by_col, by_col], axis=-1)


def _rope_table(seq, ctx, head_dim):
    cos = jnp.concatenate([jnp.ones((ctx, head_dim), F32), _rope_trig(jnp.cos, seq, head_dim)], axis=0)
    sin = jnp.concatenate([jnp.zeros((ctx, head_dim), F32), _rope_trig(jnp.sin, seq, head_dim)], axis=0)
    q = head_dim // 4
    first = (jnp.arange(head_dim) % (2 * q)) < q
    s_minus = jnp.where(first, -sin, 0.0)
    s_plus = jnp.where(first, 0.0, sin)
    rep = max(1, LANES // head_dim)
    return jnp.concatenate([jnp.tile(t, (1, rep)) for t in (cos, s_minus, s_plus)], axis=1)


def _apply_rope(y, tab, quarter):
    w = tab.shape[1] // 3
    cos, s_minus, s_plus = tab[:, :w], tab[:, w:2 * w], tab[:, 2 * w:]
    outs = []
    for k in range(y.shape[1] // LANES):
        yk = y[:, k * LANES:(k + 1) * LANES]
        t0 = (k * LANES) % w
        ck, mk, pk = (t[:, t0:t0 + LANES] for t in (cos, s_minus, s_plus))
        if 2 * quarter == LANES:
            outs.append(yk * ck + pltpu.roll(yk, quarter, 1) * (mk + pk))
        else:
            outs.append(yk * ck + pltpu.roll(yk, LANES - quarter, 1) * mk
                        + pltpu.roll(yk, quarter, 1) * pk)
    return jnp.concatenate(outs, axis=1)


def _proj_in_kernel(x_ref, mod_ref, g_ref, w_ref, tab_ref, o_ref, *, chunks, quarter):
    mod = mod_ref[0]
    u = (_rms(x_ref[0], g_ref[0:1, :]) * (1.0 + mod[1:2, :]) + mod[0:1, :]).astype(BF16)
    tab = tab_ref[...]
    for c0, width, mult, rope in chunks:
        y = _dot(u, w_ref[:, c0:c0 + width])
        if mult != 1.0:
            y = y * mult
        if rope:
            y = _apply_rope(y, tab, quarter)
        o_ref[0, :, c0:c0 + width] = y.astype(BF16)


def _proj_in(hs, mod_l, gains, w, tab, chunks, quarter, nctx_tiles):
    bsz, L, d = hs.shape
    n = w.shape[1]
    return pl.pallas_call(
        functools.partial(_proj_in_kernel, chunks=chunks, quarter=quarter),
        grid=(bsz, L // TM),
        in_specs=[
            pl.BlockSpec((1, TM, d), lambda b, i: (b, i, 0)),
            _mod_spec(d, bsz, nctx_tiles),
            pl.BlockSpec(gains.shape, lambda b, i: (0, 0)),
            pl.BlockSpec(w.shape, lambda b, i: (0, 0)),
            pl.BlockSpec((TM, tab.shape[1]), lambda b, i: (i, 0)),
        ],
        out_specs=pl.BlockSpec((1, TM, n), lambda b, i: (b, i, 0)),
        out_shape=jax.ShapeDtypeStruct((bsz, L, n), BF16),
        compiler_params=_cparams("parallel", "parallel"),
    )(hs, mod_l, gains, w, tab)


def _attn_proj_kernel(x_ref, mod_ref, g_ref, w_ref, tab_ref, qt_ref, k_ref, vt_ref, *, scale, quarter):
    mod = mod_ref[0]
    u = (_rms(x_ref[0], g_ref[0:1, :]) * (1.0 + mod[1:2, :]) + mod[0:1, :]).astype(BF16)
    d = x_ref.shape[2]
    tab = tab_ref[...]
    heads = vt_ref.shape[1]
    qt_ref[0] = _apply_rope(_dot(u, w_ref[:, :d]) * scale, tab, quarter).T.astype(BF16)
    k_ref[0] = _apply_rope(_dot(u, w_ref[:, d:2 * d]), tab, quarter).astype(BF16)
    vt = _dot(u, w_ref[:, 2 * d:]).T.astype(BF16)
    vt_ref[0, :, 0, 0:LANES, :] = vt.reshape(heads, LANES, vt.shape[1])
    vt_ref[0, :, 0, LANES:, :] = jnp.ones((heads, VT_PAD, vt.shape[1]), BF16)


def _attn_proj(hs, mod_l, gains, w, tab, scale, quarter, nctx_tiles):
    bsz, L, d = hs.shape
    nlat = L // TM - nctx_tiles
    return pl.pallas_call(
        functools.partial(_attn_proj_kernel, scale=scale, quarter=quarter),
        grid=(bsz, L // TM),
        in_specs=[
            pl.BlockSpec((1, TM, d), lambda b, i: (b, i, 0)),
            _mod_spec(d, bsz, nctx_tiles),
            pl.BlockSpec(gains.shape, lambda b, i: (0, 0)),
            pl.BlockSpec(w.shape, lambda b, i: (0, 0)),
            pl.BlockSpec((TM, tab.shape[1]), lambda b, i: (i, 0)),
        ],
        out_specs=[
            pl.BlockSpec((1, d, TM), lambda b, i: (b, 0, jnp.where(i < nctx_tiles, i + nlat, i - nctx_tiles))),
            pl.BlockSpec((1, TM, d), lambda b, i: (b, i, 0)),
            pl.BlockSpec((1, DA_HEADS, 1, LANES + VT_PAD, TM), lambda b, i: (b, 0, i, 0, 0)),
        ],
        out_shape=[
            jax.ShapeDtypeStruct((bsz, d, L), BF16),
            jax.ShapeDtypeStruct((bsz, L, d), BF16),
            jax.ShapeDtypeStruct((bsz, DA_HEADS, L // TM, LANES + VT_PAD, TM), BF16),
        ],
        compiler_params=_cparams("parallel", "parallel"),
    )(hs, mod_l, gains, w, tab)


def _attn_kernel(lam_ref, sub_ref, qt_ref, k_ref, vt_ref, o_ref,
                 s_c, s_a, s_b, mb_scr, m_scr, acc_scr, *, nb, lam_init):
    ctx = s_c.shape[1]
    tk = s_a.shape[1]
    kv = vt_ref.shape[4]
    qt = qt_ref[0]
    row = lax.broadcasted_iota(jnp.int32, qt.shape, 0)
    zero = jnp.zeros_like(qt)
    qts = (jnp.where(row < LANES // 2, qt, zero), jnp.where(row >= LANES // 2, qt, zero))

    slot = {id(s_c): 0, id(s_a): 1, id(s_b): 2}

    def scores(kb, s_out):
        for part in range(2):
            s = _dot(kb, qts[part])
            s_out[part] = s
            mb_scr[slot[id(s_out)], part] = jnp.max(s, axis=0, keepdims=True)

    def stage(s_in, vt0, nxt):
        if nxt is not None:
            scores(*nxt)
        for part in range(2):
            s = s_in[part]
            m = m_scr[part]
            m_new = jnp.maximum(m, mb_scr[slot[id(s_in)], part])
            alpha = jnp.exp2(m - m_new)
            p = jnp.exp2(s - m_new).astype(BF16)
            pv = _dot(vt_ref[0, 0, vt0], p[0:kv])
            for n in range(1, s.shape[0] // kv):
                pv = pv + _dot(vt_ref[0, 0, vt0 + n], p[n * kv:(n + 1) * kv])
            m_scr[part] = m_new
            acc_scr[part] = alpha * acc_scr[part] + pv

    m_scr[...] = jnp.full(m_scr.shape, -1e30, F32)
    acc_scr[...] = jnp.zeros(acc_scr.shape, F32)

    bufs = (s_b, s_a)

    def keys(t):
        return k_ref[0, pl.ds(pl.multiple_of(ctx + (t - 1) * tk, LANES), tk), :]

    def vt_index(t):
        return ctx // kv + (t - 1) * (tk // kv)

    scores(k_ref[0, 0:ctx, :], s_c)
    stage(s_c, 0, (keys(1), s_a) if nb >= 1 else None)
    n_pairs = max(0, (nb - 1) // 2 if nb % 2 else (nb - 2) // 2)

    def pair(u, carry):
        t = 2 * u + 1
        stage(s_a, vt_index(t), (keys(t + 1), s_b))
        stage(s_b, vt_index(t + 1), (keys(t + 2), s_a))
        return carry

    if n_pairs > 0:
        lax.fori_loop(0, n_pairs, pair, 0)
    for t in range(2 * n_pairs + 1, nb + 1):
        stage(bufs[t % 2], vt_index(t), (keys(t + 1), bufs[(t + 1) % 2]) if t < nb else None)

    lv = lam_ref[...]
    lam = (jnp.exp(jnp.sum(lv[0:1] * lv[1:2], axis=-1, keepdims=True))
           - jnp.exp(jnp.sum(lv[2:3] * lv[3:4], axis=-1, keepdims=True)) + lam_init)
    a1, a2 = acc_scr[0], acc_scr[1]
    o = (a1[:LANES] / a1[LANES:LANES + 1] - lam * (a2[:LANES] / a2[LANES:LANES + 1])).T
    o = _rms(o, sub_ref[...]) * (1.0 - lam_init)
    o_ref[0] = o.astype(BF16)


def _attention_call(q_t, k, v_t, lam_vec, subln, ctx, tk, nb, lam_init, tq, q_off, n_out):
    bsz, d, _ = q_t.shape
    kv = v_t.shape[4]
    k_rows = ctx + nb * tk
    return pl.pallas_call(
        functools.partial(_attn_kernel, nb=nb, lam_init=lam_init),
        grid=(bsz, DA_HEADS, n_out // tq),
        in_specs=[
            pl.BlockSpec(lam_vec.shape, lambda b, h, i: (0, 0)),
            pl.BlockSpec((1, LANES), lambda b, h, i: (0, 0)),
            pl.BlockSpec((1, LANES, tq), lambda b, h, i: (b, h, i + q_off // tq)),
            pl.BlockSpec((1, k_rows, LANES), lambda b, h, i: (b, 0, h)),
            pl.BlockSpec((1, 1, k_rows // kv) + v_t.shape[3:], lambda b, h, i: (b, h, 0, 0, 0)),
        ],
        out_specs=pl.BlockSpec((1, tq, LANES), lambda b, h, i: (b, i, h)),
        out_shape=jax.ShapeDtypeStruct((bsz, n_out, d), BF16),
        scratch_shapes=[pltpu.VMEM((2, ctx, tq), F32), pltpu.VMEM((2, tk, tq), F32),
                        pltpu.VMEM((2, tk, tq), F32), pltpu.VMEM((3, 2, 1, tq), F32),
                        pltpu.VMEM((2, 1, tq), F32), pltpu.VMEM((2, LANES + VT_PAD, tq), F32)],
        compiler_params=_cparams("parallel", "parallel", "arbitrary"),
    )(lam_vec, subln.reshape(1, LANES), q_t, k, v_t)


def _attention(q_t, k, v_t, lam_vec, subln, ctx, lam_init, with_ctx):
    L = k.shape[1]
    seq = L - ctx
    tk = min(1024, seq)
    nb = seq // tk
    tq = min(ATTN_TQ, seq)
    o_lat = _attention_call(q_t, k, v_t, lam_vec, subln, ctx, tk, nb, lam_init, tq, 0, seq)
    if not with_ctx:
        return o_lat
    o_ctx = _attention_call(q_t, k, v_t, lam_vec, subln, ctx, tk, 0, lam_init, TM, seq, ctx)
    return jnp.concatenate([o_ctx, o_lat], axis=1)


def _ret_kernel(lg_ref, q_ref, k_ref, v_ref, o_ref, r_scr):
    d = pl.program_id(1)
    c = pl.program_id(2)
    C = q_ref.shape[1]
    heads, dk, dv = r_scr.shape

    @pl.when(c == 0)
    def _():
        r_scr[...] = jnp.zeros_like(r_scr)

    fwd = d == 0
    ii = lax.broadcasted_iota(jnp.int32, (C, C), 0)
    jj = lax.broadcasted_iota(jnp.int32, (C, C), 1)
    dd = jnp.where(fwd, ii - jj, jj - ii)
    causal = dd >= 0
    lag = jnp.maximum(dd, 0).astype(F32)
    idx = lax.broadcasted_iota(jnp.int32, (C, 1), 0)
    q_lag = jnp.where(fwd, idx + 1, C - idx).astype(F32)
    k_lag = jnp.where(fwd, C - 1 - idx, idx).astype(F32)

    for h in range(heads):
        lg = lg_ref[0, h:h + 1, 0:1]
        q = q_ref[0, :, h * dk:(h + 1) * dk]
        k = k_ref[0, :, h * dk:(h + 1) * dk]
        v = v_ref[0, :, h * dv:(h + 1) * dv]
        r = r_scr[h]
        s = _dot_nt(q, k) * jnp.where(causal, jnp.exp(lg * lag), 0.0)
        inner = _dot(s.astype(BF16), v)
        cross = _dot((q.astype(F32) * jnp.exp(lg * q_lag)).astype(BF16), r.astype(BF16))
        o_ref[0, 0, :, h * dv:(h + 1) * dv] = (inner + cross).astype(BF16)
        r_scr[h] = jnp.exp(lg * float(C)) * r + _dot_tn((k.astype(F32) * jnp.exp(lg * k_lag)).astype(BF16), v)


def _retention(p, log_g, ctx):
    bsz, L, n = p.shape
    d = n // 6
    C = RET_CHUNK
    nc, ncc = L // C, ctx // C

    def chunk(dr, c):
        back = jnp.where(c < ncc, ncc - 1 - c, nc + ncc - 1 - c)
        return jnp.where(dr == 0, c, back)

    lg = jnp.broadcast_to(log_g.reshape(2, RET_HEADS, 1), (2, RET_HEADS, LANES))
    return pl.pallas_call(
        _ret_kernel,
        grid=(bsz, 2, nc),
        in_specs=[
            pl.BlockSpec((1, RET_HEADS, LANES), lambda b, dr, c: (dr, 0, 0)),
            pl.BlockSpec((1, C, d), lambda b, dr, c: (b, chunk(dr, c), 0)),
            pl.BlockSpec((1, C, d), lambda b, dr, c: (b, chunk(dr, c), 1)),
            pl.BlockSpec((1, C, 2 * d), lambda b, dr, c: (b, chunk(dr, c), 1)),
        ],
        out_specs=pl.BlockSpec((1, 1, C, 2 * d), lambda b, dr, c: (dr, b, chunk(dr, c), 0)),
        out_shape=jax.ShapeDtypeStruct((2, bsz, L, 2 * d), BF16),
        scratch_shapes=[pltpu.VMEM((RET_HEADS, d // RET_HEADS, 2 * d // RET_HEADS), F32)],
        compiler_params=_cparams("parallel", "arbitrary", "arbitrary"),
    )(lg, p, p, p)


def _piece_masks(rows):
    lane = lax.broadcasted_iota(jnp.int32, (rows, LANES), 1)
    return [(lane >= SSM_GROUP * p) & (lane < SSM_GROUP * (p + 1)) for p in range(LANES // SSM_GROUP)]


def _to_group_major(u_scr, o_ref):
    n, hg = SSM_CHUNK, SSM_GROUP
    per = LANES // hg
    ntile, tm, _ = u_scr.shape
    rows = tm // n
    masks = _piece_masks(rows)
    for lt in range(ntile):
        for th in range(n * hg // LANES):
            src = [u_scr[lt, pl.ds(th * per + tp, rows, stride=n), :] for tp in range(per)]
            for j in range(per):
                acc = None
                for tp in range(per):
                    shift = (hg * (tp - j)) % LANES
                    piece = src[tp] if shift == 0 else pltpu.roll(src[tp], shift, 1)
                    acc = piece if acc is None else jnp.where(masks[tp], piece, acc)
                o_ref[0, lt * per + j, :, th * LANES:(th + 1) * LANES] = acc.astype(o_ref.dtype)


def _from_group_major(y_ref, o_scr):
    n, hg = SSM_CHUNK, SSM_GROUP
    per = LANES // hg
    ntile, tm, _ = o_scr.shape
    rows = tm // n
    masks = _piece_masks(rows)
    for t in range(n):
        th, tp = divmod(t, per)
        for lt in range(ntile):
            acc = None
            for j in range(per):
                piece = y_ref[0, lt * per + j, :, th * LANES:(th + 1) * LANES]
                shift = (hg * (j - tp)) % LANES
                piece = piece if shift == 0 else pltpu.roll(piece, shift, 1)
                acc = piece if acc is None else jnp.where(masks[j], piece, acc)
            o_scr[lt, pl.ds(t, rows, stride=n), :] = acc


def _normmod_kernel(x_ref, mod_ref, g_ref, o_ref, u_scr):
    mod = mod_ref[0]
    u = _rms(x_ref[0], g_ref[0:1, :]) * (1.0 + mod[1:2, :]) + mod[0:1, :]
    for lt in range(u_scr.shape[0]):
        u_scr[lt] = u[:, lt * LANES:(lt + 1) * LANES]
    _to_group_major(u_scr, o_ref)


def _normmod(hs, mod_l, gains, nctx_tiles):
    bsz, L, d = hs.shape
    G, rows, width = d // SSM_GROUP, TM // SSM_CHUNK, SSM_CHUNK * SSM_GROUP
    return pl.pallas_call(
        _normmod_kernel,
        grid=(bsz, L // TM),
        in_specs=[
            pl.BlockSpec((1, TM, d), lambda b, i: (b, i, 0)),
            _mod_spec(d, bsz, nctx_tiles),
            pl.BlockSpec(gains.shape, lambda b, i: (0, 0)),
        ],
        out_specs=pl.BlockSpec((1, G, rows, width), lambda b, i: (b, 0, i, 0)),
        out_shape=jax.ShapeDtypeStruct((bsz, G, L // SSM_CHUNK, width), BF16),
        scratch_shapes=[pltpu.VMEM((d // LANES, TM, LANES), F32)],
        compiler_params=_cparams("parallel", "parallel"),
    )(hs, mod_l, gains)


def _s5_kernel(x_ref, t_ref, wb_ref, wc_ref, a_ref, y_ref, st_scr, *, nctx):
    gb = x_ref.shape[1]
    nch = x_ref.shape[2]
    for gi in range(gb):
        x = x_ref[0, gi]
        for dr in range(2):
            s = _dot(x, wb_ref[gi, dr])
            st_scr[2 * dr, pl.ds(gi, nch, stride=gb), :] = s[:, :LANES]
            st_scr[2 * dr + 1, pl.ds(gi, nch, stride=gb), :] = s[:, LANES:]

    a = a_ref[...]

    def step(dr, row, h_re, h_im):
        s_re = st_scr[2 * dr, pl.ds(row, gb), :]
        s_im = st_scr[2 * dr + 1, pl.ds(row, gb), :]
        st_scr[2 * dr, pl.ds(row, gb), :] = h_re
        st_scr[2 * dr + 1, pl.ds(row, gb), :] = h_im
        a_re, a_im = a[dr, 0], a[dr, 1]
        return a_re * h_re - a_im * h_im + s_re, a_re * h_im + a_im * h_re + s_im

    def body(t, carry):
        f_re, f_im, b_re, b_im = carry
        cb = jnp.where(t < nctx, nctx - 1 - t, nch + nctx - 1 - t)
        f_re, f_im = step(0, pl.multiple_of(t * gb, gb), f_re, f_im)
        b_re, b_im = step(1, pl.multiple_of(cb * gb, gb), b_re, b_im)
        return f_re, f_im, b_re, b_im

    z = jnp.zeros((gb, LANES), F32)
    lax.fori_loop(0, nch, body, (z, z, z, z))

    for gi in range(gb):
        y = _dot(x_ref[0, gi], t_ref[gi])
        for dr in range(2):
            h = jnp.concatenate([st_scr[2 * dr, pl.ds(gi, nch, stride=gb), :],
                                 st_scr[2 * dr + 1, pl.ds(gi, nch, stride=gb), :]], axis=1)
            y = y + _dot(h.astype(BF16), wc_ref[gi, dr])
        y_ref[0, gi] = y


def _s5_weights(lam_re, lam_im, log_dt, b_re, b_im, c_re, c_im):
    n = SSM_CHUNK
    lam = lax.complex(lam_re.astype(F32), lam_im.astype(F32))
    ldt = lam * jnp.exp(log_dt.astype(F32))[..., None]
    lam_bar = jnp.exp(ldt)
    b_bar = ((lam_bar - 1.0) / lam)[..., None] * lax.complex(b_re.astype(F32), b_im.astype(F32))
    c_mat = lax.complex(c_re.astype(F32), c_im.astype(F32))
    steps = jnp.arange(n + 1, dtype=F32)
    pw = jnp.exp(ldt[None] * steps[:, None, None, None])
    G, P, Hg = b_bar.shape[1:]

    kern = jnp.real(jnp.einsum('kdgp,dgjp,dgpi->dkgij', pw[:n], c_mat, b_bar))
    s_idx = jnp.arange(n)[:, None]
    t_idx = jnp.arange(n)[None, :]
    lag = t_idx - s_idx
    kf = jnp.where((lag >= 0)[..., None, None, None], kern[0][jnp.clip(lag, 0, n - 1)], 0.0)
    kb = jnp.where((lag <= 0)[..., None, None, None], kern[1][jnp.clip(-lag, 0, n - 1)], 0.0)
    t_mat = (kf + kb).transpose(2, 0, 3, 1, 4).reshape(G, n * Hg, n * Hg)

    zpad = jnp.zeros((G, n * Hg, LANES - P), F32)
    zrow = jnp.zeros((G, LANES - P, n * Hg), F32)
    wbs, wcs = [], []
    for dr in range(2):
        e_in = pw[:n][::-1, dr] if dr == 0 else pw[:n, dr]
        wb = (e_in[:, :, :, None] * b_bar[dr][None]).transpose(1, 0, 3, 2).reshape(G, n * Hg, P)
        wbs.append(jnp.concatenate([jnp.real(wb), zpad, jnp.imag(wb), zpad], axis=2))
        e_out = pw[1:, dr] if dr == 0 else pw[1:][::-1, dr]
        wc = (c_mat[dr][None] * e_out[:, :, None, :]).transpose(1, 3, 0, 2).reshape(G, P, n * Hg)
        wcs.append(jnp.concatenate([jnp.real(wc), zrow, -jnp.imag(wc), zrow], axis=1))
    w_b = jnp.stack(wbs, axis=1)
    w_c = jnp.stack(wcs, axis=1)
    a_n = pw[n]
    pad = jnp.zeros((2, G, LANES - P), F32)
    a = jnp.stack([jnp.concatenate([jnp.real(a_n), pad], -1),
                   jnp.concatenate([jnp.imag(a_n), pad], -1)], axis=1)
    return t_mat.astype(BF16), w_b.astype(BF16), w_c.astype(BF16), a


def _s5_core(x, weights, ctx):
    bsz, G, nch, width = x.shape
    n, gb = SSM_CHUNK, SSM_GROUP_BLOCK
    t_mat, w_b, w_c, a = weights
    return pl.pallas_call(
        functools.partial(_s5_kernel, nctx=ctx // n),
        grid=(bsz, G // gb),
        in_specs=[
            pl.BlockSpec((1, gb, nch, width), lambda b, g: (b, g, 0, 0)),
            pl.BlockSpec((gb, width, width), lambda b, g: (g, 0, 0)),
            pl.BlockSpec((gb, 2, width, 2 * LANES), lambda b, g: (g, 0, 0, 0)),
            pl.BlockSpec((gb, 2, 2 * LANES, width), lambda b, g: (g, 0, 0, 0)),
            pl.BlockSpec((2, 2, gb, LANES), lambda b, g: (0, 0, g, 0)),
        ],
        out_specs=pl.BlockSpec((1, gb, nch, width), lambda b, g: (b, g, 0, 0)),
        out_shape=jax.ShapeDtypeStruct((bsz, G, nch, width), F32),
        scratch_shapes=[pltpu.VMEM((4, nch * gb, LANES), F32)],
        compiler_params=_cparams("parallel", "parallel"),
    )(x, t_mat, w_b, w_c, a)


def _finish(h, y, mod, g_ref, o_ref):
    o_ref[0] = h + mod[2:3, :] * _rms(y, g_ref[1:2, :])


def _attn_out_kernel(h_ref, mod_ref, g_ref, o_in_ref, w_ref, o_ref):
    _finish(h_ref[0], _dot(o_in_ref[0], w_ref[...]), mod_ref[0], g_ref, o_ref)


def _ret_out_kernel(h_ref, mod_ref, g_ref, o_in_ref, gate_ref, w_ref, o_ref):
    o = o_in_ref[0, 0].astype(F32) + o_in_ref[1, 0].astype(F32)
    dv = o.shape[1] // RET_HEADS
    parts = []
    for hd in range(RET_HEADS):
        oh = o[:, hd * dv:(hd + 1) * dv]
        parts.append(oh * lax.rsqrt(jnp.mean(oh * oh, axis=-1, keepdims=True) + NORM_EPS))
    o = jnp.concatenate(parts, axis=1)
    y = _dot((_silu(gate_ref[0].astype(F32)) * o).astype(BF16), w_ref[...])
    _finish(h_ref[0], y, mod_ref[0], g_ref, o_ref)


def _s5_out_kernel(h_ref, mod_ref, g_ref, y_ref, dskip_ref, w_ref, o_ref, y_scr):
    h = h_ref[0]
    mod = mod_ref[0]
    d = h.shape[1]
    _from_group_major(y_ref, y_scr)
    u = _rms(h, g_ref[0:1, :]) * (1.0 + mod[1:2, :]) + mod[0:1, :]
    y_ssm = jnp.concatenate([y_scr[lt] for lt in range(y_scr.shape[0])], axis=1)
    z = y_ssm + dskip_ref[...] * u
    gl = 0.5 * z * (1.0 + jnp.tanh(math.sqrt(2.0 / math.pi) * (z + 0.044715 * (z * z * z))))
    gl = gl.astype(BF16)
    y = _dot(gl, w_ref[:, :d]) * jax.nn.sigmoid(_dot(gl, w_ref[:, d:]))
    _finish(h, y, mod, g_ref, o_ref)


def _mixer_out(kernel, hs, mod_l, gains, nctx_tiles, extra, extra_specs, scratch=(), skip_ctx=False):
    bsz, L, d = hs.shape
    off = nctx_tiles if skip_ctx else 0
    n_tiles = L // TM - off
    return pl.pallas_call(
        kernel,
        grid=(bsz, n_tiles),
        in_specs=[
            pl.BlockSpec((1, TM, d), lambda b, i: (b, i + off, 0)),
            _mod_spec(d, bsz, nctx_tiles - off),
            pl.BlockSpec(gains.shape, lambda b, i: (0, 0)),
        ] + extra_specs,
        out_specs=pl.BlockSpec((1, TM, d), lambda b, i: (b, i, 0)),
        out_shape=jax.ShapeDtypeStruct((bsz, n_tiles * TM, d), F32),
        scratch_shapes=list(scratch),
        compiler_params=_cparams("parallel", "parallel"),
    )(hs, mod_l, gains, *extra)


def _full_spec(a):
    nd = a.ndim
    return pl.BlockSpec(a.shape, lambda b, i: (0,) * nd)


def _ffn_kernel(h_ref, mod_ref, g_ref, wg_ref, wu_ref, wd_ref, o_ref):
    h = h_ref[0]
    mod = mod_ref[0]
    v = (_rms(h, g_ref[2:3, :]) * (1.0 + mod[4:5, :]) + mod[3:4, :]).astype(BF16)
    act = (_silu(_dot(v, wg_ref[...])) * _dot(v, wu_ref[...])).astype(BF16)
    y = _dot(act, wd_ref[...])
    o_ref[0] = h + mod[5:6, :] * _rms(y, g_ref[3:4, :])


def _ffn(hs, mod_l, gains, wg, wu, wd, nctx_tiles):
    bsz, L, d = hs.shape
    return pl.pallas_call(
        _ffn_kernel,
        grid=(bsz, L // TM),
        in_specs=[
            pl.BlockSpec((1, TM, d), lambda b, i: (b, i, 0)),
            _mod_spec(d, bsz, nctx_tiles),
            pl.BlockSpec(gains.shape, lambda b, i: (0, 0)),
            _full_spec(wg), _full_spec(wu), _full_spec(wd),
        ],
        out_specs=pl.BlockSpec((1, TM, d), lambda b, i: (b, i, 0)),
        out_shape=jax.ShapeDtypeStruct((bsz, L, d), F32),
        compiler_params=_cparams("parallel", "parallel"),
    )(hs, mod_l, gains, wg, wu, wd)


def kernel(x, c, ctx, c_ctx, mod_w, mod_b, norm_g, attn_w_in, attn_w_out, attn_lambda, attn_subln,
           ret_w_in, ret_w_out, ret_decay_logit, ssm_lambda_re, ssm_lambda_im, ssm_log_dt,
           ssm_b_re, ssm_b_im, ssm_c_re, ssm_c_im, ssm_d, ssm_w_glu, ffn_w_gate, ffn_w_up, ffn_w_down):
    bsz, seq, d = x.shape
    nctx = ctx.shape[1]
    depth = mod_w.shape[0]
    assert nctx % TM == 0 and seq % TM == 0 and seq % GRID_W == 0
    nct = nctx // TM
    L = nctx + seq

    mods = _modulation(c, c_ctx, mod_w, mod_b)
    hs = jnp.concatenate([ctx, x], axis=1)
    da_dim = d // (2 * DA_HEADS)
    dk = d // RET_HEADS
    tab_a = _rope_table(seq, nctx, da_dim)
    tab_r = _rope_table(seq, nctx, dk)

    def tok_spec(width, col=0):
        return pl.BlockSpec((1, TM, width), lambda b, i: (b, i, col))

    for l in range(depth):
        kind, j = l % 3, l // 3
        last = l == depth - 1
        mod_l, gains = mods[l], norm_g[l]
        if kind == 0:
            lam_init = 0.8 - 0.6 * math.exp(-0.3 * l)
            q_t, k, v_t = _attn_proj(hs, mod_l, gains, attn_w_in[j].astype(BF16), tab_a,
                                     da_dim ** -0.5 * math.log2(math.e), da_dim // 4, nct)
            o = _attention(q_t, k, v_t, attn_lambda[j], attn_subln[j], nctx, lam_init, not last)
            w_out = attn_w_out[j].astype(BF16)
            hs = _mixer_out(_attn_out_kernel, hs, mod_l, gains, nct, (o, w_out),
                            [tok_spec(d), _full_spec(w_out)], skip_ctx=last)
            nct = 0 if last else nct
        elif kind == 1:
            chunks = ((0, d, 1.0, True), (d, d, dk ** -0.5, True)) + tuple(
                (c0, d, 1.0, False) for c0 in range(2 * d, 6 * d, d))
            p = _proj_in(hs, mod_l, gains, ret_w_in[j].astype(BF16), tab_r, chunks, dk // 4, nct)
            log_g = jax.nn.log_sigmoid(ret_decay_logit[j].astype(F32))
            o = _retention(p, log_g, nctx)
            w_out = ret_w_out[j].astype(BF16)
            hs = _mixer_out(_ret_out_kernel, hs, mod_l, gains, nct, (o, p, w_out),
                            [pl.BlockSpec((2, 1, TM, 2 * d), lambda b, i: (0, b, i, 0)),
                             tok_spec(2 * d, 2), _full_spec(w_out)])
        else:
            u = _normmod(hs, mod_l, gains, nct)
            weights = _s5_weights(ssm_lambda_re[j], ssm_lambda_im[j], ssm_log_dt[j], ssm_b_re[j],
                                  ssm_b_im[j], ssm_c_re[j], ssm_c_im[j])
            y = _s5_core(u, weights, nctx)
            w_glu = ssm_w_glu[j].astype(BF16)
            dskip = ssm_d[j].astype(F32).reshape(1, d)
            y_spec = pl.BlockSpec((1,) + y.shape[1:2] + (TM // SSM_CHUNK,) + y.shape[3:],
                                  lambda b, i: (b, 0, i, 0))
            hs = _mixer_out(_s5_out_kernel, hs, mod_l, gains, nct, (y, dskip, w_glu),
                            [y_spec, _full_spec(dskip), _full_spec(w_glu)],
                            scratch=[pltpu.VMEM((d // LANES, TM, LANES), F32)])
        hs = _ffn(hs, mod_l, gains, ffn_w_gate[l].astype(BF16), ffn_w_up[l].astype(BF16),
                  ffn_w_down[l].astype(BF16), nct)
    return hs if hs.shape[1] == seq else hs[:, nctx:, :]
```

```python
import functools
import math

import jax
import jax.numpy as jnp
from jax import lax
from jax.experimental import pallas as pl
from jax.experimental.pallas import tpu as pltpu

F32 = jnp.float32
BF16 = jnp.bfloat16

GRID_W = 64
DA_HEADS = 8
RET_HEADS = 4
RET_CHUNK = 256
SSM_GROUP = 16
SSM_STATE = 64
SSM_CHUNK = 16
SSM_GROUP_BLOCK = 8
ROPE_THETA = 10000.0
NORM_EPS = 1e-6
LANES = 128
MXU_COLS = 256
TM = 256
ATTN_TQ = 1024
VT_PAD = 16
VMEM_LIMIT = 56 * 1024 * 1024


def _cparams(*sem):
    return pltpu.CompilerParams(dimension_semantics=sem, vmem_limit_bytes=VMEM_LIMIT)


def _rms(x, gain):
    return x * lax.rsqrt(jnp.mean(x * x, axis=-1, keepdims=True) + NORM_EPS) * gain


def _silu(x):
    return x * jax.nn.sigmoid(x)


def _dot(a, b):
    return jnp.dot(a, b, preferred_element_type=F32)


def _dot_nt(a, b):
    return lax.dot_general(a, b, (((1,), (1,)), ((), ())), preferred_element_type=F32)


def _dot_tn(a, b):
    return lax.dot_general(a, b, (((0,), (0,)), ((), ())), preferred_element_type=F32)


def _mod_kernel(s_ref, w_ref, b_ref, o_ref):
    s = _silu(s_ref[...])
    o_ref[0] = jnp.dot(s, w_ref[0], preferred_element_type=F32,
                       precision=lax.Precision.HIGHEST) + b_ref[0]


def _modulation(c, c_ctx, mod_w, mod_b):
    depth, d, n = mod_w.shape
    bsz = c.shape[0]
    rows = 8
    s = jnp.zeros((rows, d), F32).at[:bsz].set(c).at[bsz].set(c_ctx)
    tn = 1536
    out = pl.pallas_call(
        _mod_kernel,
        grid=(depth, n // tn),
        in_specs=[
            pl.BlockSpec((rows, d), lambda l, j: (0, 0)),
            pl.BlockSpec((1, d, tn), lambda l, j: (l, 0, j)),
            pl.BlockSpec((1, 1, tn), lambda l, j: (l, 0, j)),
        ],
        out_specs=pl.BlockSpec((1, rows, tn), lambda l, j: (l, 0, j)),
        out_shape=jax.ShapeDtypeStruct((depth, rows, n), F32),
        compiler_params=_cparams("parallel", "parallel"),
    )(s, mod_w, mod_b.reshape(depth, 1, n))
    return out.reshape(depth, rows, 6, d)


def _mod_spec(d, bsz, nctx_tiles):
    return pl.BlockSpec((1, 6, d), lambda b, i: (jnp.where(i < nctx_tiles, bsz, b), 0, 0))


def _rope_trig(fn, seq, head_dim):
    quarter = head_dim // 4
    freqs = ROPE_THETA ** (-jnp.arange(quarter, dtype=F32) / quarter)
    rows = seq // GRID_W
    by_row = jnp.repeat(fn(jnp.arange(rows, dtype=F32)[:, None] * freqs), GRID_W, axis=0)
    by_col = jnp.tile(fn(jnp.arange(GRID_W, dtype=F32)[:, None] * freqs), (rows, 1))
    return jnp.concatenate([by_row, by_row, by_col, by_col], axis=-1)


def _rope_table(seq, ctx, head_dim):
    cos = jnp.concatenate([jnp.ones((ctx, head_dim), F32), _rope_trig(jnp.cos, seq, head_dim)], axis=0)
    sin = jnp.concatenate([jnp.zeros((ctx, head_dim), F32), _rope_trig(jnp.sin, seq, head_dim)], axis=0)
    q = head_dim // 4
    first = (jnp.arange(head_dim) % (2 * q)) < q
    s_minus = jnp.where(first, -sin, 0.0)
    s_plus = jnp.where(first, 0.0, sin)
    rep = max(1, LANES // head_dim)
    return jnp.concatenate([jnp.tile(t, (1, rep)) for t in (cos, s_minus, s_plus)], axis=1)


def _apply_rope(y, tab, quarter):
    w = tab.shape[1] // 3
    cos, s_minus, s_plus = tab[:, :w], tab[:, w:2 * w], tab[:, 2 * w:]
    outs = []
    for k in range(y.shape[1] // LANES):
        yk = y[:, k * LANES:(k + 1) * LANES]
        t0 = (k * LANES) % w
        ck, mk, pk = (t[:, t0:t0 + LANES] for t in (cos, s_minus, s_plus))
        if 2 * quarter == LANES:
            outs.append(yk * ck + pltpu.roll(yk, quarter, 1) * (mk + pk))
        else:
            outs.append(yk * ck + pltpu.roll(yk, LANES - quarter, 1) * mk
                        + pltpu.roll(yk, quarter, 1) * pk)
    return jnp.concatenate(outs, axis=1)


def _proj_in_kernel(x_ref, mod_ref, g_ref, w_ref, tab_ref, o_ref, *, chunks, quarter):
    mod = mod_ref[0]
    u = (_rms(x_ref[0], g_ref[0:1, :]) * (1.0 + mod[1:2, :]) + mod[0:1, :]).astype(BF16)
    tab = tab_ref[...]
    for c0, width, mult, rope in chunks:
        y = _dot(u, w_ref[:, c0:c0 + width])
        if mult != 1.0:
            y = y * mult
        if rope:
            y = _apply_rope(y, tab, quarter)
        o_ref[0, :, c0:c0 + width] = y.astype(BF16)


def _proj_in(hs, mod_l, gains, w, tab, chunks, quarter, nctx_tiles):
    bsz, L, d = hs.shape
    n = w.shape[1]
    return pl.pallas_call(
        functools.partial(_proj_in_kernel, chunks=chunks, quarter=quarter),
        grid=(bsz, L // TM),
        in_specs=[
            pl.BlockSpec((1, TM, d), lambda b, i: (b, i, 0)),
            _mod_spec(d, bsz, nctx_tiles),
            pl.BlockSpec(gains.shape, lambda b, i: (0, 0)),
            pl.BlockSpec(w.shape, lambda b, i: (0, 0)),
            pl.BlockSpec((TM, tab.shape[1]), lambda b, i: (i, 0)),
        ],
        out_specs=pl.BlockSpec((1, TM, n), lambda b, i: (b, i, 0)),
        out_shape=jax.ShapeDtypeStruct((bsz, L, n), BF16),
        compiler_params=_cparams("parallel", "parallel"),
    )(hs, mod_l, gains, w, tab)


def _attn_proj_kernel(x_ref, mod_ref, g_ref, w_ref, tab_ref, qt_ref, k_ref, vt_ref, *, scale, quarter):
    mod = mod_ref[0]
    u = (_rms(x_ref[0], g_ref[0:1, :]) * (1.0 + mod[1:2, :]) + mod[0:1, :]).astype(BF16)
    d = x_ref.shape[2]
    tab = tab_ref[...]
    heads = vt_ref.shape[1]
    qt_ref[0] = _apply_rope(_dot(u, w_ref[:, :d]) * scale, tab, quarter).T.astype(BF16)
    k_ref[0] = _apply_rope(_dot(u, w_ref[:, d:2 * d]), tab, quarter).astype(BF16)
    vt = _dot(u, w_ref[:, 2 * d:]).T.astype(BF16)
    vt_ref[0, :, 0, 0:LANES, :] = vt.reshape(heads, LANES, vt.shape[1])
    vt_ref[0, :, 0, LANES:, :] = jnp.ones((heads, VT_PAD, vt.shape[1]), BF16)


def _attn_proj(hs, mod_l, gains, w, tab, scale, quarter, nctx_tiles):
    bsz, L, d = hs.shape
    nlat = L // TM - nctx_tiles
    return pl.pallas_call(
        functools.partial(_attn_proj_kernel, scale=scale, quarter=quarter),
        grid=(bsz, L // TM),
        in_specs=[
            pl.BlockSpec((1, TM, d), lambda b, i: (b, i, 0)),
            _mod_spec(d, bsz, nctx_tiles),
            pl.BlockSpec(gains.shape, lambda b, i: (0, 0)),
            pl.BlockSpec(w.shape, lambda b, i: (0, 0)),
            pl.BlockSpec((TM, tab.shape[1]), lambda b, i: (i, 0)),
        ],
        out_specs=[
            pl.BlockSpec((1, d, TM), lambda b, i: (b, 0, jnp.where(i < nctx_tiles, i + nlat, i - nctx_tiles))),
            pl.BlockSpec((1, TM, d), lambda b, i: (b, i, 0)),
            pl.BlockSpec((1, DA_HEADS, 1, LANES + VT_PAD, TM), lambda b, i: (b, 0, i, 0, 0)),
        ],
        out_shape=[
            jax.ShapeDtypeStruct((bsz, d, L), BF16),
            jax.ShapeDtypeStruct((bsz, L, d), BF16),
            jax.ShapeDtypeStruct((bsz, DA_HEADS, L // TM, LANES + VT_PAD, TM), BF16),
        ],
        compiler_params=_cparams("parallel", "parallel"),
    )(hs, mod_l, gains, w, tab)


def _attn_kernel(lam_ref, sub_ref, qt_ref, k_ref, vt_ref, o_ref,
                 s_c, s_a, s_b, mb_scr, m_scr, acc_scr, *, nb, lam_init):
    ctx = s_c.shape[1]
    tk = s_a.shape[1]
    kv = vt_ref.shape[4]
    qt = qt_ref[0]
    row = lax.broadcasted_iota(jnp.int32, qt.shape, 0)
    zero = jnp.zeros_like(qt)
    qts = (jnp.where(row < LANES // 2, qt, zero), jnp.where(row >= LANES // 2, qt, zero))

    slot = {id(s_c): 0, id(s_a): 1, id(s_b): 2}

    qcols = [slice(j, j + MXU_COLS) for j in range(0, qt.shape[1], MXU_COLS)]

    def scores(kb, s_out):
        for part in range(2):
            for qc in qcols:
                s = _dot(kb, qts[part][:, qc])
                s_out[part, :, qc] = s
                mb_scr[slot[id(s_out)], part, :, qc] = jnp.max(s, axis=0, keepdims=True)

    def stage(s_in, vt0, nxt):
        if nxt is not None:
            scores(*nxt)
        for part in range(2):
            for qc in qcols:
                m = m_scr[part, :, qc]
                m_new = jnp.maximum(m, mb_scr[slot[id(s_in)], part, :, qc])
                alpha = jnp.exp2(m - m_new)
                pv = None
                for n in range(s_in.shape[1] // kv):
                    p = jnp.exp2(s_in[part, n * kv:(n + 1) * kv, qc] - m_new).astype(BF16)
                    pv_n = _dot(vt_ref[0, 0, vt0 + n], p)
                    pv = pv_n if pv is None else pv + pv_n
                m_scr[part, :, qc] = m_new
                acc_scr[part, :, qc] = alpha * acc_scr[part, :, qc] + pv

    m_scr[...] = jnp.full(m_scr.shape, -1e30, F32)
    acc_scr[...] = jnp.zeros(acc_scr.shape, F32)

    bufs = (s_b, s_a)

    def keys(t):
        return k_ref[0, pl.ds(pl.multiple_of(ctx + (t - 1) * tk, LANES), tk), :]

    def vt_index(t):
        return ctx // kv + (t - 1) * (tk // kv)

    scores(k_ref[0, 0:ctx, :], s_c)
    stage(s_c, 0, (keys(1), s_a) if nb >= 1 else None)
    n_pairs = max(0, (nb - 1) // 2 if nb % 2 else (nb - 2) // 2)

    def pair(u, carry):
        t = 2 * u + 1
        stage(s_a, vt_index(t), (keys(t + 1), s_b))
        stage(s_b, vt_index(t + 1), (keys(t + 2), s_a))
        return carry

    if n_pairs > 0:
        lax.fori_loop(0, n_pairs, pair, 0)
    for t in range(2 * n_pairs + 1, nb + 1):
        stage(bufs[t % 2], vt_index(t), (keys(t + 1), bufs[(t + 1) % 2]) if t < nb else None)

    lv = lam_ref[...]
    lam = (jnp.exp(jnp.sum(lv[0:1] * lv[1:2], axis=-1, keepdims=True))
           - jnp.exp(jnp.sum(lv[2:3] * lv[3:4], axis=-1, keepdims=True)) + lam_init)
    a1, a2 = acc_scr[0], acc_scr[1]
    o = (a1[:LANES] / a1[LANES:LANES + 1] - lam * (a2[:LANES] / a2[LANES:LANES + 1])).T
    o = _rms(o, sub_ref[...]) * (1.0 - lam_init)
    o_ref[0] = o.astype(BF16)


def _attention_call(q_t, k, v_t, lam_vec, subln, ctx, tk, nb, lam_init, tq, q_off, n_out):
    bsz, d, _ = q_t.shape
    kv = v_t.shape[4]
    k_rows = ctx + nb * tk
    return pl.pallas_call(
        functools.partial(_attn_kernel, nb=nb, lam_init=lam_init),
        grid=(bsz, DA_HEADS, n_out // tq),
        in_specs=[
            pl.BlockSpec(lam_vec.shape, lambda b, h, i: (0, 0)),
            pl.BlockSpec((1, LANES), lambda b, h, i: (0, 0)),
            pl.BlockSpec((1, LANES, tq), lambda b, h, i: (b, h, i + q_off // tq)),
            pl.BlockSpec((1, k_rows, LANES), lambda b, h, i: (b, 0, h)),
            pl.BlockSpec((1, 1, k_rows // kv) + v_t.shape[3:], lambda b, h, i: (b, h, 0, 0, 0)),
        ],
        out_specs=pl.BlockSpec((1, tq, LANES), lambda b, h, i: (b, i, h)),
        out_shape=jax.ShapeDtypeStruct((bsz, n_out, d), BF16),
        scratch_shapes=[pltpu.VMEM((2, ctx, tq), F32), pltpu.VMEM((2, tk, tq), F32),
                        pltpu.VMEM((2, tk, tq), F32), pltpu.VMEM((3, 2, 1, tq), F32),
                        pltpu.VMEM((2, 1, tq), F32), pltpu.VMEM((2, LANES + VT_PAD, tq), F32)],
        compiler_params=_cparams("parallel", "parallel", "arbitrary"),
    )(lam_vec, subln.reshape(1, LANES), q_t, k, v_t)


def _attention(q_t, k, v_t, lam_vec, subln, ctx, lam_init, with_ctx):
    L = k.shape[1]
    seq = L - ctx
    tk = min(1024, seq)
    nb = seq // tk
    tq = min(ATTN_TQ, seq)
    o_lat = _attention_call(q_t, k, v_t, lam_vec, subln, ctx, tk, nb, lam_init, tq, 0, seq)
    if not with_ctx:
        return o_lat
    o_ctx = _attention_call(q_t, k, v_t, lam_vec, subln, ctx, tk, 0, lam_init, TM, seq, ctx)
    return jnp.concatenate([o_ctx, o_lat], axis=1)


def _ret_kernel(lg_ref, q_ref, k_ref, v_ref, o_ref, r_scr):
    d = pl.program_id(1)
    c = pl.program_id(2)
    C = q_ref.shape[1]
    heads, dk, dv = r_scr.shape

    @pl.when(c == 0)
    def _():
        r_scr[...] = jnp.zeros_like(r_scr)

    fwd = d == 0
    ii = lax.broadcasted_iota(jnp.int32, (C, C), 0)
    jj = lax.broadcasted_iota(jnp.int32, (C, C), 1)
    dd = jnp.where(fwd, ii - jj, jj - ii)
    causal = dd >= 0
    lag = jnp.maximum(dd, 0).astype(F32)
    idx = lax.broadcasted_iota(jnp.int32, (C, 1), 0)
    q_lag = jnp.where(fwd, idx + 1, C - idx).astype(F32)
    k_lag = jnp.where(fwd, C - 1 - idx, idx).astype(F32)

    for h in range(heads):
        lg = lg_ref[0, h:h + 1, 0:1]
        q = q_ref[0, :, h * dk:(h + 1) * dk]
        k = k_ref[0, :, h * dk:(h + 1) * dk]
        v = v_ref[0, :, h * dv:(h + 1) * dv]
        r = r_scr[h]
        s = _dot_nt(q, k) * jnp.where(causal, jnp.exp(lg * lag), 0.0)
        inner = _dot(s.astype(BF16), v)
        cross = _dot((q.astype(F32) * jnp.exp(lg * q_lag)).astype(BF16), r.astype(BF16))
        o_ref[0, 0, :, h * dv:(h + 1) * dv] = (inner + cross).astype(BF16)
        r_scr[h] = jnp.exp(lg * float(C)) * r + _dot_tn((k.astype(F32) * jnp.exp(lg * k_lag)).astype(BF16), v)


def _retention(p, log_g, ctx):
    bsz, L, n = p.shape
    d = n // 6
    C = RET_CHUNK
    nc, ncc = L // C, ctx // C

    def chunk(dr, c):
        back = jnp.where(c < ncc, ncc - 1 - c, nc + ncc - 1 - c)
        return jnp.where(dr == 0, c, back)

    lg = jnp.broadcast_to(log_g.reshape(2, RET_HEADS, 1), (2, RET_HEADS, LANES))
    return pl.pallas_call(
        _ret_kernel,
        grid=(bsz, 2, nc),
        in_specs=[
            pl.BlockSpec((1, RET_HEADS, LANES), lambda b, dr, c: (dr, 0, 0)),
            pl.BlockSpec((1, C, d), lambda b, dr, c: (b, chunk(dr, c), 0)),
            pl.BlockSpec((1, C, d), lambda b, dr, c: (b, chunk(dr, c), 1)),
            pl.BlockSpec((1, C, 2 * d), lambda b, dr, c: (b, chunk(dr, c), 1)),
        ],
        out_specs=pl.BlockSpec((1, 1, C, 2 * d), lambda b, dr, c: (dr, b, chunk(dr, c), 0)),
        out_shape=jax.ShapeDtypeStruct((2, bsz, L, 2 * d), BF16),
        scratch_shapes=[pltpu.VMEM((RET_HEADS, d // RET_HEADS, 2 * d // RET_HEADS), F32)],
        compiler_params=_cparams("parallel", "arbitrary", "arbitrary"),
    )(lg, p, p, p)


def _piece_masks(rows):
    lane = lax.broadcasted_iota(jnp.int32, (rows, LANES), 1)
    return [(lane >= SSM_GROUP * p) & (lane < SSM_GROUP * (p + 1)) for p in range(LANES // SSM_GROUP)]


def _to_group_major(u_scr, o_ref):
    n, hg = SSM_CHUNK, SSM_GROUP
    per = LANES // hg
    ntile, tm, _ = u_scr.shape
    rows = tm // n
    masks = _piece_masks(rows)
    for lt in range(ntile):
        for th in range(n * hg // LANES):
            src = [u_scr[lt, pl.ds(th * per + tp, rows, stride=n), :] for tp in range(per)]
            for j in range(per):
                acc = None
                for tp in range(per):
                    shift = (hg * (tp - j)) % LANES
                    piece = src[tp] if shift == 0 else pltpu.roll(src[tp], shift, 1)
                    acc = piece if acc is None else jnp.where(masks[tp], piece, acc)
                o_ref[0, lt * per + j, :, th * LANES:(th + 1) * LANES] = acc.astype(o_ref.dtype)


def _from_group_major(y_ref, o_scr):
    n, hg = SSM_CHUNK, SSM_GROUP
    per = LANES // hg
    ntile, tm, _ = o_scr.shape
    rows = tm // n
    masks = _piece_masks(rows)
    for t in range(n):
        th, tp = divmod(t, per)
        for lt in range(ntile):
            acc = None
            for j in range(per):
                piece = y_ref[0, lt * per + j, :, th * LANES:(th + 1) * LANES]
                shift = (hg * (j - tp)) % LANES
                piece = piece if shift == 0 else pltpu.roll(piece, shift, 1)
                acc = piece if acc is None else jnp.where(masks[j], piece, acc)
            o_scr[lt, pl.ds(t, rows, stride=n), :] = acc


def _normmod_kernel(x_ref, mod_ref, g_ref, o_ref, u_scr):
    mod = mod_ref[0]
    u = _rms(x_ref[0], g_ref[0:1, :]) * (1.0 + mod[1:2, :]) + mod[0:1, :]
    for lt in range(u_scr.shape[0]):
        u_scr[lt] = u[:, lt * LANES:(lt + 1) * LANES]
    _to_group_major(u_scr, o_ref)


def _normmod(hs, mod_l, gains, nctx_tiles):
    bsz, L, d = hs.shape
    G, rows, width = d // SSM_GROUP, TM // SSM_CHUNK, SSM_CHUNK * SSM_GROUP
    return pl.pallas_call(
        _normmod_kernel,
        grid=(bsz, L // TM),
        in_specs=[
            pl.BlockSpec((1, TM, d), lambda b, i: (b, i, 0)),
            _mod_spec(d, bsz, nctx_tiles),
            pl.BlockSpec(gains.shape, lambda b, i: (0, 0)),
        ],
        out_specs=pl.BlockSpec((1, G, rows, width), lambda b, i: (b, 0, i, 0)),
        out_shape=jax.ShapeDtypeStruct((bsz, G, L // SSM_CHUNK, width), BF16),
        scratch_shapes=[pltpu.VMEM((d // LANES, TM, LANES), F32)],
        compiler_params=_cparams("parallel", "parallel"),
    )(hs, mod_l, gains)


def _s5_kernel(x_ref, t_ref, wb_ref, wc_ref, a_ref, y_ref, st_scr, *, nctx):
    gb = x_ref.shape[1]
    nch = x_ref.shape[2]
    for gi in range(gb):
        x = x_ref[0, gi]
        for dr in range(2):
            s = _dot(x, wb_ref[gi, dr])
            st_scr[2 * dr, pl.ds(gi, nch, stride=gb), :] = s[:, :LANES]
            st_scr[2 * dr + 1, pl.ds(gi, nch, stride=gb), :] = s[:, LANES:]

    a = a_ref[...]

    def step(dr, row, h_re, h_im):
        s_re = st_scr[2 * dr, pl.ds(row, gb), :]
        s_im = st_scr[2 * dr + 1, pl.ds(row, gb), :]
        st_scr[2 * dr, pl.ds(row, gb), :] = h_re
        st_scr[2 * dr + 1, pl.ds(row, gb), :] = h_im
        a_re, a_im = a[dr, 0], a[dr, 1]
        return a_re * h_re - a_im * h_im + s_re, a_re * h_im + a_im * h_re + s_im

    def body(t, carry):
        f_re, f_im, b_re, b_im = carry
        cb = jnp.where(t < nctx, nctx - 1 - t, nch + nctx - 1 - t)
        f_re, f_im = step(0, pl.multiple_of(t * gb, gb), f_re, f_im)
        b_re, b_im = step(1, pl.multiple_of(cb * gb, gb), b_re, b_im)
        return f_re, f_im, b_re, b_im

    z = jnp.zeros((gb, LANES), F32)
    lax.fori_loop(0, nch, body, (z, z, z, z))

    for gi in range(gb):
        y = _dot(x_ref[0, gi], t_ref[gi])
        for dr in range(2):
            h = jnp.concatenate([st_scr[2 * dr, pl.ds(gi, nch, stride=gb), :],
                                 st_scr[2 * dr + 1, pl.ds(gi, nch, stride=gb), :]], axis=1)
            y = y + _dot(h.astype(BF16), wc_ref[gi, dr])
        y_ref[0, gi] = y


def _s5_weights(lam_re, lam_im, log_dt, b_re, b_im, c_re, c_im):
    n = SSM_CHUNK
    lam = lax.complex(lam_re.astype(F32), lam_im.astype(F32))
    ldt = lam * jnp.exp(log_dt.astype(F32))[..., None]
    lam_bar = jnp.exp(ldt)
    b_bar = ((lam_bar - 1.0) / lam)[..., None] * lax.complex(b_re.astype(F32), b_im.astype(F32))
    c_mat = lax.complex(c_re.astype(F32), c_im.astype(F32))
    steps = jnp.arange(n + 1, dtype=F32)
    pw = jnp.exp(ldt[None] * steps[:, None, None, None])
    G, P, Hg = b_bar.shape[1:]

    kern = jnp.real(jnp.einsum('kdgp,dgjp,dgpi->dkgij', pw[:n], c_mat, b_bar))
    s_idx = jnp.arange(n)[:, None]
    t_idx = jnp.arange(n)[None, :]
    lag = t_idx - s_idx
    kf = jnp.where((lag >= 0)[..., None, None, None], kern[0][jnp.clip(lag, 0, n - 1)], 0.0)
    kb = jnp.where((lag <= 0)[..., None, None, None], kern[1][jnp.clip(-lag, 0, n - 1)], 0.0)
    t_mat = (kf + kb).transpose(2, 0, 3, 1, 4).reshape(G, n * Hg, n * Hg)

    zpad = jnp.zeros((G, n * Hg, LANES - P), F32)
    zrow = jnp.zeros((G, LANES - P, n * Hg), F32)
    wbs, wcs = [], []
    for dr in range(2):
        e_in = pw[:n][::-1, dr] if dr == 0 else pw[:n, dr]
        wb = (e_in[:, :, :, None] * b_bar[dr][None]).transpose(1, 0, 3, 2).reshape(G, n * Hg, P)
        wbs.append(jnp.concatenate([jnp.real(wb), zpad, jnp.imag(wb), zpad], axis=2))
        e_out = pw[1:, dr] if dr == 0 else pw[1:][::-1, dr]
        wc = (c_mat[dr][None] * e_out[:, :, None, :]).transpose(1, 3, 0, 2).reshape(G, P, n * Hg)
        wcs.append(jnp.concatenate([jnp.real(wc), zrow, -jnp.imag(wc), zrow], axis=1))
    w_b = jnp.stack(wbs, axis=1)
    w_c = jnp.stack(wcs, axis=1)
    a_n = pw[n]
    pad = jnp.zeros((2, G, LANES - P), F32)
    a = jnp.stack([jnp.concatenate([jnp.real(a_n), pad], -1),
                   jnp.concatenate([jnp.imag(a_n), pad], -1)], axis=1)
    return t_mat.astype(BF16), w_b.astype(BF16), w_c.astype(BF16), a


def _s5_core(x, weights, ctx):
    bsz, G, nch, width = x.shape
    n, gb = SSM_CHUNK, SSM_GROUP_BLOCK
    t_mat, w_b, w_c, a = weights
    return pl.pallas_call(
        functools.partial(_s5_kernel, nctx=ctx // n),
        grid=(bsz, G // gb),
        in_specs=[
            pl.BlockSpec((1, gb, nch, width), lambda b, g: (b, g, 0, 0)),
            pl.BlockSpec((gb, width, width), lambda b, g: (g, 0, 0)),
            pl.BlockSpec((gb, 2, width, 2 * LANES), lambda b, g: (g, 0, 0, 0)),
            pl.BlockSpec((gb, 2, 2 * LANES, width), lambda b, g: (g, 0, 0, 0)),
            pl.BlockSpec((2, 2, gb, LANES), lambda b, g: (0, 0, g, 0)),
        ],
        out_specs=pl.BlockSpec((1, gb, nch, width), lambda b, g: (b, g, 0, 0)),
        out_shape=jax.ShapeDtypeStruct((bsz, G, nch, width), F32),
        scratch_shapes=[pltpu.VMEM((4, nch * gb, LANES), F32)],
        compiler_params=_cparams("parallel", "parallel"),
    )(x, t_mat, w_b, w_c, a)


def _finish(h, y, mod, g_ref, o_ref):
    o_ref[0] = h + mod[2:3, :] * _rms(y, g_ref[1:2, :])


def _attn_out_kernel(h_ref, mod_ref, g_ref, o_in_ref, w_ref, o_ref):
    _finish(h_ref[0], _dot(o_in_ref[0], w_ref[...]), mod_ref[0], g_ref, o_ref)


def _ret_out_kernel(h_ref, mod_ref, g_ref, o_in_ref, gate_ref, w_ref, o_ref):
    o = o_in_ref[0, 0].astype(F32) + o_in_ref[1, 0].astype(F32)
    dv = o.shape[1] // RET_HEADS
    parts = []
    for hd in range(RET_HEADS):
        oh = o[:, hd * dv:(hd + 1) * dv]
        parts.append(oh * lax.rsqrt(jnp.mean(oh * oh, axis=-1, keepdims=True) + NORM_EPS))
    o = jnp.concatenate(parts, axis=1)
    y = _dot((_silu(gate_ref[0].astype(F32)) * o).astype(BF16), w_ref[...])
    _finish(h_ref[0], y, mod_ref[0], g_ref, o_ref)


def _s5_out_kernel(h_ref, mod_ref, g_ref, y_ref, dskip_ref, w_ref, o_ref, y_scr):
    h = h_ref[0]
    mod = mod_ref[0]
    d = h.shape[1]
    _from_group_major(y_ref, y_scr)
    u = _rms(h, g_ref[0:1, :]) * (1.0 + mod[1:2, :]) + mod[0:1, :]
    y_ssm = jnp.concatenate([y_scr[lt] for lt in range(y_scr.shape[0])], axis=1)
    z = y_ssm + dskip_ref[...] * u
    gl = 0.5 * z * (1.0 + jnp.tanh(math.sqrt(2.0 / math.pi) * (z + 0.044715 * (z * z * z))))
    gl = gl.astype(BF16)
    y = _dot(gl, w_ref[:, :d]) * jax.nn.sigmoid(_dot(gl, w_ref[:, d:]))
    _finish(h, y, mod, g_ref, o_ref)


def _mixer_out(kernel, hs, mod_l, gains, nctx_tiles, extra, extra_specs, scratch=(), skip_ctx=False):
    bsz, L, d = hs.shape
    off = nctx_tiles if skip_ctx else 0
    n_tiles = L // TM - off
    return pl.pallas_call(
        kernel,
        grid=(bsz, n_tiles),
        in_specs=[
            pl.BlockSpec((1, TM, d), lambda b, i: (b, i + off, 0)),
            _mod_spec(d, bsz, nctx_tiles - off),
            pl.BlockSpec(gains.shape, lambda b, i: (0, 0)),
        ] + extra_specs,
        out_specs=pl.BlockSpec((1, TM, d), lambda b, i: (b, i, 0)),
        out_shape=jax.ShapeDtypeStruct((bsz, n_tiles * TM, d), F32),
        scratch_shapes=list(scratch),
        compiler_params=_cparams("parallel", "parallel"),
    )(hs, mod_l, gains, *extra)


def _full_spec(a):
    nd = a.ndim
    return pl.BlockSpec(a.shape, lambda b, i: (0,) * nd)


def _ffn_kernel(h_ref, mod_ref, g_ref, wg_ref, wu_ref, wd_ref, o_ref):
    h = h_ref[0]
    mod = mod_ref[0]
    v = (_rms(h, g_ref[2:3, :]) * (1.0 + mod[4:5, :]) + mod[3:4, :]).astype(BF16)
    act = (_silu(_dot(v, wg_ref[...])) * _dot(v, wu_ref[...])).astype(BF16)
    y = _dot(act, wd_ref[...])
    o_ref[0] = h + mod[5:6, :] * _rms(y, g_ref[3:4, :])


def _ffn(hs, mod_l, gains, wg, wu, wd, nctx_tiles):
    bsz, L, d = hs.shape
    return pl.pallas_call(
        _ffn_kernel,
        grid=(bsz, L // TM),
        in_specs=[
            pl.BlockSpec((1, TM, d), lambda b, i: (b, i, 0)),
            _mod_spec(d, bsz, nctx_tiles),
            pl.BlockSpec(gains.shape, lambda b, i: (0, 0)),
            _full_spec(wg), _full_spec(wu), _full_spec(wd),
        ],
        out_specs=pl.BlockSpec((1, TM, d), lambda b, i: (b, i, 0)),
        out_shape=jax.ShapeDtypeStruct((bsz, L, d), F32),
        compiler_params=_cparams("parallel", "parallel"),
    )(hs, mod_l, gains, wg, wu, wd)


def kernel(x, c, ctx, c_ctx, mod_w, mod_b, norm_g, attn_w_in, attn_w_out, attn_lambda, attn_subln,
           ret_w_in, ret_w_out, ret_decay_logit, ssm_lambda_re, ssm_lambda_im, ssm_log_dt,
           ssm_b_re, ssm_b_im, ssm_c_re, ssm_c_im, ssm_d, ssm_w_glu, ffn_w_gate, ffn_w_up, ffn_w_down):
    bsz, seq, d = x.shape
    nctx = ctx.shape[1]
    depth = mod_w.shape[0]
    assert nctx % TM == 0 and seq % TM == 0 and seq % GRID_W == 0
    nct = nctx // TM
    L = nctx + seq

    mods = _modulation(c, c_ctx, mod_w, mod_b)
    hs = jnp.concatenate([ctx, x], axis=1)
    da_dim = d // (2 * DA_HEADS)
    dk = d // RET_HEADS
    tab_a = _rope_table(seq, nctx, da_dim)
    tab_r = _rope_table(seq, nctx, dk)

    def tok_spec(width, col=0):
        return pl.BlockSpec((1, TM, width), lambda b, i: (b, i, col))

    for l in range(depth):
        kind, j = l % 3, l // 3
        last = l == depth - 1
        mod_l, gains = mods[l], norm_g[l]
        if kind == 0:
            lam_init = 0.8 - 0.6 * math.exp(-0.3 * l)
            q_t, k, v_t = _attn_proj(hs, mod_l, gains, attn_w_in[j].astype(BF16), tab_a,
                                     da_dim ** -0.5 * math.log2(math.e), da_dim // 4, nct)
            o = _attention(q_t, k, v_t, attn_lambda[j], attn_subln[j], nctx, lam_init, not last)
            w_out = attn_w_out[j].astype(BF16)
            hs = _mixer_out(_attn_out_kernel, hs, mod_l, gains, nct, (o, w_out),
                            [tok_spec(d), _full_spec(w_out)], skip_ctx=last)
            nct = 0 if last else nct
        elif kind == 1:
            chunks = ((0, d, 1.0, True), (d, d, dk ** -0.5, True)) + tuple(
                (c0, d, 1.0, False) for c0 in range(2 * d, 6 * d, d))
            p = _proj_in(hs, mod_l, gains, ret_w_in[j].astype(BF16), tab_r, chunks, dk // 4, nct)
            log_g = jax.nn.log_sigmoid(ret_decay_logit[j].astype(F32))
            o = _retention(p, log_g, nctx)
            w_out = ret_w_out[j].astype(BF16)
            hs = _mixer_out(_ret_out_kernel, hs, mod_l, gains, nct, (o, p, w_out),
                            [pl.BlockSpec((2, 1, TM, 2 * d), lambda b, i: (0, b, i, 0)),
                             tok_spec(2 * d, 2), _full_spec(w_out)])
        else:
            u = _normmod(hs, mod_l, gains, nct)
            weights = _s5_weights(ssm_lambda_re[j], ssm_lambda_im[j], ssm_log_dt[j], ssm_b_re[j],
                                  ssm_b_im[j], ssm_c_re[j], ssm_c_im[j])
            y = _s5_core(u, weights, nctx)
            w_glu = ssm_w_glu[j].astype(BF16)
            dskip = ssm_d[j].astype(F32).reshape(1, d)
            y_spec = pl.BlockSpec((1,) + y.shape[1:2] + (TM // SSM_CHUNK,) + y.shape[3:],
                                  lambda b, i: (b, 0, i, 0))
            hs = _mixer_out(_s5_out_kernel, hs, mod_l, gains, nct, (y, dskip, w_glu),
                            [y_spec, _full_spec(dskip), _full_spec(w_glu)],
                            scratch=[pltpu.VMEM((d // LANES, TM, LANES), F32)])
        hs = _ffn(hs, mod_l, gains, ffn_w_gate[l].astype(BF16), ffn_w_up[l].astype(BF16),
                  ffn_w_down[l].astype(BF16), nct)
    return hs if hs.shape[1] == seq else hs[:, nctx:, :]
```

```python
import functools
import math

import jax
import jax.numpy as jnp
from jax import lax
from jax.experimental import pallas as pl
from jax.experimental.pallas import tpu as pltpu

F32 = jnp.float32
BF16 = jnp.bfloat16

GRID_W = 64
DA_HEADS = 8
RET_HEADS = 4
RET_CHUNK = 256
SSM_GROUP = 16
SSM_STATE = 64
SSM_CHUNK = 16
SSM_GROUP_BLOCK = 8
ROPE_THETA = 10000.0
NORM_EPS = 1e-6
LANES = 128
MXU_COLS = 256
TM = 256
ATTN_TQ = 1024
VT_PAD = 16
VMEM_LIMIT = 56 * 1024 * 1024


def _cparams(*sem):
    return pltpu.CompilerParams(dimension_semantics=sem, vmem_limit_bytes=VMEM_LIMIT)


def _rms(x, gain):
    return x * lax.rsqrt(jnp.mean(x * x, axis=-1, keepdims=True) + NORM_EPS) * gain


def _silu(x):
    return x * jax.nn.sigmoid(x)


def _dot(a, b):
    return jnp.dot(a, b, preferred_element_type=F32)


def _dot_nt(a, b):
    return lax.dot_general(a, b, (((1,), (1,)), ((), ())), preferred_element_type=F32)


def _dot_tn(a, b):
    return lax.dot_general(a, b, (((0,), (0,)), ((), ())), preferred_element_type=F32)


def _mod_kernel(s_ref, w_ref, b_ref, o_ref):
    s = _silu(s_ref[...])
    o_ref[0] = jnp.dot(s, w_ref[0], preferred_element_type=F32,
                       precision=lax.Precision.HIGHEST) + b_ref[0]


def _modulation(c, c_ctx, mod_w, mod_b):
    depth, d, n = mod_w.shape
    bsz = c.shape[0]
    rows = 8
    s = jnp.zeros((rows, d), F32).at[:bsz].set(c).at[bsz].set(c_ctx)
    tn = 1536
    out = pl.pallas_call(
        _mod_kernel,
        grid=(depth, n // tn),
        in_specs=[
            pl.BlockSpec((rows, d), lambda l, j: (0, 0)),
            pl.BlockSpec((1, d, tn), lambda l, j: (l, 0, j)),
            pl.BlockSpec((1, 1, tn), lambda l, j: (l, 0, j)),
        ],
        out_specs=pl.BlockSpec((1, rows, tn), lambda l, j: (l, 0, j)),
        out_shape=jax.ShapeDtypeStruct((depth, rows, n), F32),
        compiler_params=_cparams("parallel", "parallel"),
    )(s, mod_w, mod_b.reshape(depth, 1, n))
    return out.reshape(depth, rows, 6, d)


def _mod_spec(d, bsz, nctx_tiles):
    return pl.BlockSpec((1, 6, d), lambda b, i: (jnp.where(i < nctx_tiles, bsz, b), 0, 0))


def _rope_tables(seq, ctx, head_dim):
    q = head_dim // 4
    freqs = ROPE_THETA ** (-jnp.arange(q, dtype=F32) / q)
    rep = max(1, LANES // head_dim)

    def table(pos, first_seg, identity):
        ang = pos[:, None] * freqs
        cos = jnp.ones_like(ang) if identity else jnp.cos(ang)
        sin = jnp.zeros_like(ang) if identity else jnp.sin(ang)
        z = jnp.zeros_like(ang)
        segs = lambda a, b: jnp.concatenate([z] * first_seg + [a, b] + [z] * (2 - first_seg), axis=1)
        parts = (segs(cos, cos), segs(-sin, z), segs(z, sin))
        return jnp.concatenate([jnp.tile(p, (1, rep)) for p in parts], axis=1)

    per_tile = TM // GRID_W
    rows = jnp.arange(seq // GRID_W, dtype=F32)
    cols = jnp.arange(GRID_W, dtype=F32)
    row_lat = table(rows, 0, False).reshape(seq // TM, per_tile, -1)
    row_ctx = table(jnp.zeros((ctx // GRID_W,), F32), 0, True).reshape(ctx // TM, per_tile, -1)
    row_tab = jnp.concatenate([row_ctx, row_lat], axis=0)
    col_tab = jnp.stack([table(cols, 2, True), table(cols, 2, False)], axis=0)
    return row_tab, col_tab


def _rope_tile(row_ref, col_ref):
    col = col_ref[0]
    return jnp.concatenate([row_ref[0, a:a + 1, :] + col for a in range(row_ref.shape[1])], axis=0)


def _rope_specs(row_tab, col_tab, nctx_tiles):
    return [pl.BlockSpec((1,) + row_tab.shape[1:], lambda b, i: (i, 0, 0)),
            pl.BlockSpec((1,) + col_tab.shape[1:], lambda b, i: (jnp.where(i < nctx_tiles, 0, 1), 0, 0))]


def _apply_rope(y, tab, quarter):
    w = tab.shape[1] // 3
    cos, s_minus, s_plus = tab[:, :w], tab[:, w:2 * w], tab[:, 2 * w:]
    outs = []
    for k in range(y.shape[1] // LANES):
        yk = y[:, k * LANES:(k + 1) * LANES]
        t0 = (k * LANES) % w
        ck, mk, pk = (t[:, t0:t0 + LANES] for t in (cos, s_minus, s_plus))
        if 2 * quarter == LANES:
            outs.append(yk * ck + pltpu.roll(yk, quarter, 1) * (mk + pk))
        else:
            outs.append(yk * ck + pltpu.roll(yk, LANES - quarter, 1) * mk
                        + pltpu.roll(yk, quarter, 1) * pk)
    return jnp.concatenate(outs, axis=1)


def _proj_in_kernel(x_ref, mod_ref, g_ref, w_ref, row_ref, col_ref, o_ref, *, chunks, quarter):
    mod = mod_ref[0]
    u = (_rms(x_ref[0], g_ref[0:1, :]) * (1.0 + mod[1:2, :]) + mod[0:1, :]).astype(BF16)
    tab = _rope_tile(row_ref, col_ref)
    for c0, width, mult, rope in chunks:
        y = _dot(u, w_ref[:, c0:c0 + width])
        if mult != 1.0:
            y = y * mult
        if rope:
            y = _apply_rope(y, tab, quarter)
        o_ref[0, :, c0:c0 + width] = y.astype(BF16)


def _proj_in(hs, mod_l, gains, w, tabs, chunks, quarter, nctx_tiles):
    bsz, L, d = hs.shape
    n = w.shape[1]
    return pl.pallas_call(
        functools.partial(_proj_in_kernel, chunks=chunks, quarter=quarter),
        grid=(bsz, L // TM),
        in_specs=[
            pl.BlockSpec((1, TM, d), lambda b, i: (b, i, 0)),
            _mod_spec(d, bsz, nctx_tiles),
            pl.BlockSpec(gains.shape, lambda b, i: (0, 0)),
            pl.BlockSpec(w.shape, lambda b, i: (0, 0)),
        ] + _rope_specs(*tabs, nctx_tiles),
        out_specs=pl.BlockSpec((1, TM, n), lambda b, i: (b, i, 0)),
        out_shape=jax.ShapeDtypeStruct((bsz, L, n), BF16),
        compiler_params=_cparams("parallel", "parallel"),
    )(hs, mod_l, gains, w, *tabs)


def _attn_proj_kernel(x_ref, mod_ref, g_ref, w_ref, row_ref, col_ref, qt_ref, k_ref, vt_ref, *,
                      scale, quarter):
    mod = mod_ref[0]
    u = (_rms(x_ref[0], g_ref[0:1, :]) * (1.0 + mod[1:2, :]) + mod[0:1, :]).astype(BF16)
    d = x_ref.shape[2]
    tab = _rope_tile(row_ref, col_ref)
    heads = vt_ref.shape[1]
    qt_ref[0] = _apply_rope(_dot(u, w_ref[:, :d]) * scale, tab, quarter).T.astype(BF16)
    k_ref[0] = _apply_rope(_dot(u, w_ref[:, d:2 * d]), tab, quarter).astype(BF16)
    vt = _dot(u, w_ref[:, 2 * d:]).T.astype(BF16)
    vt_ref[0, :, 0, 0:LANES, :] = vt.reshape(heads, LANES, vt.shape[1])
    vt_ref[0, :, 0, LANES:, :] = jnp.ones((heads, VT_PAD, vt.shape[1]), BF16)


def _attn_proj(hs, mod_l, gains, w, tabs, scale, quarter, nctx_tiles):
    bsz, L, d = hs.shape
    nlat = L // TM - nctx_tiles
    return pl.pallas_call(
        functools.partial(_attn_proj_kernel, scale=scale, quarter=quarter),
        grid=(bsz, L // TM),
        in_specs=[
            pl.BlockSpec((1, TM, d), lambda b, i: (b, i, 0)),
            _mod_spec(d, bsz, nctx_tiles),
            pl.BlockSpec(gains.shape, lambda b, i: (0, 0)),
            pl.BlockSpec(w.shape, lambda b, i: (0, 0)),
        ] + _rope_specs(*tabs, nctx_tiles),
        out_specs=[
            pl.BlockSpec((1, d, TM), lambda b, i: (b, 0, jnp.where(i < nctx_tiles, i + nlat, i - nctx_tiles))),
            pl.BlockSpec((1, TM, d), lambda b, i: (b, i, 0)),
            pl.BlockSpec((1, DA_HEADS, 1, LANES + VT_PAD, TM), lambda b, i: (b, 0, i, 0, 0)),
        ],
        out_shape=[
            jax.ShapeDtypeStruct((bsz, d, L), BF16),
            jax.ShapeDtypeStruct((bsz, L, d), BF16),
            jax.ShapeDtypeStruct((bsz, DA_HEADS, L // TM, LANES + VT_PAD, TM), BF16),
        ],
        compiler_params=_cparams("parallel", "parallel"),
    )(hs, mod_l, gains, w, *tabs)


def _attn_kernel(lam_ref, sub_ref, qt_ref, k_ref, vt_ref, o_ref,
                 s_c, s_a, s_b, mb_scr, m_scr, acc_scr, *, nb, lam_init):
    ctx = s_c.shape[2]
    tk = s_a.shape[2]
    kv = vt_ref.shape[4]
    qt = qt_ref[0]
    row = lax.broadcasted_iota(jnp.int32, qt.shape, 0)
    zero = jnp.zeros_like(qt)
    qts = (jnp.where(row < LANES // 2, qt, zero), jnp.where(row >= LANES // 2, qt, zero))

    slot = {id(s_c): 0, id(s_a): 1, id(s_b): 2}

    nqb = qt.shape[1] // MXU_COLS

    def scores(kb, s_out):
        for part in range(2):
            for j in range(nqb):
                s = _dot(kb, qts[part][:, j * MXU_COLS:(j + 1) * MXU_COLS])
                s_out[part, j] = s
                mb_scr[slot[id(s_out)], part, j] = jnp.max(s, axis=0, keepdims=True)

    def stage(s_in, vt0, nxt):
        if nxt is not None:
            scores(*nxt)
        for part in range(2):
            for j in range(nqb):
                m = m_scr[part, j]
                m_new = jnp.maximum(m, mb_scr[slot[id(s_in)], part, j])
                alpha = jnp.exp2(m - m_new)
                pv = None
                for n in range(s_in.shape[2] // kv):
                    p = jnp.exp2(s_in[part, j, n * kv:(n + 1) * kv, :] - m_new).astype(BF16)
                    pv_n = _dot(vt_ref[0, 0, vt0 + n], p)
                    pv = pv_n if pv is None else pv + pv_n
                m_scr[part, j] = m_new
                acc_scr[part, j] = alpha * acc_scr[part, j] + pv

    m_scr[...] = jnp.full(m_scr.shape, -1e30, F32)
    acc_scr[...] = jnp.zeros(acc_scr.shape, F32)

    bufs = (s_b, s_a)

    def keys(t):
        return k_ref[0, pl.ds(pl.multiple_of(ctx + (t - 1) * tk, LANES), tk), :]

    def vt_index(t):
        return ctx // kv + (t - 1) * (tk // kv)

    scores(k_ref[0, 0:ctx, :], s_c)
    stage(s_c, 0, (keys(1), s_a) if nb >= 1 else None)
    n_pairs = max(0, (nb - 1) // 2 if nb % 2 else (nb - 2) // 2)

    def pair(u, carry):
        t = 2 * u + 1
        stage(s_a, vt_index(t), (keys(t + 1), s_b))
        stage(s_b, vt_index(t + 1), (keys(t + 2), s_a))
        return carry

    if n_pairs > 0:
        lax.fori_loop(0, n_pairs, pair, 0)
    for t in range(2 * n_pairs + 1, nb + 1):
        stage(bufs[t % 2], vt_index(t), (keys(t + 1), bufs[(t + 1) % 2]) if t < nb else None)

    lv = lam_ref[...]
    lam = (jnp.exp(jnp.sum(lv[0:1] * lv[1:2], axis=-1, keepdims=True))
           - jnp.exp(jnp.sum(lv[2:3] * lv[3:4], axis=-1, keepdims=True)) + lam_init)
    a1, a2 = (jnp.concatenate([acc_scr[part, j] for j in range(nqb)], axis=1) for part in range(2))
    o = (a1[:LANES] / a1[LANES:LANES + 1] - lam * (a2[:LANES] / a2[LANES:LANES + 1])).T
    o = _rms(o, sub_ref[...]) * (1.0 - lam_init)
    o_ref[0] = o.astype(BF16)


def _attention_call(q_t, k, v_t, lam_vec, subln, ctx, tk, nb, lam_init, tq, q_off, n_out):
    bsz, d, _ = q_t.shape
    kv = v_t.shape[4]
    k_rows = ctx + nb * tk
    nqb = tq // MXU_COLS
    return pl.pallas_call(
        functools.partial(_attn_kernel, nb=nb, lam_init=lam_init),
        grid=(bsz, DA_HEADS, n_out // tq),
        in_specs=[
            pl.BlockSpec(lam_vec.shape, lambda b, h, i: (0, 0)),
            pl.BlockSpec((1, LANES), lambda b, h, i: (0, 0)),
            pl.BlockSpec((1, LANES, tq), lambda b, h, i: (b, h, i + q_off // tq)),
            pl.BlockSpec((1, k_rows, LANES), lambda b, h, i: (b, 0, h)),
            pl.BlockSpec((1, 1, k_rows // kv) + v_t.shape[3:], lambda b, h, i: (b, h, 0, 0, 0)),
        ],
        out_specs=pl.BlockSpec((1, tq, LANES), lambda b, h, i: (b, i, h)),
        out_shape=jax.ShapeDtypeStruct((bsz, n_out, d), BF16),
        scratch_shapes=[pltpu.VMEM((2, nqb, ctx, MXU_COLS), F32), pltpu.VMEM((2, nqb, tk, MXU_COLS), F32),
                        pltpu.VMEM((2, nqb, tk, MXU_COLS), F32), pltpu.VMEM((3, 2, nqb, 1, MXU_COLS), F32),
                        pltpu.VMEM((2, nqb, 1, MXU_COLS), F32),
                        pltpu.VMEM((2, nqb, LANES + VT_PAD, MXU_COLS), F32)],
        compiler_params=_cparams("parallel", "parallel", "arbitrary"),
    )(lam_vec, subln.reshape(1, LANES), q_t, k, v_t)


def _attention(q_t, k, v_t, lam_vec, subln, ctx, lam_init, with_ctx):
    L = k.shape[1]
    seq = L - ctx
    tk = min(1024, seq)
    nb = seq // tk
    tq = min(ATTN_TQ, seq)
    o_lat = _attention_call(q_t, k, v_t, lam_vec, subln, ctx, tk, nb, lam_init, tq, 0, seq)
    if not with_ctx:
        return o_lat
    o_ctx = _attention_call(q_t, k, v_t, lam_vec, subln, ctx, tk, 0, lam_init, TM, seq, ctx)
    return jnp.concatenate([o_ctx, o_lat], axis=1)


def _ret_kernel(lg_ref, q_ref, k_ref, v_ref, o_ref, r_scr):
    d = pl.program_id(1)
    c = pl.program_id(2)
    C = q_ref.shape[1]
    heads, dk, dv = r_scr.shape

    @pl.when(c == 0)
    def _():
        r_scr[...] = jnp.zeros_like(r_scr)

    fwd = d == 0
    ii = lax.broadcasted_iota(jnp.int32, (C, C), 0)
    jj = lax.broadcasted_iota(jnp.int32, (C, C), 1)
    dd = jnp.where(fwd, ii - jj, jj - ii)
    causal = dd >= 0
    lag = jnp.maximum(dd, 0).astype(F32)
    idx = lax.broadcasted_iota(jnp.int32, (C, 1), 0)
    q_lag = jnp.where(fwd, idx + 1, C - idx).astype(F32)
    k_lag = jnp.where(fwd, C - 1 - idx, idx).astype(F32)

    for h in range(heads):
        lg = lg_ref[0, h:h + 1, 0:1]
        q = q_ref[0, :, h * dk:(h + 1) * dk]
        k = k_ref[0, :, h * dk:(h + 1) * dk]
        v = v_ref[0, :, h * dv:(h + 1) * dv]
        r = r_scr[h]
        s = _dot_nt(q, k) * jnp.where(causal, jnp.exp(lg * lag), 0.0)
        inner = _dot(s.astype(BF16), v)
        cross = _dot((q.astype(F32) * jnp.exp(lg * q_lag)).astype(BF16), r.astype(BF16))
        o_ref[0, 0, :, h * dv:(h + 1) * dv] = (inner + cross).astype(BF16)
        r_scr[h] = jnp.exp(lg * float(C)) * r + _dot_tn((k.astype(F32) * jnp.exp(lg * k_lag)).astype(BF16), v)


def _retention(p, log_g, ctx):
    bsz, L, n = p.shape
    d = n // 6
    C = RET_CHUNK
    nc, ncc = L // C, ctx // C

    def chunk(dr, c):
        back = jnp.where(c < ncc, ncc - 1 - c, nc + ncc - 1 - c)
        return jnp.where(dr == 0, c, back)

    lg = jnp.broadcast_to(log_g.reshape(2, RET_HEADS, 1), (2, RET_HEADS, LANES))
    return pl.pallas_call(
        _ret_kernel,
        grid=(bsz, 2, nc),
        in_specs=[
            pl.BlockSpec((1, RET_HEADS, LANES), lambda b, dr, c: (dr, 0, 0)),
            pl.BlockSpec((1, C, d), lambda b, dr, c: (b, chunk(dr, c), 0)),
            pl.BlockSpec((1, C, d), lambda b, dr, c: (b, chunk(dr, c), 1)),
            pl.BlockSpec((1, C, 2 * d), lambda b, dr, c: (b, chunk(dr, c), 1)),
        ],
        out_specs=pl.BlockSpec((1, 1, C, 2 * d), lambda b, dr, c: (dr, b, chunk(dr, c), 0)),
        out_shape=jax.ShapeDtypeStruct((2, bsz, L, 2 * d), BF16),
        scratch_shapes=[pltpu.VMEM((RET_HEADS, d // RET_HEADS, 2 * d // RET_HEADS), F32)],
        compiler_params=_cparams("parallel", "arbitrary", "arbitrary"),
    )(lg, p, p, p)


def _transpose_pieces(tiles):
    hg = SSM_GROUP
    per = len(tiles)
    piece = lax.broadcasted_iota(jnp.int32, tiles[0].shape, 1) // hg
    k = per // 2
    while k >= 1:
        keep = (piece & k) == 0
        out = list(tiles)
        for r in range(per):
            if r & k == 0:
                x, y = tiles[r], tiles[r + k]
                out[r] = jnp.where(keep, x, pltpu.roll(y, hg * k, 1))
                out[r + k] = jnp.where(keep, pltpu.roll(x, LANES - hg * k, 1), y)
        tiles = out
        k //= 2
    return tiles


def _to_group_major(u_scr, o_ref):
    n, hg = SSM_CHUNK, SSM_GROUP
    per = LANES // hg
    ntile, tm, _ = u_scr.shape
    rows = tm // n
    piece = lax.broadcasted_iota(jnp.int32, (rows, LANES), 1) // hg
    for lt in range(ntile):
        for th in range(n * hg // LANES):
            src = [u_scr[lt, pl.ds(th * per + tp, rows, stride=n), :] for tp in range(per)]
            for j in range(per):
                acc = None
                for tp in range(per):
                    shift = (hg * (tp - j)) % LANES
                    moved = src[tp] if shift == 0 else pltpu.roll(src[tp], shift, 1)
                    acc = moved if acc is None else jnp.where(piece == tp, moved, acc)
                o_ref[0, lt * per + j, :, th * LANES:(th + 1) * LANES] = acc.astype(o_ref.dtype)


def _from_group_major(y_ref, o_scr):
    n, hg = SSM_CHUNK, SSM_GROUP
    per = LANES // hg
    ntile, tm, _ = o_scr.shape
    rows = tm // n
    for lt in range(ntile):
        for th in range(n * hg // LANES):
            src = [y_ref[0, lt * per + j, :, th * LANES:(th + 1) * LANES] for j in range(per)]
            for tp, tile in enumerate(_transpose_pieces(src)):
                o_scr[lt, pl.ds(th * per + tp, rows, stride=n), :] = tile


def _normmod_kernel(x_ref, mod_ref, g_ref, o_ref, u_scr):
    mod = mod_ref[0]
    u = _rms(x_ref[0], g_ref[0:1, :]) * (1.0 + mod[1:2, :]) + mod[0:1, :]
    for lt in range(u_scr.shape[0]):
        u_scr[lt] = u[:, lt * LANES:(lt + 1) * LANES]
    _to_group_major(u_scr, o_ref)


def _normmod(hs, mod_l, gains, nctx_tiles):
    bsz, L, d = hs.shape
    G, rows, width = d // SSM_GROUP, TM // SSM_CHUNK, SSM_CHUNK * SSM_GROUP
    return pl.pallas_call(
        _normmod_kernel,
        grid=(bsz, L // TM),
        in_specs=[
            pl.BlockSpec((1, TM, d), lambda b, i: (b, i, 0)),
            _mod_spec(d, bsz, nctx_tiles),
            pl.BlockSpec(gains.shape, lambda b, i: (0, 0)),
        ],
        out_specs=pl.BlockSpec((1, G, rows, width), lambda b, i: (b, 0, i, 0)),
        out_shape=jax.ShapeDtypeStruct((bsz, G, L // SSM_CHUNK, width), BF16),
        scratch_shapes=[pltpu.VMEM((d // LANES, TM, LANES), F32)],
        compiler_params=_cparams("parallel", "parallel"),
    )(hs, mod_l, gains)


def _s5_kernel(x_ref, t_ref, wb_ref, wc_ref, a_ref, y_ref, st_scr, *, nctx):
    gb = x_ref.shape[1]
    nch = x_ref.shape[2]
    for gi in range(gb):
        x = x_ref[0, gi]
        for dr in range(2):
            s = _dot(x, wb_ref[gi, dr])
            st_scr[2 * dr, pl.ds(gi, nch, stride=gb), :] = s[:, :LANES]
            st_scr[2 * dr + 1, pl.ds(gi, nch, stride=gb), :] = s[:, LANES:]

    a = a_ref[...]

    def step(dr, row, h_re, h_im):
        s_re = st_scr[2 * dr, pl.ds(row, gb), :]
        s_im = st_scr[2 * dr + 1, pl.ds(row, gb), :]
        st_scr[2 * dr, pl.ds(row, gb), :] = h_re
        st_scr[2 * dr + 1, pl.ds(row, gb), :] = h_im
        a_re, a_im = a[dr, 0], a[dr, 1]
        return a_re * h_re - a_im * h_im + s_re, a_re * h_im + a_im * h_re + s_im

    def body(t, carry):
        f_re, f_im, b_re, b_im = carry
        cb = jnp.where(t < nctx, nctx - 1 - t, nch + nctx - 1 - t)
        f_re, f_im = step(0, pl.multiple_of(t * gb, gb), f_re, f_im)
        b_re, b_im = step(1, pl.multiple_of(cb * gb, gb), b_re, b_im)
        return f_re, f_im, b_re, b_im

    z = jnp.zeros((gb, LANES), F32)
    lax.fori_loop(0, nch, body, (z, z, z, z))

    for gi in range(gb):
        y = _dot(x_ref[0, gi], t_ref[gi])
        for dr in range(2):
            h = jnp.concatenate([st_scr[2 * dr, pl.ds(gi, nch, stride=gb), :],
                                 st_scr[2 * dr + 1, pl.ds(gi, nch, stride=gb), :]], axis=1)
            y = y + _dot(h.astype(BF16), wc_ref[gi, dr])
        y_ref[0, gi] = y


def _s5_weights(lam_re, lam_im, log_dt, b_re, b_im, c_re, c_im):
    n = SSM_CHUNK
    lam = lax.complex(lam_re.astype(F32), lam_im.astype(F32))
    ldt = lam * jnp.exp(log_dt.astype(F32))[..., None]
    lam_bar = jnp.exp(ldt)
    b_bar = ((lam_bar - 1.0) / lam)[..., None] * lax.complex(b_re.astype(F32), b_im.astype(F32))
    c_mat = lax.complex(c_re.astype(F32), c_im.astype(F32))
    steps = jnp.arange(n + 1, dtype=F32)
    pw = jnp.exp(ldt[None] * steps[:, None, None, None])
    G, P, Hg = b_bar.shape[1:]

    kern = jnp.real(jnp.einsum('kdgp,dgjp,dgpi->dkgij', pw[:n], c_mat, b_bar))
    s_idx = jnp.arange(n)[:, None]
    t_idx = jnp.arange(n)[None, :]
    lag = t_idx - s_idx
    kf = jnp.where((lag >= 0)[..., None, None, None], kern[0][jnp.clip(lag, 0, n - 1)], 0.0)
    kb = jnp.where((lag <= 0)[..., None, None, None], kern[1][jnp.clip(-lag, 0, n - 1)], 0.0)
    t_mat = (kf + kb).transpose(2, 0, 3, 1, 4).reshape(G, n * Hg, n * Hg)

    zpad = jnp.zeros((G, n * Hg, LANES - P), F32)
    zrow = jnp.zeros((G, LANES - P, n * Hg), F32)
    wbs, wcs = [], []
    for dr in range(2):
        e_in = pw[:n][::-1, dr] if dr == 0 else pw[:n, dr]
        wb = (e_in[:, :, :, None] * b_bar[dr][None]).transpose(1, 0, 3, 2).reshape(G, n * Hg, P)
        wbs.append(jnp.concatenate([jnp.real(wb), zpad, jnp.imag(wb), zpad], axis=2))
        e_out = pw[1:, dr] if dr == 0 else pw[1:][::-1, dr]
        wc = (c_mat[dr][None] * e_out[:, :, None, :]).transpose(1, 3, 0, 2).reshape(G, P, n * Hg)
        wcs.append(jnp.concatenate([jnp.real(wc), zrow, -jnp.imag(wc), zrow], axis=1))
    w_b = jnp.stack(wbs, axis=1)
    w_c = jnp.stack(wcs, axis=1)
    a_n = pw[n]
    pad = jnp.zeros((2, G, LANES - P), F32)
    a = jnp.stack([jnp.concatenate([jnp.real(a_n), pad], -1),
                   jnp.concatenate([jnp.imag(a_n), pad], -1)], axis=1)
    return t_mat.astype(BF16), w_b.astype(BF16), w_c.astype(BF16), a


def _s5_core(x, weights, ctx):
    bsz, G, nch, width = x.shape
    n, gb = SSM_CHUNK, SSM_GROUP_BLOCK
    t_mat, w_b, w_c, a = weights
    return pl.pallas_call(
        functools.partial(_s5_kernel, nctx=ctx // n),
        grid=(bsz, G // gb),
        in_specs=[
            pl.BlockSpec((1, gb, nch, width), lambda b, g: (b, g, 0, 0)),
            pl.BlockSpec((gb, width, width), lambda b, g: (g, 0, 0)),
            pl.BlockSpec((gb, 2, width, 2 * LANES), lambda b, g: (g, 0, 0, 0)),
            pl.BlockSpec((gb, 2, 2 * LANES, width), lambda b, g: (g, 0, 0, 0)),
            pl.BlockSpec((2, 2, gb, LANES), lambda b, g: (0, 0, g, 0)),
        ],
        out_specs=pl.BlockSpec((1, gb, nch, width), lambda b, g: (b, g, 0, 0)),
        out_shape=jax.ShapeDtypeStruct((bsz, G, nch, width), F32),
        scratch_shapes=[pltpu.VMEM((4, nch * gb, LANES), F32)],
        compiler_params=_cparams("parallel", "parallel"),
    )(x, t_mat, w_b, w_c, a)


def _finish(h, y, mod, g_ref, o_ref):
    o_ref[0] = h + mod[2:3, :] * _rms(y, g_ref[1:2, :])


def _attn_out_kernel(h_ref, mod_ref, g_ref, o_in_ref, w_ref, o_ref):
    _finish(h_ref[0], _dot(o_in_ref[0], w_ref[...]), mod_ref[0], g_ref, o_ref)


def _ret_out_kernel(h_ref, mod_ref, g_ref, o_in_ref, gate_ref, w_ref, o_ref):
    o = o_in_ref[0, 0].astype(F32) + o_in_ref[1, 0].astype(F32)
    dv = o.shape[1] // RET_HEADS
    parts = []
    for hd in range(RET_HEADS):
        oh = o[:, hd * dv:(hd + 1) * dv]
        parts.append(oh * lax.rsqrt(jnp.mean(oh * oh, axis=-1, keepdims=True) + NORM_EPS))
    o = jnp.concatenate(parts, axis=1)
    y = _dot((_silu(gate_ref[0].astype(F32)) * o).astype(BF16), w_ref[...])
    _finish(h_ref[0], y, mod_ref[0], g_ref, o_ref)


def _s5_out_kernel(h_ref, mod_ref, g_ref, y_ref, dskip_ref, w_ref, o_ref, y_scr):
    h = h_ref[0]
    mod = mod_ref[0]
    d = h.shape[1]
    _from_group_major(y_ref, y_scr)
    u = _rms(h, g_ref[0:1, :]) * (1.0 + mod[1:2, :]) + mod[0:1, :]
    y_ssm = jnp.concatenate([y_scr[lt] for lt in range(y_scr.shape[0])], axis=1)
    z = y_ssm + dskip_ref[...] * u
    gl = 0.5 * z * (1.0 + jnp.tanh(math.sqrt(2.0 / math.pi) * (z + 0.044715 * (z * z * z))))
    gl = gl.astype(BF16)
    y = _dot(gl, w_ref[:, :d]) * jax.nn.sigmoid(_dot(gl, w_ref[:, d:]))
    _finish(h, y, mod, g_ref, o_ref)


def _mixer_out(kernel, hs, mod_l, gains, nctx_tiles, extra, extra_specs, scratch=(), skip_ctx=False):
    bsz, L, d = hs.shape
    off = nctx_tiles if skip_ctx else 0
    n_tiles = L // TM - off
    return pl.pallas_call(
        kernel,
        grid=(bsz, n_tiles),
        in_specs=[
            pl.BlockSpec((1, TM, d), lambda b, i: (b, i + off, 0)),
            _mod_spec(d, bsz, nctx_tiles - off),
            pl.BlockSpec(gains.shape, lambda b, i: (0, 0)),
        ] + extra_specs,
        out_specs=pl.BlockSpec((1, TM, d), lambda b, i: (b, i, 0)),
        out_shape=jax.ShapeDtypeStruct((bsz, n_tiles * TM, d), F32),
        scratch_shapes=list(scratch),
        compiler_params=_cparams("parallel", "parallel"),
    )(hs, mod_l, gains, *extra)


def _full_spec(a):
    nd = a.ndim
    return pl.BlockSpec(a.shape, lambda b, i: (0,) * nd)


def _ffn_kernel(h_ref, mod_ref, g_ref, wg_ref, wu_ref, wd_ref, o_ref):
    for b in range(h_ref.shape[0]):
        h = h_ref[b]
        mod = mod_ref[b]
        v = (_rms(h, g_ref[2:3, :]) * (1.0 + mod[4:5, :]) + mod[3:4, :]).astype(BF16)
        act = (_silu(_dot(v, wg_ref[...])) * _dot(v, wu_ref[...])).astype(BF16)
        y = _dot(act, wd_ref[...])
        o_ref[b] = h + mod[5:6, :] * _rms(y, g_ref[3:4, :])


def _ffn(hs, mod_l, gains, wg, wu, wd, nctx_tiles):
    bsz, L, d = hs.shape
    mod_rows = jnp.concatenate([mod_l[:bsz], jnp.broadcast_to(mod_l[bsz], (bsz,) + mod_l.shape[1:])])

    def resident(w):
        return pl.BlockSpec(w.shape, lambda i: (0, 0), pipeline_mode=pl.Buffered(1))

    return pl.pallas_call(
        _ffn_kernel,
        grid=(L // TM,),
        in_specs=[
            pl.BlockSpec((bsz, TM, d), lambda i: (0, i, 0)),
            pl.BlockSpec((bsz, 6, d), lambda i: (jnp.where(i < nctx_tiles, 1, 0), 0, 0)),
            pl.BlockSpec(gains.shape, lambda i: (0, 0)),
            resident(wg), resident(wu), resident(wd),
        ],
        out_specs=pl.BlockSpec((bsz, TM, d), lambda i: (0, i, 0)),
        out_shape=jax.ShapeDtypeStruct((bsz, L, d), F32),
        compiler_params=_cparams("parallel"),
    )(hs, mod_rows, gains, wg, wu, wd)


def kernel(x, c, ctx, c_ctx, mod_w, mod_b, norm_g, attn_w_in, attn_w_out, attn_lambda, attn_subln,
           ret_w_in, ret_w_out, ret_decay_logit, ssm_lambda_re, ssm_lambda_im, ssm_log_dt,
           ssm_b_re, ssm_b_im, ssm_c_re, ssm_c_im, ssm_d, ssm_w_glu, ffn_w_gate, ffn_w_up, ffn_w_down):
    bsz, seq, d = x.shape
    nctx = ctx.shape[1]
    depth = mod_w.shape[0]
    assert nctx % TM == 0 and seq % TM == 0 and seq % GRID_W == 0
    nct = nctx // TM
    L = nctx + seq

    mods = _modulation(c, c_ctx, mod_w, mod_b)
    hs = jnp.concatenate([ctx, x], axis=1)
    da_dim = d // (2 * DA_HEADS)
    dk = d // RET_HEADS
    tab_a = _rope_tables(seq, nctx, da_dim)
    tab_r = _rope_tables(seq, nctx, dk)

    def tok_spec(width, col=0):
        return pl.BlockSpec((1, TM, width), lambda b, i: (b, i, col))

    for l in range(depth):
        kind, j = l % 3, l // 3
        last = l == depth - 1
        mod_l, gains = mods[l], norm_g[l]
        if kind == 0:
            lam_init = 0.8 - 0.6 * math.exp(-0.3 * l)
            q_t, k, v_t = _attn_proj(hs, mod_l, gains, attn_w_in[j].astype(BF16), tab_a,
                                     da_dim ** -0.5 * math.log2(math.e), da_dim // 4, nct)
            o = _attention(q_t, k, v_t, attn_lambda[j], attn_subln[j], nctx, lam_init, not last)
            w_out = attn_w_out[j].astype(BF16)
            hs = _mixer_out(_attn_out_kernel, hs, mod_l, gains, nct, (o, w_out),
                            [tok_spec(d), _full_spec(w_out)], skip_ctx=last)
            nct = 0 if last else nct
        elif kind == 1:
            chunks = ((0, d, 1.0, True), (d, d, dk ** -0.5, True)) + tuple(
                (c0, d, 1.0, False) for c0 in range(2 * d, 6 * d, d))
            p = _proj_in(hs, mod_l, gains, ret_w_in[j].astype(BF16), tab_r, chunks, dk // 4, nct)
            log_g = jax.nn.log_sigmoid(ret_decay_logit[j].astype(F32))
            o = _retention(p, log_g, nctx)
            w_out = ret_w_out[j].astype(BF16)
            hs = _mixer_out(_ret_out_kernel, hs, mod_l, gains, nct, (o, p, w_out),
                            [pl.BlockSpec((2, 1, TM, 2 * d), lambda b, i: (0, b, i, 0)),
                             tok_spec(2 * d, 2), _full_spec(w_out)])
        else:
            u = _normmod(hs, mod_l, gains, nct)
            weights = _s5_weights(ssm_lambda_re[j], ssm_lambda_im[j], ssm_log_dt[j], ssm_b_re[j],
                                  ssm_b_im[j], ssm_c_re[j], ssm_c_im[j])
            y = _s5_core(u, weights, nctx)
            w_glu = ssm_w_glu[j].astype(BF16)
            dskip = ssm_d[j].astype(F32).reshape(1, d)
            y_spec = pl.BlockSpec((1,) + y.shape[1:2] + (TM // SSM_CHUNK,) + y.shape[3:],
                                  lambda b, i: (b, 0, i, 0))
            hs = _mixer_out(_s5_out_kernel, hs, mod_l, gains, nct, (y, dskip, w_glu),
                            [y_spec, _full_spec(dskip), _full_spec(w_glu)],
                            scratch=[pltpu.VMEM((d // LANES, TM, LANES), F32)])
        hs = _ffn(hs, mod_l, gains, ffn_w_gate[l].astype(BF16), ffn_w_up[l].astype(BF16),
                  ffn_w_down[l].astype(BF16), nct)
    return hs if hs.shape[1] == seq else hs[:, nctx:, :]
```

```python
import functools
import math

import jax
import jax.numpy as jnp
from jax import lax
from jax.experimental import pallas as pl
from jax.experimental.pallas import tpu as pltpu

F32 = jnp.float32
BF16 = jnp.bfloat16

GRID_W = 64
DA_HEADS = 8
RET_HEADS = 4
RET_CHUNK = 256
SSM_GROUP = 16
SSM_STATE = 64
SSM_CHUNK = 16
SSM_GROUP_BLOCK = 8
ROPE_THETA = 10000.0
NORM_EPS = 1e-6
LANES = 128
MXU_COLS = 256
TM = 256
ATTN_TQ = 1024
VT_PAD = 16
VMEM_LIMIT = 56 * 1024 * 1024


def _cparams(*sem):
    return pltpu.CompilerParams(dimension_semantics=sem, vmem_limit_bytes=VMEM_LIMIT)


def _rms(x, gain):
    return x * lax.rsqrt(jnp.mean(x * x, axis=-1, keepdims=True) + NORM_EPS) * gain


def _silu(x):
    return x * jax.nn.sigmoid(x)


def _dot(a, b):
    return jnp.dot(a, b, preferred_element_type=F32)


def _dot_nt(a, b):
    return lax.dot_general(a, b, (((1,), (1,)), ((), ())), preferred_element_type=F32)


def _dot_tn(a, b):
    return lax.dot_general(a, b, (((0,), (0,)), ((), ())), preferred_element_type=F32)


def _mod_kernel(s_ref, w_ref, b_ref, o_ref):
    s = _silu(s_ref[...])
    o_ref[0] = jnp.dot(s, w_ref[0], preferred_element_type=F32,
                       precision=lax.Precision.HIGHEST) + b_ref[0]


def _modulation(c, c_ctx, mod_w, mod_b):
    depth, d, n = mod_w.shape
    bsz = c.shape[0]
    rows = 8
    s = jnp.zeros((rows, d), F32).at[:bsz].set(c).at[bsz].set(c_ctx)
    tn = 1536
    out = pl.pallas_call(
        _mod_kernel,
        grid=(depth, n // tn),
        in_specs=[
            pl.BlockSpec((rows, d), lambda l, j: (0, 0)),
            pl.BlockSpec((1, d, tn), lambda l, j: (l, 0, j)),
            pl.BlockSpec((1, 1, tn), lambda l, j: (l, 0, j)),
        ],
        out_specs=pl.BlockSpec((1, rows, tn), lambda l, j: (l, 0, j)),
        out_shape=jax.ShapeDtypeStruct((depth, rows, n), F32),
        compiler_params=_cparams("parallel", "parallel"),
    )(s, mod_w, mod_b.reshape(depth, 1, n))
    return out.reshape(depth, rows, 6, d)


def _mod_rows(mod_l, bsz):
    return jnp.concatenate([mod_l[:bsz], jnp.broadcast_to(mod_l[bsz], (bsz,) + mod_l.shape[1:])])


def _mod_spec(d, bsz, nctx_tiles):
    return pl.BlockSpec((bsz, 6, d), lambda i: (jnp.where(i < nctx_tiles, 1, 0), 0, 0))


def _tok_spec(bsz, width, col=0, off=0):
    return pl.BlockSpec((bsz, TM, width), lambda i: (0, i + off, col))


def _resident(a):
    nd = a.ndim
    return pl.BlockSpec(a.shape, lambda i: (0,) * nd, pipeline_mode=pl.Buffered(1))


def _rope_tables(seq, ctx, head_dim):
    q = head_dim // 4
    freqs = ROPE_THETA ** (-jnp.arange(q, dtype=F32) / q)
    rep = max(1, LANES // head_dim)

    def table(pos, first_seg, identity):
        ang = pos[:, None] * freqs
        cos = jnp.ones_like(ang) if identity else jnp.cos(ang)
        sin = jnp.zeros_like(ang) if identity else jnp.sin(ang)
        z = jnp.zeros_like(ang)
        segs = lambda a, b: jnp.concatenate([z] * first_seg + [a, b] + [z] * (2 - first_seg), axis=1)
        parts = (segs(cos, cos), segs(-sin, z), segs(z, sin))
        return jnp.concatenate([jnp.tile(p, (1, rep)) for p in parts], axis=1)

    per_tile = TM // GRID_W
    rows = jnp.arange(seq // GRID_W, dtype=F32)
    cols = jnp.arange(GRID_W, dtype=F32)
    row_lat = table(rows, 0, False).reshape(seq // TM, per_tile, -1)
    row_ctx = table(jnp.zeros((ctx // GRID_W,), F32), 0, True).reshape(ctx // TM, per_tile, -1)
    row_tab = jnp.concatenate([row_ctx, row_lat], axis=0)
    col_tab = jnp.stack([table(cols, 2, True), table(cols, 2, False)], axis=0)
    return row_tab, col_tab


def _rope_tile(row_ref, col_ref):
    col = col_ref[0]
    return jnp.concatenate([row_ref[0, a:a + 1, :] + col for a in range(row_ref.shape[1])], axis=0)


def _rope_specs(row_tab, col_tab, nctx_tiles):
    return [pl.BlockSpec((1,) + row_tab.shape[1:], lambda i: (i, 0, 0)),
            pl.BlockSpec((1,) + col_tab.shape[1:], lambda i: (jnp.where(i < nctx_tiles, 0, 1), 0, 0))]


def _apply_rope(y, tab, quarter):
    w = tab.shape[1] // 3
    cos, s_minus, s_plus = tab[:, :w], tab[:, w:2 * w], tab[:, 2 * w:]
    outs = []
    for k in range(y.shape[1] // LANES):
        yk = y[:, k * LANES:(k + 1) * LANES]
        t0 = (k * LANES) % w
        ck, mk, pk = (t[:, t0:t0 + LANES] for t in (cos, s_minus, s_plus))
        if 2 * quarter == LANES:
            outs.append(yk * ck + pltpu.roll(yk, quarter, 1) * (mk + pk))
        else:
            outs.append(yk * ck + pltpu.roll(yk, LANES - quarter, 1) * mk
                        + pltpu.roll(yk, quarter, 1) * pk)
    return jnp.concatenate(outs, axis=1)


def _proj_in_kernel(x_ref, mod_ref, g_ref, w_ref, row_ref, col_ref, o_ref, *, chunks, quarter):
    tab = _rope_tile(row_ref, col_ref)
    for b in range(x_ref.shape[0]):
        mod = mod_ref[b]
        u = (_rms(x_ref[b], g_ref[0:1, :]) * (1.0 + mod[1:2, :]) + mod[0:1, :]).astype(BF16)
        for c0, width, mult, rope in chunks:
            y = _dot(u, w_ref[:, c0:c0 + width])
            if mult != 1.0:
                y = y * mult
            if rope:
                y = _apply_rope(y, tab, quarter)
            o_ref[b, :, c0:c0 + width] = y.astype(BF16)


def _proj_in(hs, mod_l, gains, w, tabs, chunks, quarter, nctx_tiles):
    bsz, L, d = hs.shape
    n = w.shape[1]
    return pl.pallas_call(
        functools.partial(_proj_in_kernel, chunks=chunks, quarter=quarter),
        grid=(L // TM,),
        in_specs=[_tok_spec(bsz, d), _mod_spec(d, bsz, nctx_tiles), _resident(gains), _resident(w)]
        + _rope_specs(*tabs, nctx_tiles),
        out_specs=_tok_spec(bsz, n),
        out_shape=jax.ShapeDtypeStruct((bsz, L, n), BF16),
        compiler_params=_cparams("parallel"),
    )(hs, _mod_rows(mod_l, bsz), gains, w, *tabs)


def _attn_proj_kernel(x_ref, mod_ref, g_ref, w_ref, row_ref, col_ref, qt_ref, k_ref, vt_ref, *,
                      scale, quarter):
    d = x_ref.shape[2]
    tab = _rope_tile(row_ref, col_ref)
    heads = vt_ref.shape[1]
    for b in range(x_ref.shape[0]):
        mod = mod_ref[b]
        u = (_rms(x_ref[b], g_ref[0:1, :]) * (1.0 + mod[1:2, :]) + mod[0:1, :]).astype(BF16)
        qt_ref[b] = _apply_rope(_dot(u, w_ref[:, :d]) * scale, tab, quarter).T.astype(BF16)
        k_ref[b] = _apply_rope(_dot(u, w_ref[:, d:2 * d]), tab, quarter).astype(BF16)
        vt = _dot(u, w_ref[:, 2 * d:]).T.astype(BF16)
        vt_ref[b, :, 0, 0:LANES, :] = vt.reshape(heads, LANES, vt.shape[1])
        vt_ref[b, :, 0, LANES:, :] = jnp.ones((heads, VT_PAD, vt.shape[1]), BF16)


def _attn_proj(hs, mod_l, gains, w, tabs, scale, quarter, nctx_tiles):
    bsz, L, d = hs.shape
    nlat = L // TM - nctx_tiles
    return pl.pallas_call(
        functools.partial(_attn_proj_kernel, scale=scale, quarter=quarter),
        grid=(L // TM,),
        in_specs=[_tok_spec(bsz, d), _mod_spec(d, bsz, nctx_tiles), _resident(gains), _resident(w)]
        + _rope_specs(*tabs, nctx_tiles),
        out_specs=[
            pl.BlockSpec((bsz, d, TM), lambda i: (0, 0, jnp.where(i < nctx_tiles, i + nlat, i - nctx_tiles))),
            _tok_spec(bsz, d),
            pl.BlockSpec((bsz, DA_HEADS, 1, LANES + VT_PAD, TM), lambda i: (0, 0, i, 0, 0)),
        ],
        out_shape=[
            jax.ShapeDtypeStruct((bsz, d, L), BF16),
            jax.ShapeDtypeStruct((bsz, L, d), BF16),
            jax.ShapeDtypeStruct((bsz, DA_HEADS, L // TM, LANES + VT_PAD, TM), BF16),
        ],
        compiler_params=_cparams("parallel"),
    )(hs, _mod_rows(mod_l, bsz), gains, w, *tabs)


def _attn_kernel(lam_ref, sub_ref, qt_ref, k_ref, vt_ref, o_ref,
                 s_c, s_a, s_b, mb_scr, m_scr, acc_scr, *, nb, lam_init):
    ctx = s_c.shape[2]
    tk = s_a.shape[2]
    kv = vt_ref.shape[4]
    qt = qt_ref[0]
    row = lax.broadcasted_iota(jnp.int32, qt.shape, 0)
    zero = jnp.zeros_like(qt)
    qts = (jnp.where(row < LANES // 2, qt, zero), jnp.where(row >= LANES // 2, qt, zero))

    slot = {id(s_c): 0, id(s_a): 1, id(s_b): 2}

    nqb = qt.shape[1] // MXU_COLS

    def scores(kb, s_out):
        for part in range(2):
            for j in range(nqb):
                s = _dot(kb, qts[part][:, j * MXU_COLS:(j + 1) * MXU_COLS])
                s_out[part, j] = s
                mb_scr[slot[id(s_out)], part, j] = jnp.max(s, axis=0, keepdims=True)

    def stage(s_in, vt0, nxt):
        if nxt is not None:
            scores(*nxt)
        for part in range(2):
            for j in range(nqb):
                m = m_scr[part, j]
                m_new = jnp.maximum(m, mb_scr[slot[id(s_in)], part, j])
                alpha = jnp.exp2(m - m_new)
                pv = None
                for n in range(s_in.shape[2] // kv):
                    p = jnp.exp2(s_in[part, j, n * kv:(n + 1) * kv, :] - m_new).astype(BF16)
                    pv_n = _dot(vt_ref[0, 0, vt0 + n], p)
                    pv = pv_n if pv is None else pv + pv_n
                m_scr[part, j] = m_new
                acc_scr[part, j] = alpha * acc_scr[part, j] + pv

    m_scr[...] = jnp.full(m_scr.shape, -1e30, F32)
    acc_scr[...] = jnp.zeros(acc_scr.shape, F32)

    bufs = (s_b, s_a)

    def keys(t):
        return k_ref[0, pl.ds(pl.multiple_of(ctx + (t - 1) * tk, LANES), tk), :]

    def vt_index(t):
        return ctx // kv + (t - 1) * (tk // kv)

    scores(k_ref[0, 0:ctx, :], s_c)
    stage(s_c, 0, (keys(1), s_a) if nb >= 1 else None)
    n_pairs = max(0, (nb - 1) // 2 if nb % 2 else (nb - 2) // 2)

    def pair(u, carry):
        t = 2 * u + 1
        stage(s_a, vt_index(t), (keys(t + 1), s_b))
        stage(s_b, vt_index(t + 1), (keys(t + 2), s_a))
        return carry

    if n_pairs > 0:
        lax.fori_loop(0, n_pairs, pair, 0)
    for t in range(2 * n_pairs + 1, nb + 1):
        stage(bufs[t % 2], vt_index(t), (keys(t + 1), bufs[(t + 1) % 2]) if t < nb else None)

    lv = lam_ref[...]
    lam = (jnp.exp(jnp.sum(lv[0:1] * lv[1:2], axis=-1, keepdims=True))
           - jnp.exp(jnp.sum(lv[2:3] * lv[3:4], axis=-1, keepdims=True)) + lam_init)
    a1, a2 = (jnp.concatenate([acc_scr[part, j] for j in range(nqb)], axis=1) for part in range(2))
    o = (a1[:LANES] / a1[LANES:LANES + 1] - lam * (a2[:LANES] / a2[LANES:LANES + 1])).T
    o = _rms(o, sub_ref[...]) * (1.0 - lam_init)
    o_ref[0] = o.astype(BF16)


def _attention_call(q_t, k, v_t, lam_vec, subln, ctx, tk, nb, lam_init, tq, q_off, n_out):
    bsz, d, _ = q_t.shape
    kv = v_t.shape[4]
    k_rows = ctx + nb * tk
    nqb = tq // MXU_COLS
    return pl.pallas_call(
        functools.partial(_attn_kernel, nb=nb, lam_init=lam_init),
        grid=(bsz, DA_HEADS, n_out // tq),
        in_specs=[
            pl.BlockSpec(lam_vec.shape, lambda b, h, i: (0, 0)),
            pl.BlockSpec((1, LANES), lambda b, h, i: (0, 0)),
            pl.BlockSpec((1, LANES, tq), lambda b, h, i: (b, h, i + q_off // tq)),
            pl.BlockSpec((1, k_rows, LANES), lambda b, h, i: (b, 0, h)),
            pl.BlockSpec((1, 1, k_rows // kv) + v_t.shape[3:], lambda b, h, i: (b, h, 0, 0, 0)),
        ],
        out_specs=pl.BlockSpec((1, tq, LANES), lambda b, h, i: (b, i, h)),
        out_shape=jax.ShapeDtypeStruct((bsz, n_out, d), BF16),
        scratch_shapes=[pltpu.VMEM((2, nqb, ctx, MXU_COLS), F32), pltpu.VMEM((2, nqb, tk, MXU_COLS), F32),
                        pltpu.VMEM((2, nqb, tk, MXU_COLS), F32), pltpu.VMEM((3, 2, nqb, 1, MXU_COLS), F32),
                        pltpu.VMEM((2, nqb, 1, MXU_COLS), F32),
                        pltpu.VMEM((2, nqb, LANES + VT_PAD, MXU_COLS), F32)],
        compiler_params=_cparams("parallel", "parallel", "arbitrary"),
    )(lam_vec, subln.reshape(1, LANES), q_t, k, v_t)


def _attention(q_t, k, v_t, lam_vec, subln, ctx, lam_init, with_ctx):
    L = k.shape[1]
    seq = L - ctx
    tk = min(1024, seq)
    nb = seq // tk
    tq = min(ATTN_TQ, seq)
    o_lat = _attention_call(q_t, k, v_t, lam_vec, subln, ctx, tk, nb, lam_init, tq, 0, seq)
    if not with_ctx:
        return o_lat
    o_ctx = _attention_call(q_t, k, v_t, lam_vec, subln, ctx, tk, 0, lam_init, TM, seq, ctx)
    return jnp.concatenate([o_ctx, o_lat], axis=1)


def _ret_kernel(lg_ref, q_ref, k_ref, v_ref, o_ref, r_scr):
    d = pl.program_id(1)
    c = pl.program_id(2)
    C = q_ref.shape[1]
    heads, dk, dv = r_scr.shape

    @pl.when(c == 0)
    def _():
        r_scr[...] = jnp.zeros_like(r_scr)

    fwd = d == 0
    ii = lax.broadcasted_iota(jnp.int32, (C, C), 0)
    jj = lax.broadcasted_iota(jnp.int32, (C, C), 1)
    dd = jnp.where(fwd, ii - jj, jj - ii)
    causal = dd >= 0
    lag = jnp.maximum(dd, 0).astype(F32)
    idx = lax.broadcasted_iota(jnp.int32, (C, 1), 0)
    q_lag = jnp.where(fwd, idx + 1, C - idx).astype(F32)
    k_lag = jnp.where(fwd, C - 1 - idx, idx).astype(F32)

    for h in range(heads):
        lg = lg_ref[0, h:h + 1, 0:1]
        q = q_ref[0, :, h * dk:(h + 1) * dk]
        k = k_ref[0, :, h * dk:(h + 1) * dk]
        v = v_ref[0, :, h * dv:(h + 1) * dv]
        r = r_scr[h]
        s = _dot_nt(q, k) * jnp.where(causal, jnp.exp(lg * lag), 0.0)
        inner = _dot(s.astype(BF16), v)
        cross = _dot((q.astype(F32) * jnp.exp(lg * q_lag)).astype(BF16), r.astype(BF16))
        o_ref[0, 0, :, h * dv:(h + 1) * dv] = (inner + cross).astype(BF16)
        r_scr[h] = jnp.exp(lg * float(C)) * r + _dot_tn((k.astype(F32) * jnp.exp(lg * k_lag)).astype(BF16), v)


def _retention(p, log_g, ctx):
    bsz, L, n = p.shape
    d = n // 6
    C = RET_CHUNK
    nc, ncc = L // C, ctx // C

    def chunk(dr, c):
        back = jnp.where(c < ncc, ncc - 1 - c, nc + ncc - 1 - c)
        return jnp.where(dr == 0, c, back)

    lg = jnp.broadcast_to(log_g.reshape(2, RET_HEADS, 1), (2, RET_HEADS, LANES))
    return pl.pallas_call(
        _ret_kernel,
        grid=(bsz, 2, nc),
        in_specs=[
            pl.BlockSpec((1, RET_HEADS, LANES), lambda b, dr, c: (dr, 0, 0)),
            pl.BlockSpec((1, C, d), lambda b, dr, c: (b, chunk(dr, c), 0)),
            pl.BlockSpec((1, C, d), lambda b, dr, c: (b, chunk(dr, c), 1)),
            pl.BlockSpec((1, C, 2 * d), lambda b, dr, c: (b, chunk(dr, c), 1)),
        ],
        out_specs=pl.BlockSpec((1, 1, C, 2 * d), lambda b, dr, c: (dr, b, chunk(dr, c), 0)),
        out_shape=jax.ShapeDtypeStruct((2, bsz, L, 2 * d), BF16),
        scratch_shapes=[pltpu.VMEM((RET_HEADS, d // RET_HEADS, 2 * d // RET_HEADS), F32)],
        compiler_params=_cparams("parallel", "arbitrary", "arbitrary"),
    )(lg, p, p, p)


def _transpose_pieces(tiles):
    hg = SSM_GROUP
    per = len(tiles)
    piece = lax.broadcasted_iota(jnp.int32, tiles[0].shape, 1) // hg
    k = per // 2
    while k >= 1:
        keep = (piece & k) == 0
        out = list(tiles)
        for r in range(per):
            if r & k == 0:
                x, y = tiles[r], tiles[r + k]
                out[r] = jnp.where(keep, x, pltpu.roll(y, hg * k, 1))
                out[r + k] = jnp.where(keep, pltpu.roll(x, LANES - hg * k, 1), y)
        tiles = out
        k //= 2
    return tiles


def _to_group_major(u_scr, o_ref, b):
    n, hg = SSM_CHUNK, SSM_GROUP
    per = LANES // hg
    _, ntile, tm, _ = u_scr.shape
    rows = tm // n
    piece = lax.broadcasted_iota(jnp.int32, (rows, LANES), 1) // hg
    for lt in range(ntile):
        for th in range(n * hg // LANES):
            src = [u_scr[b, lt, pl.ds(th * per + tp, rows, stride=n), :] for tp in range(per)]
            for j in range(per):
                acc = None
                for tp in range(per):
                    shift = (hg * (tp - j)) % LANES
                    moved = src[tp] if shift == 0 else pltpu.roll(src[tp], shift, 1)
                    acc = moved if acc is None else jnp.where(piece == tp, moved, acc)
                o_ref[b, lt * per + j, :, th * LANES:(th + 1) * LANES] = acc.astype(o_ref.dtype)


def _from_group_major(y_ref, o_scr, b):
    n, hg = SSM_CHUNK, SSM_GROUP
    per = LANES // hg
    _, ntile, tm, _ = o_scr.shape
    rows = tm // n
    for lt in range(ntile):
        for th in range(n * hg // LANES):
            src = [y_ref[b, lt * per + j, :, th * LANES:(th + 1) * LANES] for j in range(per)]
            for tp, tile in enumerate(_transpose_pieces(src)):
                o_scr[b, lt, pl.ds(th * per + tp, rows, stride=n), :] = tile


def _normmod_kernel(x_ref, mod_ref, g_ref, o_ref, u_scr):
    for b in range(x_ref.shape[0]):
        mod = mod_ref[b]
        u = _rms(x_ref[b], g_ref[0:1, :]) * (1.0 + mod[1:2, :]) + mod[0:1, :]
        for lt in range(u_scr.shape[1]):
            u_scr[b, lt] = u[:, lt * LANES:(lt + 1) * LANES]
        _to_group_major(u_scr, o_ref, b)


def _normmod(hs, mod_l, gains, nctx_tiles):
    bsz, L, d = hs.shape
    G, rows, width = d // SSM_GROUP, TM // SSM_CHUNK, SSM_CHUNK * SSM_GROUP
    return pl.pallas_call(
        _normmod_kernel,
        grid=(L // TM,),
        in_specs=[_tok_spec(bsz, d), _mod_spec(d, bsz, nctx_tiles), _resident(gains)],
        out_specs=pl.BlockSpec((bsz, G, rows, width), lambda i: (0, 0, i, 0)),
        out_shape=jax.ShapeDtypeStruct((bsz, G, L // SSM_CHUNK, width), BF16),
        scratch_shapes=[pltpu.VMEM((bsz, d // LANES, TM, LANES), F32)],
        compiler_params=_cparams("parallel"),
    )(hs, _mod_rows(mod_l, bsz), gains)


def _s5_kernel(x_ref, t_ref, wb_ref, wc_ref, a_ref, y_ref, st_scr, *, nctx):
    gb = x_ref.shape[1]
    nch = x_ref.shape[2]
    for gi in range(gb):
        x = x_ref[0, gi]
        for dr in range(2):
            s = _dot(x, wb_ref[gi, dr])
            st_scr[2 * dr, pl.ds(gi, nch, stride=gb), :] = s[:, :LANES]
            st_scr[2 * dr + 1, pl.ds(gi, nch, stride=gb), :] = s[:, LANES:]

    a = a_ref[...]

    def step(dr, row, h_re, h_im):
        s_re = st_scr[2 * dr, pl.ds(row, gb), :]
        s_im = st_scr[2 * dr + 1, pl.ds(row, gb), :]
        st_scr[2 * dr, pl.ds(row, gb), :] = h_re
        st_scr[2 * dr + 1, pl.ds(row, gb), :] = h_im
        a_re, a_im = a[dr, 0], a[dr, 1]
        return a_re * h_re - a_im * h_im + s_re, a_re * h_im + a_im * h_re + s_im

    def body(t, carry):
        f_re, f_im, b_re, b_im = carry
        cb = jnp.where(t < nctx, nctx - 1 - t, nch + nctx - 1 - t)
        f_re, f_im = step(0, pl.multiple_of(t * gb, gb), f_re, f_im)
        b_re, b_im = step(1, pl.multiple_of(cb * gb, gb), b_re, b_im)
        return f_re, f_im, b_re, b_im

    z = jnp.zeros((gb, LANES), F32)
    lax.fori_loop(0, nch, body, (z, z, z, z))

    for gi in range(gb):
        y = _dot(x_ref[0, gi], t_ref[gi])
        for dr in range(2):
            h = jnp.concatenate([st_scr[2 * dr, pl.ds(gi, nch, stride=gb), :],
                                 st_scr[2 * dr + 1, pl.ds(gi, nch, stride=gb), :]], axis=1)
            y = y + _dot(h.astype(BF16), wc_ref[gi, dr])
        y_ref[0, gi] = y


def _s5_weights(lam_re, lam_im, log_dt, b_re, b_im, c_re, c_im):
    n = SSM_CHUNK
    lam = lax.complex(lam_re.astype(F32), lam_im.astype(F32))
    ldt = lam * jnp.exp(log_dt.astype(F32))[..., None]
    lam_bar = jnp.exp(ldt)
    b_bar = ((lam_bar - 1.0) / lam)[..., None] * lax.complex(b_re.astype(F32), b_im.astype(F32))
    c_mat = lax.complex(c_re.astype(F32), c_im.astype(F32))
    steps = jnp.arange(n + 1, dtype=F32)
    pw = jnp.exp(ldt[None] * steps[:, None, None, None])
    G, P, Hg = b_bar.shape[1:]

    kern = jnp.real(jnp.einsum('kdgp,dgjp,dgpi->dkgij', pw[:n], c_mat, b_bar))
    lag = jnp.arange(n)[None, :] - jnp.arange(n)[:, None]
    k_idx = jnp.arange(n)[:, None, None]
    pick = jnp.stack([lag[None] == k_idx, -lag[None] == k_idx]).astype(F32)
    t_mat = jnp.einsum('dkst,dkgij->gsitj', pick, kern).reshape(G, n * Hg, n * Hg)

    zpad = jnp.zeros((G, n * Hg, LANES - P), F32)
    zrow = jnp.zeros((G, LANES - P, n * Hg), F32)
    wbs, wcs = [], []
    for dr in range(2):
        e_in = pw[:n][::-1, dr] if dr == 0 else pw[:n, dr]
        wb = (e_in[:, :, :, None] * b_bar[dr][None]).transpose(1, 0, 3, 2).reshape(G, n * Hg, P)
        wbs.append(jnp.concatenate([jnp.real(wb), zpad, jnp.imag(wb), zpad], axis=2))
        e_out = pw[1:, dr] if dr == 0 else pw[1:][::-1, dr]
        wc = (c_mat[dr][None] * e_out[:, :, None, :]).transpose(1, 3, 0, 2).reshape(G, P, n * Hg)
        wcs.append(jnp.concatenate([jnp.real(wc), zrow, -jnp.imag(wc), zrow], axis=1))
    w_b = jnp.stack(wbs, axis=1)
    w_c = jnp.stack(wcs, axis=1)
    a_n = pw[n]
    pad = jnp.zeros((2, G, LANES - P), F32)
    a = jnp.stack([jnp.concatenate([jnp.real(a_n), pad], -1),
                   jnp.concatenate([jnp.imag(a_n), pad], -1)], axis=1)
    return t_mat.astype(BF16), w_b.astype(BF16), w_c.astype(BF16), a


def _s5_core(x, weights, ctx):
    bsz, G, nch, width = x.shape
    n, gb = SSM_CHUNK, SSM_GROUP_BLOCK
    t_mat, w_b, w_c, a = weights
    return pl.pallas_call(
        functools.partial(_s5_kernel, nctx=ctx // n),
        grid=(bsz, G // gb),
        in_specs=[
            pl.BlockSpec((1, gb, nch, width), lambda b, g: (b, g, 0, 0)),
            pl.BlockSpec((gb, width, width), lambda b, g: (g, 0, 0)),
            pl.BlockSpec((gb, 2, width, 2 * LANES), lambda b, g: (g, 0, 0, 0)),
            pl.BlockSpec((gb, 2, 2 * LANES, width), lambda b, g: (g, 0, 0, 0)),
            pl.BlockSpec((2, 2, gb, LANES), lambda b, g: (0, 0, g, 0)),
        ],
        out_specs=pl.BlockSpec((1, gb, nch, width), lambda b, g: (b, g, 0, 0)),
        out_shape=jax.ShapeDtypeStruct((bsz, G, nch, width), F32),
        scratch_shapes=[pltpu.VMEM((4, nch * gb, LANES), F32)],
        compiler_params=_cparams("parallel", "parallel"),
    )(x, t_mat, w_b, w_c, a)


def _finish(h, y, mod, g_ref, o_ref, b):
    o_ref[b] = h + mod[2:3, :] * _rms(y, g_ref[1:2, :])


def _attn_out_kernel(h_ref, mod_ref, g_ref, o_in_ref, w_ref, o_ref):
    for b in range(h_ref.shape[0]):
        _finish(h_ref[b], _dot(o_in_ref[b], w_ref[...]), mod_ref[b], g_ref, o_ref, b)


def _ret_out_kernel(h_ref, mod_ref, g_ref, o_in_ref, gate_ref, w_ref, o_ref):
    for b in range(h_ref.shape[0]):
        o = o_in_ref[0, b].astype(F32) + o_in_ref[1, b].astype(F32)
        dv = o.shape[1] // RET_HEADS
        parts = []
        for hd in range(RET_HEADS):
            oh = o[:, hd * dv:(hd + 1) * dv]
            parts.append(oh * lax.rsqrt(jnp.mean(oh * oh, axis=-1, keepdims=True) + NORM_EPS))
        o = jnp.concatenate(parts, axis=1)
        y = _dot((_silu(gate_ref[b].astype(F32)) * o).astype(BF16), w_ref[...])
        _finish(h_ref[b], y, mod_ref[b], g_ref, o_ref, b)


def _s5_out_kernel(h_ref, mod_ref, g_ref, y_ref, dskip_ref, w_ref, o_ref, y_scr):
    for b in range(h_ref.shape[0]):
        h = h_ref[b]
        mod = mod_ref[b]
        d = h.shape[1]
        _from_group_major(y_ref, y_scr, b)
        u = _rms(h, g_ref[0:1, :]) * (1.0 + mod[1:2, :]) + mod[0:1, :]
        y_ssm = jnp.concatenate([y_scr[b, lt] for lt in range(y_scr.shape[1])], axis=1)
        z = y_ssm + dskip_ref[...] * u
        gl = 0.5 * z * (1.0 + jnp.tanh(math.sqrt(2.0 / math.pi) * (z + 0.044715 * (z * z * z))))
        gl = gl.astype(BF16)
        y = _dot(gl, w_ref[:, :d]) * jax.nn.sigmoid(_dot(gl, w_ref[:, d:]))
        _finish(h, y, mod, g_ref, o_ref, b)


def _mixer_out(kernel, hs, mod_l, gains, nctx_tiles, extra, extra_specs, scratch=(), skip_ctx=False):
    bsz, L, d = hs.shape
    off = nctx_tiles if skip_ctx else 0
    n_tiles = L // TM - off
    return pl.pallas_call(
        kernel,
        grid=(n_tiles,),
        in_specs=[_tok_spec(bsz, d, off=off), _mod_spec(d, bsz, nctx_tiles - off), _resident(gains)]
        + extra_specs,
        out_specs=_tok_spec(bsz, d),
        out_shape=jax.ShapeDtypeStruct((bsz, n_tiles * TM, d), F32),
        scratch_shapes=list(scratch),
        compiler_params=_cparams("parallel"),
    )(hs, _mod_rows(mod_l, bsz), gains, *extra)


def _ffn_kernel(h_ref, mod_ref, g_ref, wg_ref, wu_ref, wd_ref, o_ref):
    for b in range(h_ref.shape[0]):
        h = h_ref[b]
        mod = mod_ref[b]
        v = (_rms(h, g_ref[2:3, :]) * (1.0 + mod[4:5, :]) + mod[3:4, :]).astype(BF16)
        act = (_silu(_dot(v, wg_ref[...])) * _dot(v, wu_ref[...])).astype(BF16)
        y = _dot(act, wd_ref[...])
        o_ref[b] = h + mod[5:6, :] * _rms(y, g_ref[3:4, :])


def _ffn(hs, mod_l, gains, wg, wu, wd, nctx_tiles):
    bsz, L, d = hs.shape
    return pl.pallas_call(
        _ffn_kernel,
        grid=(L // TM,),
        in_specs=[_tok_spec(bsz, d), _mod_spec(d, bsz, nctx_tiles), _resident(gains),
                  _resident(wg), _resident(wu), _resident(wd)],
        out_specs=_tok_spec(bsz, d),
        out_shape=jax.ShapeDtypeStruct((bsz, L, d), F32),
        compiler_params=_cparams("parallel"),
    )(hs, _mod_rows(mod_l, bsz), gains, wg, wu, wd)


def kernel(x, c, ctx, c_ctx, mod_w, mod_b, norm_g, attn_w_in, attn_w_out, attn_lambda, attn_subln,
           ret_w_in, ret_w_out, ret_decay_logit, ssm_lambda_re, ssm_lambda_im, ssm_log_dt,
           ssm_b_re, ssm_b_im, ssm_c_re, ssm_c_im, ssm_d, ssm_w_glu, ffn_w_gate, ffn_w_up, ffn_w_down):
    bsz, seq, d = x.shape
    nctx = ctx.shape[1]
    depth = mod_w.shape[0]
    assert nctx % TM == 0 and seq % TM == 0 and seq % GRID_W == 0
    nct = nctx // TM
    L = nctx + seq

    mods = _modulation(c, c_ctx, mod_w, mod_b)
    (attn_w_in, attn_w_out, ret_w_in, ret_w_out, ssm_w_glu, ffn_w_gate, ffn_w_up, ffn_w_down) = (
        w.astype(BF16) for w in (attn_w_in, attn_w_out, ret_w_in, ret_w_out, ssm_w_glu,
                                 ffn_w_gate, ffn_w_up, ffn_w_down))
    hs = jnp.concatenate([ctx, x], axis=1)
    da_dim = d // (2 * DA_HEADS)
    dk = d // RET_HEADS
    tab_a = _rope_tables(seq, nctx, da_dim)
    tab_r = _rope_tables(seq, nctx, dk)

    for l in range(depth):
        kind, j = l % 3, l // 3
        last = l == depth - 1
        mod_l, gains = mods[l], norm_g[l]
        if kind == 0:
            lam_init = 0.8 - 0.6 * math.exp(-0.3 * l)
            q_t, k, v_t = _attn_proj(hs, mod_l, gains, attn_w_in[j], tab_a,
                                     da_dim ** -0.5 * math.log2(math.e), da_dim // 4, nct)
            o = _attention(q_t, k, v_t, attn_lambda[j], attn_subln[j], nctx, lam_init, not last)
            w_out = attn_w_out[j]
            hs = _mixer_out(_attn_out_kernel, hs, mod_l, gains, nct, (o, w_out),
                            [_tok_spec(bsz, d), _resident(w_out)], skip_ctx=last)
            nct = 0 if last else nct
        elif kind == 1:
            chunks = ((0, d, 1.0, True), (d, d, dk ** -0.5, True)) + tuple(
                (c0, d, 1.0, False) for c0 in range(2 * d, 6 * d, d))
            p = _proj_in(hs, mod_l, gains, ret_w_in[j], tab_r, chunks, dk // 4, nct)
            log_g = jax.nn.log_sigmoid(ret_decay_logit[j].astype(F32))
            o = _retention(p, log_g, nctx)
            w_out = ret_w_out[j]
            hs = _mixer_out(_ret_out_kernel, hs, mod_l, gains, nct, (o, p, w_out),
                            [pl.BlockSpec((2, bsz, TM, 2 * d), lambda i: (0, 0, i, 0)),
                             _tok_spec(bsz, 2 * d, 2), _resident(w_out)])
        else:
            u = _normmod(hs, mod_l, gains, nct)
            weights = _s5_weights(ssm_lambda_re[j], ssm_lambda_im[j], ssm_log_dt[j], ssm_b_re[j],
                                  ssm_b_im[j], ssm_c_re[j], ssm_c_im[j])
            y = _s5_core(u, weights, nctx)
            w_glu = ssm_w_glu[j]
            dskip = ssm_d[j].astype(F32).reshape(1, d)
            y_spec = pl.BlockSpec(y.shape[0:2] + (TM // SSM_CHUNK,) + y.shape[3:], lambda i: (0, 0, i, 0))
            hs = _mixer_out(_s5_out_kernel, hs, mod_l, gains, nct, (y, dskip, w_glu),
                            [y_spec, _resident(dskip), _resident(w_glu)],
                            scratch=[pltpu.VMEM((bsz, d // LANES, TM, LANES), F32)])
        hs = _ffn(hs, mod_l, gains, ffn_w_gate[l], ffn_w_up[l], ffn_w_down[l], nct)
    return hs if hs.shape[1] == seq else hs[:, nctx:, :]
```

```python
import functools
import math

import jax
import jax.numpy as jnp
from jax import lax
from jax.experimental import pallas as pl
from jax.experimental.pallas import tpu as pltpu

F32 = jnp.float32
BF16 = jnp.bfloat16

GRID_W = 64
DA_HEADS = 8
RET_HEADS = 4
RET_CHUNK = 256
SSM_GROUP = 16
SSM_STATE = 64
SSM_CHUNK = 16
SSM_GROUP_BLOCK = 8
ROPE_THETA = 10000.0
NORM_EPS = 1e-6
LANES = 128
MXU_COLS = 256
TM = 256
ATTN_TQ = 1024
VT_PAD = 16
VMEM_LIMIT = 56 * 1024 * 1024


def _cparams(*sem):
    return pltpu.CompilerParams(dimension_semantics=sem, vmem_limit_bytes=VMEM_LIMIT)


def _rms(x, gain):
    return x * lax.rsqrt(jnp.mean(x * x, axis=-1, keepdims=True) + NORM_EPS) * gain


def _silu(x):
    return x * jax.nn.sigmoid(x)


def _dot(a, b):
    return jnp.dot(a, b, preferred_element_type=F32)


def _dot_nt(a, b):
    return lax.dot_general(a, b, (((1,), (1,)), ((), ())), preferred_element_type=F32)


def _dot_tn(a, b):
    return lax.dot_general(a, b, (((0,), (0,)), ((), ())), preferred_element_type=F32)


def _mod_kernel(s_ref, w_ref, b_ref, o_ref):
    s = _silu(s_ref[...])
    o_ref[0] = jnp.dot(s, w_ref[0], preferred_element_type=F32,
                       precision=lax.Precision.HIGHEST) + b_ref[0]


def _modulation(c, c_ctx, mod_w, mod_b):
    depth, d, n = mod_w.shape
    bsz = c.shape[0]
    rows = 8
    s = jnp.zeros((rows, d), F32).at[:bsz].set(c).at[bsz].set(c_ctx)
    tn = 3072
    out = pl.pallas_call(
        _mod_kernel,
        grid=(depth, n // tn),
        in_specs=[
            pl.BlockSpec((rows, d), lambda l, j: (0, 0)),
            pl.BlockSpec((1, d, tn), lambda l, j: (l, 0, j)),
            pl.BlockSpec((1, 1, tn), lambda l, j: (l, 0, j)),
        ],
        out_specs=pl.BlockSpec((1, rows, tn), lambda l, j: (l, 0, j)),
        out_shape=jax.ShapeDtypeStruct((depth, rows, n), F32),
        compiler_params=_cparams("parallel", "parallel"),
    )(s, mod_w, mod_b.reshape(depth, 1, n))
    return out.reshape(depth, rows, 6, d)


def _mod_rows(mod_l, bsz):
    return jnp.concatenate([mod_l[:bsz], jnp.broadcast_to(mod_l[bsz], (bsz,) + mod_l.shape[1:])])


def _mod_spec(d, bsz, nctx_tiles):
    return pl.BlockSpec((bsz, 6, d), lambda i: (jnp.where(i < nctx_tiles, 1, 0), 0, 0))


def _tok_spec(bsz, width, col=0, off=0):
    return pl.BlockSpec((bsz, TM, width), lambda i: (0, i + off, col))


def _resident(a, layer=None):
    nd = a.ndim
    if layer is None:
        return pl.BlockSpec(a.shape, lambda i: (0,) * nd, pipeline_mode=pl.Buffered(1))
    return pl.BlockSpec((None,) + a.shape[1:], lambda i: (layer,) + (0,) * (nd - 1),
                        pipeline_mode=pl.Buffered(1))


def _rope_tables(seq, ctx, head_dim):
    q = head_dim // 4
    freqs = ROPE_THETA ** (-jnp.arange(q, dtype=F32) / q)
    rep = max(1, LANES // head_dim)

    def table(pos, first_seg, identity):
        ang = pos[:, None] * freqs
        cos = jnp.ones_like(ang) if identity else jnp.cos(ang)
        sin = jnp.zeros_like(ang) if identity else jnp.sin(ang)
        z = jnp.zeros_like(ang)
        segs = lambda a, b: jnp.concatenate([z] * first_seg + [a, b] + [z] * (2 - first_seg), axis=1)
        parts = (segs(cos, cos), segs(-sin, z), segs(z, sin))
        return jnp.concatenate([jnp.tile(p, (1, rep)) for p in parts], axis=1)

    per_tile = TM // GRID_W
    rows = jnp.arange(seq // GRID_W, dtype=F32)
    cols = jnp.arange(GRID_W, dtype=F32)
    row_lat = table(rows, 0, False).reshape(seq // TM, per_tile, -1)
    row_ctx = table(jnp.zeros((ctx // GRID_W,), F32), 0, True).reshape(ctx // TM, per_tile, -1)
    row_tab = jnp.concatenate([row_ctx, row_lat], axis=0)
    col_tab = jnp.stack([table(cols, 2, True), table(cols, 2, False)], axis=0)
    return row_tab, col_tab


def _rope_tile(row_ref, col_ref):
    col = col_ref[0]
    return jnp.concatenate([row_ref[0, a:a + 1, :] + col for a in range(row_ref.shape[1])], axis=0)


def _rope_specs(row_tab, col_tab, nctx_tiles):
    return [pl.BlockSpec((1,) + row_tab.shape[1:], lambda i: (i, 0, 0)),
            pl.BlockSpec((1,) + col_tab.shape[1:], lambda i: (jnp.where(i < nctx_tiles, 0, 1), 0, 0))]


def _apply_rope(y, tab, quarter):
    w = tab.shape[1] // 3
    cos, s_minus, s_plus = tab[:, :w], tab[:, w:2 * w], tab[:, 2 * w:]
    outs = []
    for k in range(y.shape[1] // LANES):
        yk = y[:, k * LANES:(k + 1) * LANES]
        t0 = (k * LANES) % w
        ck, mk, pk = (t[:, t0:t0 + LANES] for t in (cos, s_minus, s_plus))
        if 2 * quarter == LANES:
            outs.append(yk * ck + pltpu.roll(yk, quarter, 1) * (mk + pk))
        else:
            outs.append(yk * ck + pltpu.roll(yk, LANES - quarter, 1) * mk
                        + pltpu.roll(yk, quarter, 1) * pk)
    return jnp.concatenate(outs, axis=1)


def _proj_in_kernel(x_ref, mod_ref, g_ref, w_ref, row_ref, col_ref, o_ref, *, chunks, quarter):
    tab = _rope_tile(row_ref, col_ref)
    for b in range(x_ref.shape[0]):
        mod = mod_ref[b]
        u = (_rms(x_ref[b], g_ref[0:1, :]) * (1.0 + mod[1:2, :]) + mod[0:1, :]).astype(BF16)
        for c0, width, mult, rope in chunks:
            y = _dot(u, w_ref[:, c0:c0 + width])
            if mult != 1.0:
                y = y * mult
            if rope:
                y = _apply_rope(y, tab, quarter)
            o_ref[b, :, c0:c0 + width] = y.astype(BF16)


def _proj_in(hs, mod_l, gains, w, layer, tabs, chunks, quarter, nctx_tiles):
    bsz, L, d = hs.shape
    n = w.shape[2]
    return pl.pallas_call(
        functools.partial(_proj_in_kernel, chunks=chunks, quarter=quarter),
        grid=(L // TM,),
        in_specs=[_tok_spec(bsz, d), _mod_spec(d, bsz, nctx_tiles), _resident(gains), _resident(w, layer)]
        + _rope_specs(*tabs, nctx_tiles),
        out_specs=_tok_spec(bsz, n),
        out_shape=jax.ShapeDtypeStruct((bsz, L, n), BF16),
        compiler_params=_cparams("parallel"),
    )(hs, _mod_rows(mod_l, bsz), gains, w, *tabs)


def _attn_proj_kernel(x_ref, mod_ref, g_ref, w_ref, row_ref, col_ref, qt_ref, k_ref, vt_ref, *,
                      scale, quarter):
    d = x_ref.shape[2]
    tab = _rope_tile(row_ref, col_ref)
    heads = vt_ref.shape[1]
    for b in range(x_ref.shape[0]):
        mod = mod_ref[b]
        u = (_rms(x_ref[b], g_ref[0:1, :]) * (1.0 + mod[1:2, :]) + mod[0:1, :]).astype(BF16)
        qt_ref[b] = _apply_rope(_dot(u, w_ref[:, :d]) * scale, tab, quarter).T.astype(BF16)
        k_ref[b] = _apply_rope(_dot(u, w_ref[:, d:2 * d]), tab, quarter).astype(BF16)
        vt = _dot(u, w_ref[:, 2 * d:]).T.astype(BF16)
        vt_ref[b, :, 0, 0:LANES, :] = vt.reshape(heads, LANES, vt.shape[1])
        vt_ref[b, :, 0, LANES:, :] = jnp.ones((heads, VT_PAD, vt.shape[1]), BF16)


def _attn_proj(hs, mod_l, gains, w, layer, tabs, scale, quarter, nctx_tiles):
    bsz, L, d = hs.shape
    nlat = L // TM - nctx_tiles
    return pl.pallas_call(
        functools.partial(_attn_proj_kernel, scale=scale, quarter=quarter),
        grid=(L // TM,),
        in_specs=[_tok_spec(bsz, d), _mod_spec(d, bsz, nctx_tiles), _resident(gains), _resident(w, layer)]
        + _rope_specs(*tabs, nctx_tiles),
        out_specs=[
            pl.BlockSpec((bsz, d, TM), lambda i: (0, 0, jnp.where(i < nctx_tiles, i + nlat, i - nctx_tiles))),
            _tok_spec(bsz, d),
            pl.BlockSpec((bsz, DA_HEADS, 1, LANES + VT_PAD, TM), lambda i: (0, 0, i, 0, 0)),
        ],
        out_shape=[
            jax.ShapeDtypeStruct((bsz, d, L), BF16),
            jax.ShapeDtypeStruct((bsz, L, d), BF16),
            jax.ShapeDtypeStruct((bsz, DA_HEADS, L // TM, LANES + VT_PAD, TM), BF16),
        ],
        compiler_params=_cparams("parallel"),
    )(hs, _mod_rows(mod_l, bsz), gains, w, *tabs)


def _attn_kernel(lam_ref, sub_ref, qt_ref, k_ref, vt_ref, o_ref,
                 s_c, s_a, s_b, mb_scr, m_scr, acc_scr, *, nb, lam_init):
    ctx = s_c.shape[2]
    tk = s_a.shape[2]
    kv = vt_ref.shape[4]
    qt = qt_ref[0]
    row = lax.broadcasted_iota(jnp.int32, qt.shape, 0)
    zero = jnp.zeros_like(qt)
    qts = (jnp.where(row < LANES // 2, qt, zero), jnp.where(row >= LANES // 2, qt, zero))

    slot = {id(s_c): 0, id(s_a): 1, id(s_b): 2}

    nqb = qt.shape[1] // MXU_COLS

    def scores(kb, s_out):
        for part in range(2):
            for j in range(nqb):
                s = _dot(kb, qts[part][:, j * MXU_COLS:(j + 1) * MXU_COLS])
                s_out[part, j] = s
                mb_scr[slot[id(s_out)], part, j] = jnp.max(s, axis=0, keepdims=True)

    def stage(s_in, vt0, nxt):
        if nxt is not None:
            scores(*nxt)
        for part in range(2):
            for j in range(nqb):
                m = m_scr[part, j]
                m_new = jnp.maximum(m, mb_scr[slot[id(s_in)], part, j])
                alpha = jnp.exp2(m - m_new)
                pv = None
                for n in range(s_in.shape[2] // kv):
                    p = jnp.exp2(s_in[part, j, n * kv:(n + 1) * kv, :] - m_new).astype(BF16)
                    pv_n = _dot(vt_ref[0, 0, vt0 + n], p)
                    pv = pv_n if pv is None else pv + pv_n
                m_scr[part, j] = m_new
                acc_scr[part, j] = alpha * acc_scr[part, j] + pv

    m_scr[...] = jnp.full(m_scr.shape, -1e30, F32)
    acc_scr[...] = jnp.zeros(acc_scr.shape, F32)

    bufs = (s_b, s_a)

    def keys(t):
        return k_ref[0, pl.ds(pl.multiple_of(ctx + (t - 1) * tk, LANES), tk), :]

    def vt_index(t):
        return ctx // kv + (t - 1) * (tk // kv)

    scores(k_ref[0, 0:ctx, :], s_c)
    stage(s_c, 0, (keys(1), s_a) if nb >= 1 else None)
    n_pairs = max(0, (nb - 1) // 2 if nb % 2 else (nb - 2) // 2)

    def pair(u, carry):
        t = 2 * u + 1
        stage(s_a, vt_index(t), (keys(t + 1), s_b))
        stage(s_b, vt_index(t + 1), (keys(t + 2), s_a))
        return carry

    if n_pairs > 0:
        lax.fori_loop(0, n_pairs, pair, 0)
    for t in range(2 * n_pairs + 1, nb + 1):
        stage(bufs[t % 2], vt_index(t), (keys(t + 1), bufs[(t + 1) % 2]) if t < nb else None)

    lv = lam_ref[...]
    lam = (jnp.exp(jnp.sum(lv[0:1] * lv[1:2], axis=-1, keepdims=True))
           - jnp.exp(jnp.sum(lv[2:3] * lv[3:4], axis=-1, keepdims=True)) + lam_init)
    a1, a2 = (jnp.concatenate([acc_scr[part, j] for j in range(nqb)], axis=1) for part in range(2))
    o = (a1[:LANES] / a1[LANES:LANES + 1] - lam * (a2[:LANES] / a2[LANES:LANES + 1])).T
    o = _rms(o, sub_ref[...]) * (1.0 - lam_init)
    o_ref[0] = o.astype(BF16)


def _attention_call(q_t, k, v_t, lam_vec, subln, ctx, tk, nb, lam_init, tq, q_off, n_out):
    bsz, d, _ = q_t.shape
    kv = v_t.shape[4]
    k_rows = ctx + nb * tk
    nqb = tq // MXU_COLS
    return pl.pallas_call(
        functools.partial(_attn_kernel, nb=nb, lam_init=lam_init),
        grid=(bsz, DA_HEADS, n_out // tq),
        in_specs=[
            pl.BlockSpec(lam_vec.shape, lambda b, h, i: (0, 0)),
            pl.BlockSpec((1, LANES), lambda b, h, i: (0, 0)),
            pl.BlockSpec((1, LANES, tq), lambda b, h, i: (b, h, i + q_off // tq)),
            pl.BlockSpec((1, k_rows, LANES), lambda b, h, i: (b, 0, h)),
            pl.BlockSpec((1, 1, k_rows // kv) + v_t.shape[3:], lambda b, h, i: (b, h, 0, 0, 0)),
        ],
        out_specs=pl.BlockSpec((1, tq, LANES), lambda b, h, i: (b, i, h)),
        out_shape=jax.ShapeDtypeStruct((bsz, n_out, d), BF16),
        scratch_shapes=[pltpu.VMEM((2, nqb, ctx, MXU_COLS), F32), pltpu.VMEM((2, nqb, tk, MXU_COLS), F32),
                        pltpu.VMEM((2, nqb, tk, MXU_COLS), F32), pltpu.VMEM((3, 2, nqb, 1, MXU_COLS), F32),
                        pltpu.VMEM((2, nqb, 1, MXU_COLS), F32),
                        pltpu.VMEM((2, nqb, LANES + VT_PAD, MXU_COLS), F32)],
        compiler_params=_cparams("parallel", "parallel", "arbitrary"),
    )(lam_vec, subln.reshape(1, LANES), q_t, k, v_t)


def _attention(q_t, k, v_t, lam_vec, subln, ctx, lam_init, with_ctx):
    L = k.shape[1]
    seq = L - ctx
    tk = min(1024, seq)
    nb = seq // tk
    tq = min(ATTN_TQ, seq)
    o_lat = _attention_call(q_t, k, v_t, lam_vec, subln, ctx, tk, nb, lam_init, tq, 0, seq)
    if not with_ctx:
        return o_lat
    o_ctx = _attention_call(q_t, k, v_t, lam_vec, subln, ctx, tk, 0, lam_init, TM, seq, ctx)
    return jnp.concatenate([o_ctx, o_lat], axis=1)


def _ret_kernel(lg_ref, q_ref, k_ref, v_ref, o_ref, r_scr):
    d = pl.program_id(1)
    c = pl.program_id(2)
    C = q_ref.shape[1]
    heads, dk, dv = r_scr.shape

    @pl.when(c == 0)
    def _():
        r_scr[...] = jnp.zeros_like(r_scr)

    fwd = d == 0
    ii = lax.broadcasted_iota(jnp.int32, (C, C), 0)
    jj = lax.broadcasted_iota(jnp.int32, (C, C), 1)
    dd = jnp.where(fwd, ii - jj, jj - ii)
    causal = dd >= 0
    lag = jnp.maximum(dd, 0).astype(F32)
    idx = lax.broadcasted_iota(jnp.int32, (C, 1), 0)
    q_lag = jnp.where(fwd, idx + 1, C - idx).astype(F32)
    k_lag = jnp.where(fwd, C - 1 - idx, idx).astype(F32)

    for h in range(heads):
        lg = lg_ref[0, h:h + 1, 0:1]
        q = q_ref[0, :, h * dk:(h + 1) * dk]
        k = k_ref[0, :, h * dk:(h + 1) * dk]
        v = v_ref[0, :, h * dv:(h + 1) * dv]
        r = r_scr[h]
        s = _dot_nt(q, k) * jnp.where(causal, jnp.exp(lg * lag), 0.0)
        inner = _dot(s.astype(BF16), v)
        cross = _dot((q.astype(F32) * jnp.exp(lg * q_lag)).astype(BF16), r.astype(BF16))
        o_ref[0, 0, :, h * dv:(h + 1) * dv] = (inner + cross).astype(BF16)
        r_scr[h] = jnp.exp(lg * float(C)) * r + _dot_tn((k.astype(F32) * jnp.exp(lg * k_lag)).astype(BF16), v)


def _retention(p, log_g, ctx):
    bsz, L, n = p.shape
    d = n // 6
    C = RET_CHUNK
    nc, ncc = L // C, ctx // C

    def chunk(dr, c):
        back = jnp.where(c < ncc, ncc - 1 - c, nc + ncc - 1 - c)
        return jnp.where(dr == 0, c, back)

    lg = jnp.broadcast_to(log_g.reshape(2, RET_HEADS, 1), (2, RET_HEADS, LANES))
    return pl.pallas_call(
        _ret_kernel,
        grid=(bsz, 2, nc),
        in_specs=[
            pl.BlockSpec((1, RET_HEADS, LANES), lambda b, dr, c: (dr, 0, 0)),
            pl.BlockSpec((1, C, d), lambda b, dr, c: (b, chunk(dr, c), 0)),
            pl.BlockSpec((1, C, d), lambda b, dr, c: (b, chunk(dr, c), 1)),
            pl.BlockSpec((1, C, 2 * d), lambda b, dr, c: (b, chunk(dr, c), 1)),
        ],
        out_specs=pl.BlockSpec((1, 1, C, 2 * d), lambda b, dr, c: (dr, b, chunk(dr, c), 0)),
        out_shape=jax.ShapeDtypeStruct((2, bsz, L, 2 * d), BF16),
        scratch_shapes=[pltpu.VMEM((RET_HEADS, d // RET_HEADS, 2 * d // RET_HEADS), F32)],
        compiler_params=_cparams("parallel", "arbitrary", "arbitrary"),
    )(lg, p, p, p)


def _transpose_pieces(tiles):
    hg = SSM_GROUP
    per = len(tiles)
    piece = lax.broadcasted_iota(jnp.int32, tiles[0].shape, 1) // hg
    k = per // 2
    while k >= 1:
        keep = (piece & k) == 0
        out = list(tiles)
        for r in range(per):
            if r & k == 0:
                x, y = tiles[r], tiles[r + k]
                out[r] = jnp.where(keep, x, pltpu.roll(y, hg * k, 1))
                out[r + k] = jnp.where(keep, pltpu.roll(x, LANES - hg * k, 1), y)
        tiles = out
        k //= 2
    return tiles


def _to_group_major(u_scr, o_ref, b):
    n, hg = SSM_CHUNK, SSM_GROUP
    per = LANES // hg
    _, ntile, tm, _ = u_scr.shape
    rows = tm // n
    piece = lax.broadcasted_iota(jnp.int32, (rows, LANES), 1) // hg
    for lt in range(ntile):
        for th in range(n * hg // LANES):
            src = [u_scr[b, lt, pl.ds(th * per + tp, rows, stride=n), :] for tp in range(per)]
            for j in range(per):
                acc = None
                for tp in range(per):
                    shift = (hg * (tp - j)) % LANES
                    moved = src[tp] if shift == 0 else pltpu.roll(src[tp], shift, 1)
                    acc = moved if acc is None else jnp.where(piece == tp, moved, acc)
                o_ref[b, lt * per + j, :, th * LANES:(th + 1) * LANES] = acc.astype(o_ref.dtype)


def _from_group_major(y_ref, o_scr, b):
    n, hg = SSM_CHUNK, SSM_GROUP
    per = LANES // hg
    _, ntile, tm, _ = o_scr.shape
    rows = tm // n
    for lt in range(ntile):
        for th in range(n * hg // LANES):
            src = [y_ref[b, lt * per + j, :, th * LANES:(th + 1) * LANES] for j in range(per)]
            for tp, tile in enumerate(_transpose_pieces(src)):
                o_scr[b, lt, pl.ds(th * per + tp, rows, stride=n), :] = tile


def _normmod_kernel(x_ref, mod_ref, g_ref, o_ref, u_scr):
    for b in range(x_ref.shape[0]):
        mod = mod_ref[b]
        u = _rms(x_ref[b], g_ref[0:1, :]) * (1.0 + mod[1:2, :]) + mod[0:1, :]
        for lt in range(u_scr.shape[1]):
            u_scr[b, lt] = u[:, lt * LANES:(lt + 1) * LANES]
        _to_group_major(u_scr, o_ref, b)


def _normmod(hs, mod_l, gains, nctx_tiles):
    bsz, L, d = hs.shape
    G, rows, width = d // SSM_GROUP, TM // SSM_CHUNK, SSM_CHUNK * SSM_GROUP
    return pl.pallas_call(
        _normmod_kernel,
        grid=(L // TM,),
        in_specs=[_tok_spec(bsz, d), _mod_spec(d, bsz, nctx_tiles), _resident(gains)],
        out_specs=pl.BlockSpec((bsz, G, rows, width), lambda i: (0, 0, i, 0)),
        out_shape=jax.ShapeDtypeStruct((bsz, G, L // SSM_CHUNK, width), BF16),
        scratch_shapes=[pltpu.VMEM((bsz, d // LANES, TM, LANES), F32)],
        compiler_params=_cparams("parallel"),
    )(hs, _mod_rows(mod_l, bsz), gains)


def _s5_kernel(x_ref, t_ref, wb_ref, wc_ref, a_ref, y_ref, st_scr, *, nctx):
    gb = x_ref.shape[1]
    nch = x_ref.shape[2]
    for gi in range(gb):
        x = x_ref[0, gi]
        for dr in range(2):
            s = _dot(x, wb_ref[gi, dr])
            st_scr[2 * dr, pl.ds(gi, nch, stride=gb), :] = s[:, :LANES]
            st_scr[2 * dr + 1, pl.ds(gi, nch, stride=gb), :] = s[:, LANES:]

    a = a_ref[...]

    def step(dr, row, h_re, h_im):
        s_re = st_scr[2 * dr, pl.ds(row, gb), :]
        s_im = st_scr[2 * dr + 1, pl.ds(row, gb), :]
        st_scr[2 * dr, pl.ds(row, gb), :] = h_re
        st_scr[2 * dr + 1, pl.ds(row, gb), :] = h_im
        a_re, a_im = a[dr, 0], a[dr, 1]
        return a_re * h_re - a_im * h_im + s_re, a_re * h_im + a_im * h_re + s_im

    def body(t, carry):
        f_re, f_im, b_re, b_im = carry
        cb = jnp.where(t < nctx, nctx - 1 - t, nch + nctx - 1 - t)
        f_re, f_im = step(0, pl.multiple_of(t * gb, gb), f_re, f_im)
        b_re, b_im = step(1, pl.multiple_of(cb * gb, gb), b_re, b_im)
        return f_re, f_im, b_re, b_im

    z = jnp.zeros((gb, LANES), F32)
    lax.fori_loop(0, nch, body, (z, z, z, z))

    for gi in range(gb):
        y = _dot(x_ref[0, gi], t_ref[gi])
        for dr in range(2):
            h = jnp.concatenate([st_scr[2 * dr, pl.ds(gi, nch, stride=gb), :],
                                 st_scr[2 * dr + 1, pl.ds(gi, nch, stride=gb), :]], axis=1)
            y = y + _dot(h.astype(BF16), wc_ref[gi, dr])
        y_ref[0, gi] = y


def _s5_weights(lam_re, lam_im, log_dt, b_re, b_im, c_re, c_im):
    n = SSM_CHUNK
    lam = lax.complex(lam_re.astype(F32), lam_im.astype(F32))
    ldt = lam * jnp.exp(log_dt.astype(F32))[..., None]
    lam_bar = jnp.exp(ldt)
    b_bar = ((lam_bar - 1.0) / lam)[..., None] * lax.complex(b_re.astype(F32), b_im.astype(F32))
    c_mat = lax.complex(c_re.astype(F32), c_im.astype(F32))
    steps = jnp.arange(n + 1, dtype=F32)
    pw = jnp.exp(ldt[None] * steps[:, None, None, None])
    G, P, Hg = b_bar.shape[1:]

    kern = jnp.real(jnp.einsum('kdgp,dgjp,dgpi->dkgij', pw[:n], c_mat, b_bar))
    lag = jnp.arange(n)[None, :] - jnp.arange(n)[:, None]
    k_idx = jnp.arange(n)[:, None, None]
    pick = jnp.stack([lag[None] == k_idx, -lag[None] == k_idx]).astype(F32)
    t_mat = jnp.einsum('dkst,dkgij->gsitj', pick, kern).reshape(G, n * Hg, n * Hg)

    zpad = jnp.zeros((G, n * Hg, LANES - P), F32)
    zrow = jnp.zeros((G, LANES - P, n * Hg), F32)
    wbs, wcs = [], []
    for dr in range(2):
        e_in = pw[:n][::-1, dr] if dr == 0 else pw[:n, dr]
        wb = (e_in[:, :, :, None] * b_bar[dr][None]).transpose(1, 0, 3, 2).reshape(G, n * Hg, P)
        wbs.append(jnp.concatenate([jnp.real(wb), zpad, jnp.imag(wb), zpad], axis=2))
        e_out = pw[1:, dr] if dr == 0 else pw[1:][::-1, dr]
        wc = (c_mat[dr][None] * e_out[:, :, None, :]).transpose(1, 3, 0, 2).reshape(G, P, n * Hg)
        wcs.append(jnp.concatenate([jnp.real(wc), zrow, -jnp.imag(wc), zrow], axis=1))
    w_b = jnp.stack(wbs, axis=1)
    w_c = jnp.stack(wcs, axis=1)
    a_n = pw[n]
    pad = jnp.zeros((2, G, LANES - P), F32)
    a = jnp.stack([jnp.concatenate([jnp.real(a_n), pad], -1),
                   jnp.concatenate([jnp.imag(a_n), pad], -1)], axis=1)
    return t_mat.astype(BF16), w_b.astype(BF16), w_c.astype(BF16), a


def _s5_core(x, weights, ctx):
    bsz, G, nch, width = x.shape
    n, gb = SSM_CHUNK, SSM_GROUP_BLOCK
    t_mat, w_b, w_c, a = weights
    return pl.pallas_call(
        functools.partial(_s5_kernel, nctx=ctx // n),
        grid=(bsz, G // gb),
        in_specs=[
            pl.BlockSpec((1, gb, nch, width), lambda b, g: (b, g, 0, 0)),
            pl.BlockSpec((gb, width, width), lambda b, g: (g, 0, 0)),
            pl.BlockSpec((gb, 2, width, 2 * LANES), lambda b, g: (g, 0, 0, 0)),
            pl.BlockSpec((gb, 2, 2 * LANES, width), lambda b, g: (g, 0, 0, 0)),
            pl.BlockSpec((2, 2, gb, LANES), lambda b, g: (0, 0, g, 0)),
        ],
        out_specs=pl.BlockSpec((1, gb, nch, width), lambda b, g: (b, g, 0, 0)),
        out_shape=jax.ShapeDtypeStruct((bsz, G, nch, width), F32),
        scratch_shapes=[pltpu.VMEM((4, nch * gb, LANES), F32)],
        compiler_params=_cparams("parallel", "parallel"),
    )(x, t_mat, w_b, w_c, a)


def _finish(h, y, mod, g_ref, o_ref, b):
    o_ref[b] = h + mod[2:3, :] * _rms(y, g_ref[1:2, :])


def _attn_out_kernel(h_ref, mod_ref, g_ref, o_in_ref, w_ref, o_ref):
    for b in range(h_ref.shape[0]):
        _finish(h_ref[b], _dot(o_in_ref[b], w_ref[...]), mod_ref[b], g_ref, o_ref, b)


def _ret_out_kernel(h_ref, mod_ref, g_ref, o_in_ref, gate_ref, w_ref, o_ref):
    for b in range(h_ref.shape[0]):
        o = o_in_ref[0, b].astype(F32) + o_in_ref[1, b].astype(F32)
        dv = o.shape[1] // RET_HEADS
        parts = []
        for hd in range(RET_HEADS):
            oh = o[:, hd * dv:(hd + 1) * dv]
            parts.append(oh * lax.rsqrt(jnp.mean(oh * oh, axis=-1, keepdims=True) + NORM_EPS))
        o = jnp.concatenate(parts, axis=1)
        y = _dot((_silu(gate_ref[b].astype(F32)) * o).astype(BF16), w_ref[...])
        _finish(h_ref[b], y, mod_ref[b], g_ref, o_ref, b)


def _s5_out_kernel(h_ref, mod_ref, g_ref, y_ref, dskip_ref, w_ref, o_ref, y_scr):
    for b in range(h_ref.shape[0]):
        h = h_ref[b]
        mod = mod_ref[b]
        d = h.shape[1]
        _from_group_major(y_ref, y_scr, b)
        u = _rms(h, g_ref[0:1, :]) * (1.0 + mod[1:2, :]) + mod[0:1, :]
        y_ssm = jnp.concatenate([y_scr[b, lt] for lt in range(y_scr.shape[1])], axis=1)
        z = y_ssm + dskip_ref[...] * u
        gl = 0.5 * z * (1.0 + jnp.tanh(math.sqrt(2.0 / math.pi) * (z + 0.044715 * (z * z * z))))
        gl = gl.astype(BF16)
        y = _dot(gl, w_ref[:, :d]) * jax.nn.sigmoid(_dot(gl, w_ref[:, d:]))
        _finish(h, y, mod, g_ref, o_ref, b)


def _mixer_out(kernel, hs, mod_l, gains, nctx_tiles, extra, extra_specs, scratch=(), skip_ctx=False):
    bsz, L, d = hs.shape
    off = nctx_tiles if skip_ctx else 0
    n_tiles = L // TM - off
    return pl.pallas_call(
        kernel,
        grid=(n_tiles,),
        in_specs=[_tok_spec(bsz, d, off=off), _mod_spec(d, bsz, nctx_tiles - off), _resident(gains)]
        + extra_specs,
        out_specs=_tok_spec(bsz, d),
        out_shape=jax.ShapeDtypeStruct((bsz, n_tiles * TM, d), F32),
        scratch_shapes=list(scratch),
        compiler_params=_cparams("parallel"),
    )(hs, _mod_rows(mod_l, bsz), gains, *extra)


def _ffn_kernel(h_ref, mod_ref, g_ref, wg_ref, wu_ref, wd_ref, o_ref):
    for b in range(h_ref.shape[0]):
        h = h_ref[b]
        mod = mod_ref[b]
        v = (_rms(h, g_ref[2:3, :]) * (1.0 + mod[4:5, :]) + mod[3:4, :]).astype(BF16)
        act = (_silu(_dot(v, wg_ref[...])) * _dot(v, wu_ref[...])).astype(BF16)
        y = _dot(act, wd_ref[...])
        o_ref[b] = h + mod[5:6, :] * _rms(y, g_ref[3:4, :])


def _ffn(hs, mod_l, gains, wg, wu, wd, layer, nctx_tiles):
    bsz, L, d = hs.shape
    return pl.pallas_call(
        _ffn_kernel,
        grid=(L // TM,),
        in_specs=[_tok_spec(bsz, d), _mod_spec(d, bsz, nctx_tiles), _resident(gains),
                  _resident(wg, layer), _resident(wu, layer), _resident(wd, layer)],
        out_specs=_tok_spec(bsz, d),
        out_shape=jax.ShapeDtypeStruct((bsz, L, d), F32),
        compiler_params=_cparams("parallel"),
    )(hs, _mod_rows(mod_l, bsz), gains, wg, wu, wd)


def kernel(x, c, ctx, c_ctx, mod_w, mod_b, norm_g, attn_w_in, attn_w_out, attn_lambda, attn_subln,
           ret_w_in, ret_w_out, ret_decay_logit, ssm_lambda_re, ssm_lambda_im, ssm_log_dt,
           ssm_b_re, ssm_b_im, ssm_c_re, ssm_c_im, ssm_d, ssm_w_glu, ffn_w_gate, ffn_w_up, ffn_w_down):
    bsz, seq, d = x.shape
    nctx = ctx.shape[1]
    depth = mod_w.shape[0]
    assert nctx % TM == 0 and seq % TM == 0 and seq % GRID_W == 0
    nct = nctx // TM
    L = nctx + seq

    mods = _modulation(c, c_ctx, mod_w, mod_b)
    (attn_w_in, attn_w_out, ret_w_in, ret_w_out, ssm_w_glu, ffn_w_gate, ffn_w_up, ffn_w_down) = (
        w.astype(BF16) for w in (attn_w_in, attn_w_out, ret_w_in, ret_w_out, ssm_w_glu,
                                 ffn_w_gate, ffn_w_up, ffn_w_down))
    hs = jnp.concatenate([ctx, x], axis=1)
    da_dim = d // (2 * DA_HEADS)
    dk = d // RET_HEADS
    tab_a = _rope_tables(seq, nctx, da_dim)
    tab_r = _rope_tables(seq, nctx, dk)

    for l in range(depth):
        kind, j = l % 3, l // 3
        last = l == depth - 1
        mod_l, gains = mods[l], norm_g[l]
        if kind == 0:
            lam_init = 0.8 - 0.6 * math.exp(-0.3 * l)
            q_t, k, v_t = _attn_proj(hs, mod_l, gains, attn_w_in, j, tab_a,
                                     da_dim ** -0.5 * math.log2(math.e), da_dim // 4, nct)
            o = _attention(q_t, k, v_t, attn_lambda[j], attn_subln[j], nctx, lam_init, not last)
            hs = _mixer_out(_attn_out_kernel, hs, mod_l, gains, nct, (o, attn_w_out),
                            [_tok_spec(bsz, d), _resident(attn_w_out, j)], skip_ctx=last)
            nct = 0 if last else nct
        elif kind == 1:
            chunks = ((0, d, 1.0, True), (d, d, dk ** -0.5, True)) + tuple(
                (c0, d, 1.0, False) for c0 in range(2 * d, 6 * d, d))
            p = _proj_in(hs, mod_l, gains, ret_w_in, j, tab_r, chunks, dk // 4, nct)
            log_g = jax.nn.log_sigmoid(ret_decay_logit[j].astype(F32))
            o = _retention(p, log_g, nctx)
            hs = _mixer_out(_ret_out_kernel, hs, mod_l, gains, nct, (o, p, ret_w_out),
                            [pl.BlockSpec((2, bsz, TM, 2 * d), lambda i: (0, 0, i, 0)),
                             _tok_spec(bsz, 2 * d, 2), _resident(ret_w_out, j)])
        else:
            u = _normmod(hs, mod_l, gains, nct)
            weights = _s5_weights(ssm_lambda_re[j], ssm_lambda_im[j], ssm_log_dt[j], ssm_b_re[j],
                                  ssm_b_im[j], ssm_c_re[j], ssm_c_im[j])
            y = _s5_core(u, weights, nctx)
            dskip = ssm_d[j].astype(F32).reshape(1, d)
            y_spec = pl.BlockSpec(y.shape[0:2] + (TM // SSM_CHUNK,) + y.shape[3:], lambda i: (0, 0, i, 0))
            hs = _mixer_out(_s5_out_kernel, hs, mod_l, gains, nct, (y, dskip, ssm_w_glu),
                            [y_spec, _resident(dskip), _resident(ssm_w_glu, j)],
                            scratch=[pltpu.VMEM((bsz, d // LANES, TM, LANES), F32)])
        hs = _ffn(hs, mod_l, gains, ffn_w_gate, ffn_w_up, ffn_w_down, l, nct)
    return hs if hs.shape[1] == seq else hs[:, nctx:, :]
```

```python
import functools
import math

import jax
import jax.numpy as jnp
from jax import lax
from jax.experimental import pallas as pl
from jax.experimental.pallas import tpu as pltpu

F32 = jnp.float32
BF16 = jnp.bfloat16

GRID_W = 64
DA_HEADS = 8
RET_HEADS = 4
RET_CHUNK = 256
SSM_GROUP = 16
SSM_STATE = 64
SSM_CHUNK = 16
SSM_GROUP_BLOCK = 8
ROPE_THETA = 10000.0
NORM_EPS = 1e-6
LANES = 128
MXU_COLS = 256
TM = 256
ATTN_TQ = 1024
VT_PAD = 16
VMEM_LIMIT = 56 * 1024 * 1024


def _cparams(*sem):
    return pltpu.CompilerParams(dimension_semantics=sem, vmem_limit_bytes=VMEM_LIMIT)


def _rms(x, gain):
    return x * lax.rsqrt(jnp.mean(x * x, axis=-1, keepdims=True) + NORM_EPS) * gain


def _silu(x):
    return x * jax.nn.sigmoid(x)


def _dot(a, b):
    return jnp.dot(a, b, preferred_element_type=F32)


def _dot_nt(a, b):
    return lax.dot_general(a, b, (((1,), (1,)), ((), ())), preferred_element_type=F32)


def _dot_tn(a, b):
    return lax.dot_general(a, b, (((0,), (0,)), ((), ())), preferred_element_type=F32)


def _mod_kernel(s_ref, w_ref, b_ref, o_ref):
    s = _silu(s_ref[...])
    o_ref[0] = jnp.dot(s, w_ref[0], preferred_element_type=F32,
                       precision=lax.Precision.HIGHEST) + b_ref[0]


def _modulation(c, c_ctx, mod_w, mod_b):
    depth, d, n = mod_w.shape
    bsz = c.shape[0]
    rows = 8
    s = jnp.zeros((rows, d), F32).at[:bsz].set(c).at[bsz].set(c_ctx)
    tn = 3072
    out = pl.pallas_call(
        _mod_kernel,
        grid=(depth, n // tn),
        in_specs=[
            pl.BlockSpec((rows, d), lambda l, j: (0, 0)),
            pl.BlockSpec((1, d, tn), lambda l, j: (l, 0, j)),
            pl.BlockSpec((1, 1, tn), lambda l, j: (l, 0, j)),
        ],
        out_specs=pl.BlockSpec((1, rows, tn), lambda l, j: (l, 0, j)),
        out_shape=jax.ShapeDtypeStruct((depth, rows, n), F32),
        compiler_params=_cparams("parallel", "parallel"),
    )(s, mod_w, mod_b.reshape(depth, 1, n))
    return out.reshape(depth, rows, 6, d)


def _mod_rows(mod_l, bsz):
    return jnp.concatenate([mod_l[:bsz], jnp.broadcast_to(mod_l[bsz], (bsz,) + mod_l.shape[1:])])


def _mod_spec(d, bsz, nctx_tiles):
    return pl.BlockSpec((bsz, 6, d), lambda i: (jnp.where(i < nctx_tiles, 1, 0), 0, 0))


def _tok_spec(bsz, width, col=0, off=0):
    return pl.BlockSpec((bsz, TM, width), lambda i: (0, i + off, col))


class _SplitTile:
    def __init__(self, ctx_ref, lat_ref, is_ctx):
        self.ctx_ref, self.lat_ref, self.is_ctx, self.shape = ctx_ref, lat_ref, is_ctx, lat_ref.shape

    def __getitem__(self, b):
        return jnp.where(self.is_ctx, self.ctx_ref[b], self.lat_ref[b])


def _split_input(kernel, nctx_tiles):
    def wrapped(ctx_ref, lat_ref, *rest, **kw):
        return kernel(_SplitTile(ctx_ref, lat_ref, pl.program_id(0) < nctx_tiles), *rest, **kw)
    return wrapped


def _stream_specs(hs, nctx_tiles, off=0):
    if not isinstance(hs, tuple):
        return [_tok_spec(hs.shape[0], hs.shape[2], off=off)]
    bsz, _, d = hs[1].shape
    return [pl.BlockSpec((bsz, TM, d), lambda i: (0, jnp.minimum(i + off, nctx_tiles - 1), 0)),
            pl.BlockSpec((bsz, TM, d), lambda i: (0, jnp.maximum(i + off - nctx_tiles, 0), 0))]


def _resident(a, layer=None):
    nd = a.ndim
    if layer is None:
        return pl.BlockSpec(a.shape, lambda i: (0,) * nd, pipeline_mode=pl.Buffered(1))
    return pl.BlockSpec((None,) + a.shape[1:], lambda i: (layer,) + (0,) * (nd - 1),
                        pipeline_mode=pl.Buffered(1))


def _rope_tables(seq, ctx, head_dim):
    q = head_dim // 4
    freqs = ROPE_THETA ** (-jnp.arange(q, dtype=F32) / q)
    rep = max(1, LANES // head_dim)

    def table(pos, first_seg, identity):
        ang = pos[:, None] * freqs
        cos = jnp.ones_like(ang) if identity else jnp.cos(ang)
        sin = jnp.zeros_like(ang) if identity else jnp.sin(ang)
        z = jnp.zeros_like(ang)
        segs = lambda a, b: jnp.concatenate([z] * first_seg + [a, b] + [z] * (2 - first_seg), axis=1)
        parts = (segs(cos, cos), segs(-sin, z), segs(z, sin))
        return jnp.concatenate([jnp.tile(p, (1, rep)) for p in parts], axis=1)

    per_tile = TM // GRID_W
    rows = jnp.arange(seq // GRID_W, dtype=F32)
    cols = jnp.arange(GRID_W, dtype=F32)
    row_lat = table(rows, 0, False).reshape(seq // TM, per_tile, -1)
    row_ctx = table(jnp.zeros((ctx // GRID_W,), F32), 0, True).reshape(ctx // TM, per_tile, -1)
    row_tab = jnp.concatenate([row_ctx, row_lat], axis=0)
    col_tab = jnp.stack([table(cols, 2, True), table(cols, 2, False)], axis=0)
    return row_tab, col_tab


def _rope_tile(row_ref, col_ref):
    col = col_ref[0]
    return jnp.concatenate([row_ref[0, a:a + 1, :] + col for a in range(row_ref.shape[1])], axis=0)


def _rope_specs(row_tab, col_tab, nctx_tiles):
    return [pl.BlockSpec((1,) + row_tab.shape[1:], lambda i: (i, 0, 0)),
            pl.BlockSpec((1,) + col_tab.shape[1:], lambda i: (jnp.where(i < nctx_tiles, 0, 1), 0, 0))]


def _apply_rope(y, tab, quarter):
    w = tab.shape[1] // 3
    cos, s_minus, s_plus = tab[:, :w], tab[:, w:2 * w], tab[:, 2 * w:]
    outs = []
    for k in range(y.shape[1] // LANES):
        yk = y[:, k * LANES:(k + 1) * LANES]
        t0 = (k * LANES) % w
        ck, mk, pk = (t[:, t0:t0 + LANES] for t in (cos, s_minus, s_plus))
        if 2 * quarter == LANES:
            outs.append(yk * ck + pltpu.roll(yk, quarter, 1) * (mk + pk))
        else:
            outs.append(yk * ck + pltpu.roll(yk, LANES - quarter, 1) * mk
                        + pltpu.roll(yk, quarter, 1) * pk)
    return jnp.concatenate(outs, axis=1)


def _proj_in_kernel(x_ref, mod_ref, g_ref, w_ref, row_ref, col_ref, o_ref, *, chunks, quarter):
    tab = _rope_tile(row_ref, col_ref)
    for b in range(x_ref.shape[0]):
        mod = mod_ref[b]
        u = (_rms(x_ref[b], g_ref[0:1, :]) * (1.0 + mod[1:2, :]) + mod[0:1, :]).astype(BF16)
        for c0, width, mult, rope in chunks:
            y = _dot(u, w_ref[:, c0:c0 + width])
            if mult != 1.0:
                y = y * mult
            if rope:
                y = _apply_rope(y, tab, quarter)
            o_ref[b, :, c0:c0 + width] = y.astype(BF16)


def _proj_in(hs, mod_l, gains, w, layer, tabs, chunks, quarter, nctx_tiles):
    bsz, L, d = hs.shape
    n = w.shape[2]
    return pl.pallas_call(
        functools.partial(_proj_in_kernel, chunks=chunks, quarter=quarter),
        grid=(L // TM,),
        in_specs=[_tok_spec(bsz, d), _mod_spec(d, bsz, nctx_tiles), _resident(gains), _resident(w, layer)]
        + _rope_specs(*tabs, nctx_tiles),
        out_specs=_tok_spec(bsz, n),
        out_shape=jax.ShapeDtypeStruct((bsz, L, n), BF16),
        compiler_params=_cparams("parallel"),
    )(hs, _mod_rows(mod_l, bsz), gains, w, *tabs)


def _attn_proj_kernel(x_ref, mod_ref, g_ref, w_ref, row_ref, col_ref, qt_ref, k_ref, vt_ref, *,
                      scale, quarter):
    d = x_ref.shape[2]
    tab = _rope_tile(row_ref, col_ref)
    heads = vt_ref.shape[1]
    for b in range(x_ref.shape[0]):
        mod = mod_ref[b]
        u = (_rms(x_ref[b], g_ref[0:1, :]) * (1.0 + mod[1:2, :]) + mod[0:1, :]).astype(BF16)
        qt_ref[b] = _apply_rope(_dot(u, w_ref[:, :d]) * scale, tab, quarter).T.astype(BF16)
        k_ref[b] = _apply_rope(_dot(u, w_ref[:, d:2 * d]), tab, quarter).astype(BF16)
        vt = _dot(u, w_ref[:, 2 * d:]).T.astype(BF16)
        vt_ref[b, :, 0, 0:LANES, :] = vt.reshape(heads, LANES, vt.shape[1])
        vt_ref[b, :, 0, LANES:, :] = jnp.ones((heads, VT_PAD, vt.shape[1]), BF16)


def _attn_proj(hs, mod_l, gains, w, layer, tabs, scale, quarter, nctx_tiles):
    split = isinstance(hs, tuple)
    streams = hs if split else (hs,)
    bsz, _, d = streams[-1].shape
    L = sum(a.shape[1] for a in streams)
    nlat = L // TM - nctx_tiles
    body = functools.partial(_attn_proj_kernel, scale=scale, quarter=quarter)
    return pl.pallas_call(
        _split_input(body, nctx_tiles) if split else body,
        grid=(L // TM,),
        in_specs=_stream_specs(hs, nctx_tiles)
        + [_mod_spec(d, bsz, nctx_tiles), _resident(gains), _resident(w, layer)]
        + _rope_specs(*tabs, nctx_tiles),
        out_specs=[
            pl.BlockSpec((bsz, d, TM), lambda i: (0, 0, jnp.where(i < nctx_tiles, i + nlat, i - nctx_tiles))),
            _tok_spec(bsz, d),
            pl.BlockSpec((bsz, DA_HEADS, 1, LANES + VT_PAD, TM), lambda i: (0, 0, i, 0, 0)),
        ],
        out_shape=[
            jax.ShapeDtypeStruct((bsz, d, L), BF16),
            jax.ShapeDtypeStruct((bsz, L, d), BF16),
            jax.ShapeDtypeStruct((bsz, DA_HEADS, L // TM, LANES + VT_PAD, TM), BF16),
        ],
        compiler_params=_cparams("parallel"),
    )(*streams, _mod_rows(mod_l, bsz), gains, w, *tabs)


def _attn_kernel(lam_ref, sub_ref, qt_ref, k_ref, vt_ref, o_ref,
                 s_c, s_a, s_b, mb_scr, m_scr, acc_scr, *, nb, lam_init):
    ctx = s_c.shape[2]
    tk = s_a.shape[2]
    kv = vt_ref.shape[4]
    qt = qt_ref[0]
    row = lax.broadcasted_iota(jnp.int32, qt.shape, 0)
    zero = jnp.zeros_like(qt)
    qts = (jnp.where(row < LANES // 2, qt, zero), jnp.where(row >= LANES // 2, qt, zero))

    slot = {id(s_c): 0, id(s_a): 1, id(s_b): 2}

    nqb = qt.shape[1] // MXU_COLS

    def scores(kb, s_out):
        for part in range(2):
            for j in range(nqb):
                s = _dot(kb, qts[part][:, j * MXU_COLS:(j + 1) * MXU_COLS])
                s_out[part, j] = s
                mb_scr[slot[id(s_out)], part, j] = jnp.max(s, axis=0, keepdims=True)

    def stage(s_in, vt0, nxt):
        if nxt is not None:
            scores(*nxt)
        for part in range(2):
            for j in range(nqb):
                m = m_scr[part, j]
                m_new = jnp.maximum(m, mb_scr[slot[id(s_in)], part, j])
                alpha = jnp.exp2(m - m_new)
                pv = None
                for n in range(s_in.shape[2] // kv):
                    p = jnp.exp2(s_in[part, j, n * kv:(n + 1) * kv, :] - m_new).astype(BF16)
                    pv_n = _dot(vt_ref[0, 0, vt0 + n], p)
                    pv = pv_n if pv is None else pv + pv_n
                m_scr[part, j] = m_new
                acc_scr[part, j] = alpha * acc_scr[part, j] + pv

    m_scr[...] = jnp.full(m_scr.shape, -1e30, F32)
    acc_scr[...] = jnp.zeros(acc_scr.shape, F32)

    bufs = (s_b, s_a)

    def keys(t):
        return k_ref[0, pl.ds(pl.multiple_of(ctx + (t - 1) * tk, LANES), tk), :]

    def vt_index(t):
        return ctx // kv + (t - 1) * (tk // kv)

    scores(k_ref[0, 0:ctx, :], s_c)
    stage(s_c, 0, (keys(1), s_a) if nb >= 1 else None)
    n_pairs = max(0, (nb - 1) // 2 if nb % 2 else (nb - 2) // 2)

    def pair(u, carry):
        t = 2 * u + 1
        stage(s_a, vt_index(t), (keys(t + 1), s_b))
        stage(s_b, vt_index(t + 1), (keys(t + 2), s_a))
        return carry

    if n_pairs > 0:
        lax.fori_loop(0, n_pairs, pair, 0)
    for t in range(2 * n_pairs + 1, nb + 1):
        stage(bufs[t % 2], vt_index(t), (keys(t + 1), bufs[(t + 1) % 2]) if t < nb else None)

    lv = lam_ref[...]
    lam = (jnp.exp(jnp.sum(lv[0:1] * lv[1:2], axis=-1, keepdims=True))
           - jnp.exp(jnp.sum(lv[2:3] * lv[3:4], axis=-1, keepdims=True)) + lam_init)
    a1, a2 = (jnp.concatenate([acc_scr[part, j] for j in range(nqb)], axis=1) for part in range(2))
    o = (a1[:LANES] / a1[LANES:LANES + 1] - lam * (a2[:LANES] / a2[LANES:LANES + 1])).T
    o = _rms(o, sub_ref[...]) * (1.0 - lam_init)
    o_ref[0] = o.astype(BF16)


def _attention_call(q_t, k, v_t, lam_vec, subln, ctx, tk, nb, lam_init, tq, q_off, n_out):
    bsz, d, _ = q_t.shape
    kv = v_t.shape[4]
    k_rows = ctx + nb * tk
    nqb = tq // MXU_COLS
    return pl.pallas_call(
        functools.partial(_attn_kernel, nb=nb, lam_init=lam_init),
        grid=(bsz, DA_HEADS, n_out // tq),
        in_specs=[
            pl.BlockSpec(lam_vec.shape, lambda b, h, i: (0, 0)),
            pl.BlockSpec((1, LANES), lambda b, h, i: (0, 0)),
            pl.BlockSpec((1, LANES, tq), lambda b, h, i: (b, h, i + q_off // tq)),
            pl.BlockSpec((1, k_rows, LANES), lambda b, h, i: (b, 0, h)),
            pl.BlockSpec((1, 1, k_rows // kv) + v_t.shape[3:], lambda b, h, i: (b, h, 0, 0, 0)),
        ],
        out_specs=pl.BlockSpec((1, tq, LANES), lambda b, h, i: (b, i, h)),
        out_shape=jax.ShapeDtypeStruct((bsz, n_out, d), BF16),
        scratch_shapes=[pltpu.VMEM((2, nqb, ctx, MXU_COLS), F32), pltpu.VMEM((2, nqb, tk, MXU_COLS), F32),
                        pltpu.VMEM((2, nqb, tk, MXU_COLS), F32), pltpu.VMEM((3, 2, nqb, 1, MXU_COLS), F32),
                        pltpu.VMEM((2, nqb, 1, MXU_COLS), F32),
                        pltpu.VMEM((2, nqb, LANES + VT_PAD, MXU_COLS), F32)],
        compiler_params=_cparams("parallel", "parallel", "arbitrary"),
    )(lam_vec, subln.reshape(1, LANES), q_t, k, v_t)


def _attention(q_t, k, v_t, lam_vec, subln, ctx, lam_init, with_ctx):
    L = k.shape[1]
    seq = L - ctx
    tk = min(1024, seq)
    nb = seq // tk
    tq = min(ATTN_TQ, seq)
    o_lat = _attention_call(q_t, k, v_t, lam_vec, subln, ctx, tk, nb, lam_init, tq, 0, seq)
    if not with_ctx:
        return o_lat
    o_ctx = _attention_call(q_t, k, v_t, lam_vec, subln, ctx, tk, 0, lam_init, TM, seq, ctx)
    return jnp.concatenate([o_ctx, o_lat], axis=1)


def _ret_kernel(lg_ref, q_ref, k_ref, v_ref, o_ref, r_scr):
    d = pl.program_id(1)
    c = pl.program_id(2)
    C = q_ref.shape[1]
    heads, dk, dv = r_scr.shape

    @pl.when(c == 0)
    def _():
        r_scr[...] = jnp.zeros_like(r_scr)

    fwd = d == 0
    ii = lax.broadcasted_iota(jnp.int32, (C, C), 0)
    jj = lax.broadcasted_iota(jnp.int32, (C, C), 1)
    dd = jnp.where(fwd, ii - jj, jj - ii)
    causal = dd >= 0
    lag = jnp.maximum(dd, 0).astype(F32)
    idx = lax.broadcasted_iota(jnp.int32, (C, 1), 0)
    q_lag = jnp.where(fwd, idx + 1, C - idx).astype(F32)
    k_lag = jnp.where(fwd, C - 1 - idx, idx).astype(F32)

    for h in range(heads):
        lg = lg_ref[0, h:h + 1, 0:1]
        q = q_ref[0, :, h * dk:(h + 1) * dk]
        k = k_ref[0, :, h * dk:(h + 1) * dk]
        v = v_ref[0, :, h * dv:(h + 1) * dv]
        r = r_scr[h]
        s = _dot_nt(q, k) * jnp.where(causal, jnp.exp(lg * lag), 0.0)
        inner = _dot(s.astype(BF16), v)
        cross = _dot((q.astype(F32) * jnp.exp(lg * q_lag)).astype(BF16), r.astype(BF16))
        o_ref[0, 0, :, h * dv:(h + 1) * dv] = (inner + cross).astype(BF16)
        r_scr[h] = jnp.exp(lg * float(C)) * r + _dot_tn((k.astype(F32) * jnp.exp(lg * k_lag)).astype(BF16), v)


def _retention(p, log_g, ctx):
    bsz, L, n = p.shape
    d = n // 6
    C = RET_CHUNK
    nc, ncc = L // C, ctx // C

    def chunk(dr, c):
        back = jnp.where(c < ncc, ncc - 1 - c, nc + ncc - 1 - c)
        return jnp.where(dr == 0, c, back)

    lg = jnp.broadcast_to(log_g.reshape(2, RET_HEADS, 1), (2, RET_HEADS, LANES))
    return pl.pallas_call(
        _ret_kernel,
        grid=(bsz, 2, nc),
        in_specs=[
            pl.BlockSpec((1, RET_HEADS, LANES), lambda b, dr, c: (dr, 0, 0)),
            pl.BlockSpec((1, C, d), lambda b, dr, c: (b, chunk(dr, c), 0)),
            pl.BlockSpec((1, C, d), lambda b, dr, c: (b, chunk(dr, c), 1)),
            pl.BlockSpec((1, C, 2 * d), lambda b, dr, c: (b, chunk(dr, c), 1)),
        ],
        out_specs=pl.BlockSpec((1, 1, C, 2 * d), lambda b, dr, c: (dr, b, chunk(dr, c), 0)),
        out_shape=jax.ShapeDtypeStruct((2, bsz, L, 2 * d), BF16),
        scratch_shapes=[pltpu.VMEM((RET_HEADS, d // RET_HEADS, 2 * d // RET_HEADS), F32)],
        compiler_params=_cparams("parallel", "arbitrary", "arbitrary"),
    )(lg, p, p, p)


def _transpose_pieces(tiles):
    hg = SSM_GROUP
    per = len(tiles)
    piece = lax.broadcasted_iota(jnp.int32, tiles[0].shape, 1) // hg
    k = per // 2
    while k >= 1:
        keep = (piece & k) == 0
        out = list(tiles)
        for r in range(per):
            if r & k == 0:
                x, y = tiles[r], tiles[r + k]
                out[r] = jnp.where(keep, x, pltpu.roll(y, hg * k, 1))
                out[r + k] = jnp.where(keep, pltpu.roll(x, LANES - hg * k, 1), y)
        tiles = out
        k //= 2
    return tiles


def _to_group_major(u_scr, o_ref, b):
    n, hg = SSM_CHUNK, SSM_GROUP
    per = LANES // hg
    _, ntile, tm, _ = u_scr.shape
    rows = tm // n
    piece = lax.broadcasted_iota(jnp.int32, (rows, LANES), 1) // hg
    for lt in range(ntile):
        for th in range(n * hg // LANES):
            src = [u_scr[b, lt, pl.ds(th * per + tp, rows, stride=n), :] for tp in range(per)]
            for j in range(per):
                acc = None
                for tp in range(per):
                    shift = (hg * (tp - j)) % LANES
                    moved = src[tp] if shift == 0 else pltpu.roll(src[tp], shift, 1)
                    acc = moved if acc is None else jnp.where(piece == tp, moved, acc)
                o_ref[b, lt * per + j, :, th * LANES:(th + 1) * LANES] = acc.astype(o_ref.dtype)


def _from_group_major(y_ref, o_scr, b):
    n, hg = SSM_CHUNK, SSM_GROUP
    per = LANES // hg
    _, ntile, tm, _ = o_scr.shape
    rows = tm // n
    for lt in range(ntile):
        for th in range(n * hg // LANES):
            src = [y_ref[b, lt * per + j, :, th * LANES:(th + 1) * LANES] for j in range(per)]
            for tp, tile in enumerate(_transpose_pieces(src)):
                o_scr[b, lt, pl.ds(th * per + tp, rows, stride=n), :] = tile


def _normmod_kernel(x_ref, mod_ref, g_ref, o_ref, u_scr):
    for b in range(x_ref.shape[0]):
        mod = mod_ref[b]
        u = _rms(x_ref[b], g_ref[0:1, :]) * (1.0 + mod[1:2, :]) + mod[0:1, :]
        for lt in range(u_scr.shape[1]):
            u_scr[b, lt] = u[:, lt * LANES:(lt + 1) * LANES]
        _to_group_major(u_scr, o_ref, b)


def _normmod(hs, mod_l, gains, nctx_tiles):
    bsz, L, d = hs.shape
    G, rows, width = d // SSM_GROUP, TM // SSM_CHUNK, SSM_CHUNK * SSM_GROUP
    return pl.pallas_call(
        _normmod_kernel,
        grid=(L // TM,),
        in_specs=[_tok_spec(bsz, d), _mod_spec(d, bsz, nctx_tiles), _resident(gains)],
        out_specs=pl.BlockSpec((bsz, G, rows, width), lambda i: (0, 0, i, 0)),
        out_shape=jax.ShapeDtypeStruct((bsz, G, L // SSM_CHUNK, width), BF16),
        scratch_shapes=[pltpu.VMEM((bsz, d // LANES, TM, LANES), F32)],
        compiler_params=_cparams("parallel"),
    )(hs, _mod_rows(mod_l, bsz), gains)


def _s5_kernel(x_ref, t_ref, wb_ref, wc_ref, a_ref, y_ref, st_scr, *, nctx):
    gb = x_ref.shape[1]
    nch = x_ref.shape[2]
    for gi in range(gb):
        x = x_ref[0, gi]
        for dr in range(2):
            s = _dot(x, wb_ref[gi, dr])
            st_scr[2 * dr, pl.ds(gi, nch, stride=gb), :] = s[:, :LANES]
            st_scr[2 * dr + 1, pl.ds(gi, nch, stride=gb), :] = s[:, LANES:]

    a = a_ref[...]

    def step(dr, row, h_re, h_im):
        s_re = st_scr[2 * dr, pl.ds(row, gb), :]
        s_im = st_scr[2 * dr + 1, pl.ds(row, gb), :]
        st_scr[2 * dr, pl.ds(row, gb), :] = h_re
        st_scr[2 * dr + 1, pl.ds(row, gb), :] = h_im
        a_re, a_im = a[dr, 0], a[dr, 1]
        return a_re * h_re - a_im * h_im + s_re, a_re * h_im + a_im * h_re + s_im

    def body(t, carry):
        f_re, f_im, b_re, b_im = carry
        cb = jnp.where(t < nctx, nctx - 1 - t, nch + nctx - 1 - t)
        f_re, f_im = step(0, pl.multiple_of(t * gb, gb), f_re, f_im)
        b_re, b_im = step(1, pl.multiple_of(cb * gb, gb), b_re, b_im)
        return f_re, f_im, b_re, b_im

    z = jnp.zeros((gb, LANES), F32)
    lax.fori_loop(0, nch, body, (z, z, z, z))

    for gi in range(gb):
        y = _dot(x_ref[0, gi], t_ref[gi])
        for dr in range(2):
            h = jnp.concatenate([st_scr[2 * dr, pl.ds(gi, nch, stride=gb), :],
                                 st_scr[2 * dr + 1, pl.ds(gi, nch, stride=gb), :]], axis=1)
            y = y + _dot(h.astype(BF16), wc_ref[gi, dr])
        y_ref[0, gi] = y


def _s5_weights(lam_re, lam_im, log_dt, b_re, b_im, c_re, c_im):
    n = SSM_CHUNK
    lam = lax.complex(lam_re.astype(F32), lam_im.astype(F32))
    ldt = lam * jnp.exp(log_dt.astype(F32))[..., None]
    lam_bar = jnp.exp(ldt)
    b_bar = ((lam_bar - 1.0) / lam)[..., None] * lax.complex(b_re.astype(F32), b_im.astype(F32))
    c_mat = lax.complex(c_re.astype(F32), c_im.astype(F32))
    steps = jnp.arange(n + 1, dtype=F32)
    pw = jnp.exp(ldt[None] * steps[:, None, None, None])
    G, P, Hg = b_bar.shape[1:]

    kern = jnp.real(jnp.einsum('kdgp,dgjp,dgpi->dkgij', pw[:n], c_mat, b_bar))
    lag = jnp.arange(n)[None, :] - jnp.arange(n)[:, None]
    k_idx = jnp.arange(n)[:, None, None]
    pick = jnp.stack([lag[None] == k_idx, -lag[None] == k_idx]).astype(F32)
    t_mat = jnp.einsum('dkst,dkgij->gsitj', pick, kern).reshape(G, n * Hg, n * Hg)

    zpad = jnp.zeros((G, n * Hg, LANES - P), F32)
    zrow = jnp.zeros((G, LANES - P, n * Hg), F32)
    wbs, wcs = [], []
    for dr in range(2):
        e_in = pw[:n][::-1, dr] if dr == 0 else pw[:n, dr]
        wb = (e_in[:, :, :, None] * b_bar[dr][None]).transpose(1, 0, 3, 2).reshape(G, n * Hg, P)
        wbs.append(jnp.concatenate([jnp.real(wb), zpad, jnp.imag(wb), zpad], axis=2))
        e_out = pw[1:, dr] if dr == 0 else pw[1:][::-1, dr]
        wc = (c_mat[dr][None] * e_out[:, :, None, :]).transpose(1, 3, 0, 2).reshape(G, P, n * Hg)
        wcs.append(jnp.concatenate([jnp.real(wc), zrow, -jnp.imag(wc), zrow], axis=1))
    w_b = jnp.stack(wbs, axis=1)
    w_c = jnp.stack(wcs, axis=1)
    a_n = pw[n]
    pad = jnp.zeros((2, G, LANES - P), F32)
    a = jnp.stack([jnp.concatenate([jnp.real(a_n), pad], -1),
                   jnp.concatenate([jnp.imag(a_n), pad], -1)], axis=1)
    return t_mat.astype(BF16), w_b.astype(BF16), w_c.astype(BF16), a


def _s5_core(x, weights, ctx):
    bsz, G, nch, width = x.shape
    n, gb = SSM_CHUNK, SSM_GROUP_BLOCK
    t_mat, w_b, w_c, a = weights
    return pl.pallas_call(
        functools.partial(_s5_kernel, nctx=ctx // n),
        grid=(bsz, G // gb),
        in_specs=[
            pl.BlockSpec((1, gb, nch, width), lambda b, g: (b, g, 0, 0)),
            pl.BlockSpec((gb, width, width), lambda b, g: (g, 0, 0)),
            pl.BlockSpec((gb, 2, width, 2 * LANES), lambda b, g: (g, 0, 0, 0)),
            pl.BlockSpec((gb, 2, 2 * LANES, width), lambda b, g: (g, 0, 0, 0)),
            pl.BlockSpec((2, 2, gb, LANES), lambda b, g: (0, 0, g, 0)),
        ],
        out_specs=pl.BlockSpec((1, gb, nch, width), lambda b, g: (b, g, 0, 0)),
        out_shape=jax.ShapeDtypeStruct((bsz, G, nch, width), F32),
        scratch_shapes=[pltpu.VMEM((4, nch * gb, LANES), F32)],
        compiler_params=_cparams("parallel", "parallel"),
    )(x, t_mat, w_b, w_c, a)


def _finish(h, y, mod, g_ref, o_ref, b):
    o_ref[b] = h + mod[2:3, :] * _rms(y, g_ref[1:2, :])


def _attn_out_kernel(h_ref, mod_ref, g_ref, o_in_ref, w_ref, o_ref):
    for b in range(h_ref.shape[0]):
        _finish(h_ref[b], _dot(o_in_ref[b], w_ref[...]), mod_ref[b], g_ref, o_ref, b)


def _ret_out_kernel(h_ref, mod_ref, g_ref, o_in_ref, gate_ref, w_ref, o_ref):
    for b in range(h_ref.shape[0]):
        o = o_in_ref[0, b].astype(F32) + o_in_ref[1, b].astype(F32)
        dv = o.shape[1] // RET_HEADS
        parts = []
        for hd in range(RET_HEADS):
            oh = o[:, hd * dv:(hd + 1) * dv]
            parts.append(oh * lax.rsqrt(jnp.mean(oh * oh, axis=-1, keepdims=True) + NORM_EPS))
        o = jnp.concatenate(parts, axis=1)
        y = _dot((_silu(gate_ref[b].astype(F32)) * o).astype(BF16), w_ref[...])
        _finish(h_ref[b], y, mod_ref[b], g_ref, o_ref, b)


def _s5_out_kernel(h_ref, mod_ref, g_ref, y_ref, dskip_ref, w_ref, o_ref, y_scr):
    for b in range(h_ref.shape[0]):
        h = h_ref[b]
        mod = mod_ref[b]
        d = h.shape[1]
        _from_group_major(y_ref, y_scr, b)
        u = _rms(h, g_ref[0:1, :]) * (1.0 + mod[1:2, :]) + mod[0:1, :]
        y_ssm = jnp.concatenate([y_scr[b, lt] for lt in range(y_scr.shape[1])], axis=1)
        z = y_ssm + dskip_ref[...] * u
        gl = 0.5 * z * (1.0 + jnp.tanh(math.sqrt(2.0 / math.pi) * (z + 0.044715 * (z * z * z))))
        gl = gl.astype(BF16)
        y = _dot(gl, w_ref[:, :d]) * jax.nn.sigmoid(_dot(gl, w_ref[:, d:]))
        _finish(h, y, mod, g_ref, o_ref, b)


def _mixer_out(kernel, hs, mod_l, gains, nctx_tiles, extra, extra_specs, scratch=(), skip_ctx=False):
    split = isinstance(hs, tuple)
    streams = hs if split else (hs,)
    bsz, _, d = streams[-1].shape
    L = sum(a.shape[1] for a in streams)
    off = nctx_tiles if skip_ctx else 0
    n_tiles = L // TM - off
    return pl.pallas_call(
        _split_input(kernel, nctx_tiles - off) if split else kernel,
        grid=(n_tiles,),
        in_specs=_stream_specs(hs, nctx_tiles, off)
        + [_mod_spec(d, bsz, nctx_tiles - off), _resident(gains)] + extra_specs,
        out_specs=_tok_spec(bsz, d),
        out_shape=jax.ShapeDtypeStruct((bsz, n_tiles * TM, d), F32),
        scratch_shapes=list(scratch),
        compiler_params=_cparams("parallel"),
    )(*streams, _mod_rows(mod_l, bsz), gains, *extra)


def _ffn_kernel(h_ref, mod_ref, g_ref, wg_ref, wu_ref, wd_ref, o_ref):
    for b in range(h_ref.shape[0]):
        h = h_ref[b]
        mod = mod_ref[b]
        v = (_rms(h, g_ref[2:3, :]) * (1.0 + mod[4:5, :]) + mod[3:4, :]).astype(BF16)
        act = (_silu(_dot(v, wg_ref[...])) * _dot(v, wu_ref[...])).astype(BF16)
        y = _dot(act, wd_ref[...])
        o_ref[b] = h + mod[5:6, :] * _rms(y, g_ref[3:4, :])


def _ffn(hs, mod_l, gains, wg, wu, wd, layer, nctx_tiles):
    bsz, L, d = hs.shape
    return pl.pallas_call(
        _ffn_kernel,
        grid=(L // TM,),
        in_specs=[_tok_spec(bsz, d), _mod_spec(d, bsz, nctx_tiles), _resident(gains),
                  _resident(wg, layer), _resident(wu, layer), _resident(wd, layer)],
        out_specs=_tok_spec(bsz, d),
        out_shape=jax.ShapeDtypeStruct((bsz, L, d), F32),
        compiler_params=_cparams("parallel"),
    )(hs, _mod_rows(mod_l, bsz), gains, wg, wu, wd)


def kernel(x, c, ctx, c_ctx, mod_w, mod_b, norm_g, attn_w_in, attn_w_out, attn_lambda, attn_subln,
           ret_w_in, ret_w_out, ret_decay_logit, ssm_lambda_re, ssm_lambda_im, ssm_log_dt,
           ssm_b_re, ssm_b_im, ssm_c_re, ssm_c_im, ssm_d, ssm_w_glu, ffn_w_gate, ffn_w_up, ffn_w_down):
    bsz, seq, d = x.shape
    nctx = ctx.shape[1]
    depth = mod_w.shape[0]
    assert nctx % TM == 0 and seq % TM == 0 and seq % GRID_W == 0
    nct = nctx // TM
    L = nctx + seq

    mods = _modulation(c, c_ctx, mod_w, mod_b)
    (attn_w_in, attn_w_out, ret_w_in, ret_w_out, ssm_w_glu, ffn_w_gate, ffn_w_up, ffn_w_down) = (
        w.astype(BF16) for w in (attn_w_in, attn_w_out, ret_w_in, ret_w_out, ssm_w_glu,
                                 ffn_w_gate, ffn_w_up, ffn_w_down))
    hs = (ctx, x) if depth > 1 else jnp.concatenate([ctx, x], axis=1)
    da_dim = d // (2 * DA_HEADS)
    dk = d // RET_HEADS
    tab_a = _rope_tables(seq, nctx, da_dim)
    tab_r = _rope_tables(seq, nctx, dk)

    for l in range(depth):
        kind, j = l % 3, l // 3
        last = l == depth - 1
        mod_l, gains = mods[l], norm_g[l]
        if kind == 0:
            lam_init = 0.8 - 0.6 * math.exp(-0.3 * l)
            q_t, k, v_t = _attn_proj(hs, mod_l, gains, attn_w_in, j, tab_a,
                                     da_dim ** -0.5 * math.log2(math.e), da_dim // 4, nct)
            o = _attention(q_t, k, v_t, attn_lambda[j], attn_subln[j], nctx, lam_init, not last)
            hs = _mixer_out(_attn_out_kernel, hs, mod_l, gains, nct, (o, attn_w_out),
                            [_tok_spec(bsz, d), _resident(attn_w_out, j)], skip_ctx=last)
            nct = 0 if last else nct
        elif kind == 1:
            chunks = ((0, d, 1.0, True), (d, d, dk ** -0.5, True)) + tuple(
                (c0, d, 1.0, False) for c0 in range(2 * d, 6 * d, d))
            p = _proj_in(hs, mod_l, gains, ret_w_in, j, tab_r, chunks, dk // 4, nct)
            log_g = jax.nn.log_sigmoid(ret_decay_logit[j].astype(F32))
            o = _retention(p, log_g, nctx)
            hs = _mixer_out(_ret_out_kernel, hs, mod_l, gains, nct, (o, p, ret_w_out),
                            [pl.BlockSpec((2, bsz, TM, 2 * d), lambda i: (0, 0, i, 0)),
                             _tok_spec(bsz, 2 * d, 2), _resident(ret_w_out, j)])
        else:
            u = _normmod(hs, mod_l, gains, nct)
            weights = _s5_weights(ssm_lambda_re[j], ssm_lambda_im[j], ssm_log_dt[j], ssm_b_re[j],
                                  ssm_b_im[j], ssm_c_re[j], ssm_c_im[j])
            y = _s5_core(u, weights, nctx)
            dskip = ssm_d[j].astype(F32).reshape(1, d)
            y_spec = pl.BlockSpec(y.shape[0:2] + (TM // SSM_CHUNK,) + y.shape[3:], lambda i: (0, 0, i, 0))
            hs = _mixer_out(_s5_out_kernel, hs, mod_l, gains, nct, (y, dskip, ssm_w_glu),
                            [y_spec, _resident(dskip), _resident(ssm_w_glu, j)],
                            scratch=[pltpu.VMEM((bsz, d // LANES, TM, LANES), F32)])
        hs = _ffn(hs, mod_l, gains, ffn_w_gate, ffn_w_up, ffn_w_down, l, nct)
    return hs if hs.shape[1] == seq else hs[:, nctx:, :]
```

```python
import functools
import math

import jax
import jax.numpy as jnp
from jax import lax
from jax.experimental import pallas as pl
from jax.experimental.pallas import tpu as pltpu

F32 = jnp.float32
BF16 = jnp.bfloat16

GRID_W = 64
DA_HEADS = 8
RET_HEADS = 4
RET_CHUNK = 256
SSM_GROUP = 16
SSM_STATE = 64
SSM_CHUNK = 16
SSM_GROUP_BLOCK = 8
ROPE_THETA = 10000.0
NORM_EPS = 1e-6
LANES = 128
MXU_COLS = 256
TM = 256
ATTN_TQ = 1024
VT_PAD = 16
VMEM_LIMIT = 56 * 1024 * 1024


def _cparams(*sem):
    return pltpu.CompilerParams(dimension_semantics=sem, vmem_limit_bytes=VMEM_LIMIT)


def _rms(x, gain):
    return x * lax.rsqrt(jnp.mean(x * x, axis=-1, keepdims=True) + NORM_EPS) * gain


def _silu(x):
    return x * jax.nn.sigmoid(x)


def _dot(a, b):
    return jnp.dot(a, b, preferred_element_type=F32)


def _dot_nt(a, b):
    return lax.dot_general(a, b, (((1,), (1,)), ((), ())), preferred_element_type=F32)


def _dot_tn(a, b):
    return lax.dot_general(a, b, (((0,), (0,)), ((), ())), preferred_element_type=F32)


def _mod_kernel(s_ref, w_ref, b_ref, o_ref):
    s = _silu(s_ref[...])
    o_ref[0] = jnp.dot(s, w_ref[0], preferred_element_type=F32,
                       precision=lax.Precision.HIGHEST) + b_ref[0]


def _modulation(c, c_ctx, mod_w, mod_b):
    depth, d, n = mod_w.shape
    bsz = c.shape[0]
    rows = 8
    s = jnp.zeros((rows, d), F32).at[:bsz].set(c).at[bsz].set(c_ctx)
    tn = 3072
    out = pl.pallas_call(
        _mod_kernel,
        grid=(depth, n // tn),
        in_specs=[
            pl.BlockSpec((rows, d), lambda l, j: (0, 0)),
            pl.BlockSpec((1, d, tn), lambda l, j: (l, 0, j)),
            pl.BlockSpec((1, 1, tn), lambda l, j: (l, 0, j)),
        ],
        out_specs=pl.BlockSpec((1, rows, tn), lambda l, j: (l, 0, j)),
        out_shape=jax.ShapeDtypeStruct((depth, rows, n), F32),
        compiler_params=_cparams("parallel", "parallel"),
    )(s, mod_w, mod_b.reshape(depth, 1, n))
    return out.reshape(depth, rows, 6, d)


def _mod_rows(mod_l, bsz):
    return jnp.concatenate([mod_l[:bsz], jnp.broadcast_to(mod_l[bsz], (bsz,) + mod_l.shape[1:])])


def _mod_spec(d, bsz, nctx_tiles):
    return pl.BlockSpec((bsz, 6, d), lambda i: (jnp.where(i < nctx_tiles, 1, 0), 0, 0))


def _tok_spec(bsz, width, col=0, off=0):
    return pl.BlockSpec((bsz, TM, width), lambda i: (0, i + off, col))


class _SplitTile:
    def __init__(self, ctx_ref, lat_ref, is_ctx):
        self.ctx_ref, self.lat_ref, self.is_ctx, self.shape = ctx_ref, lat_ref, is_ctx, lat_ref.shape

    def __getitem__(self, b):
        return jnp.where(self.is_ctx, self.ctx_ref[b], self.lat_ref[b])


def _split_input(kernel, nctx_tiles):
    def wrapped(ctx_ref, lat_ref, *rest, **kw):
        return kernel(_SplitTile(ctx_ref, lat_ref, pl.program_id(0) < nctx_tiles), *rest, **kw)
    return wrapped


def _stream_specs(hs, nctx_tiles, off=0):
    if not isinstance(hs, tuple):
        return [_tok_spec(hs.shape[0], hs.shape[2], off=off)]
    bsz, _, d = hs[1].shape
    return [pl.BlockSpec((bsz, TM, d), lambda i: (0, jnp.minimum(i + off, nctx_tiles - 1), 0)),
            pl.BlockSpec((bsz, TM, d), lambda i: (0, jnp.maximum(i + off - nctx_tiles, 0), 0))]


def _resident(a, layer=None):
    nd = a.ndim
    if layer is None:
        return pl.BlockSpec(a.shape, lambda i: (0,) * nd, pipeline_mode=pl.Buffered(1))
    return pl.BlockSpec((None,) + a.shape[1:], lambda i: (layer,) + (0,) * (nd - 1),
                        pipeline_mode=pl.Buffered(1))


def _rope_tables(seq, ctx, head_dim):
    q = head_dim // 4
    freqs = ROPE_THETA ** (-jnp.arange(q, dtype=F32) / q)
    rep = max(1, LANES // head_dim)

    def table(pos, first_seg, identity):
        ang = pos[:, None] * freqs
        cos = jnp.ones_like(ang) if identity else jnp.cos(ang)
        sin = jnp.zeros_like(ang) if identity else jnp.sin(ang)
        z = jnp.zeros_like(ang)
        segs = lambda a, b: jnp.concatenate([z] * first_seg + [a, b] + [z] * (2 - first_seg), axis=1)
        parts = (segs(cos, cos), segs(-sin, z), segs(z, sin))
        return jnp.concatenate([jnp.tile(p, (1, rep)) for p in parts], axis=1)

    per_tile = TM // GRID_W
    rows = jnp.arange(seq // GRID_W, dtype=F32)
    cols = jnp.arange(GRID_W, dtype=F32)
    row_lat = table(rows, 0, False).reshape(seq // TM, per_tile, -1)
    row_ctx = table(jnp.zeros((ctx // GRID_W,), F32), 0, True).reshape(ctx // TM, per_tile, -1)
    row_tab = jnp.concatenate([row_ctx, row_lat], axis=0)
    col_tab = jnp.stack([table(cols, 2, True), table(cols, 2, False)], axis=0)
    return row_tab, col_tab


def _rope_tile(row_ref, col_ref):
    col = col_ref[0]
    return jnp.concatenate([row_ref[0, a:a + 1, :] + col for a in range(row_ref.shape[1])], axis=0)


def _rope_specs(row_tab, col_tab, nctx_tiles):
    return [pl.BlockSpec((1,) + row_tab.shape[1:], lambda i: (i, 0, 0)),
            pl.BlockSpec((1,) + col_tab.shape[1:], lambda i: (jnp.where(i < nctx_tiles, 0, 1), 0, 0))]


def _apply_rope(y, tab, quarter):
    w = tab.shape[1] // 3
    cos, s_minus, s_plus = tab[:, :w], tab[:, w:2 * w], tab[:, 2 * w:]
    outs = []
    for k in range(y.shape[1] // LANES):
        yk = y[:, k * LANES:(k + 1) * LANES]
        t0 = (k * LANES) % w
        ck, mk, pk = (t[:, t0:t0 + LANES] for t in (cos, s_minus, s_plus))
        if 2 * quarter == LANES:
            outs.append(yk * ck + pltpu.roll(yk, quarter, 1) * (mk + pk))
        else:
            outs.append(yk * ck + pltpu.roll(yk, LANES - quarter, 1) * mk
                        + pltpu.roll(yk, quarter, 1) * pk)
    return jnp.concatenate(outs, axis=1)


def _proj_in_kernel(x_ref, mod_ref, g_ref, w_ref, row_ref, col_ref, o_ref, *, chunks, quarter):
    tab = _rope_tile(row_ref, col_ref)
    for b in range(x_ref.shape[0]):
        mod = mod_ref[b]
        u = (_rms(x_ref[b], g_ref[0:1, :]) * (1.0 + mod[1:2, :]) + mod[0:1, :]).astype(BF16)
        for c0, width, mult, rope in chunks:
            y = _dot(u, w_ref[:, c0:c0 + width])
            if mult != 1.0:
                y = y * mult
            if rope:
                y = _apply_rope(y, tab, quarter)
            o_ref[b, :, c0:c0 + width] = y.astype(BF16)


def _proj_in(hs, mod_l, gains, w, layer, tabs, chunks, quarter, nctx_tiles):
    bsz, L, d = hs.shape
    n = w.shape[2]
    return pl.pallas_call(
        functools.partial(_proj_in_kernel, chunks=chunks, quarter=quarter),
        grid=(L // TM,),
        in_specs=[_tok_spec(bsz, d), _mod_spec(d, bsz, nctx_tiles), _resident(gains), _resident(w, layer)]
        + _rope_specs(*tabs, nctx_tiles),
        out_specs=_tok_spec(bsz, n),
        out_shape=jax.ShapeDtypeStruct((bsz, L, n), BF16),
        compiler_params=_cparams("parallel"),
    )(hs, _mod_rows(mod_l, bsz), gains, w, *tabs)


def _attn_proj_kernel(x_ref, mod_ref, g_ref, w_ref, row_ref, col_ref, qt_ref, k_ref, vt_ref, *,
                      scale, quarter):
    d = x_ref.shape[2]
    tab = _rope_tile(row_ref, col_ref)
    heads = vt_ref.shape[1]
    for b in range(x_ref.shape[0]):
        mod = mod_ref[b]
        u = (_rms(x_ref[b], g_ref[0:1, :]) * (1.0 + mod[1:2, :]) + mod[0:1, :]).astype(BF16)
        qt_ref[b] = _apply_rope(_dot(u, w_ref[:, :d]) * scale, tab, quarter).T.astype(BF16)
        k_ref[b] = _apply_rope(_dot(u, w_ref[:, d:2 * d]), tab, quarter).astype(BF16)
        vt = _dot(u, w_ref[:, 2 * d:]).T.astype(BF16)
        vt_ref[b, :, 0, 0:LANES, :] = vt.reshape(heads, LANES, vt.shape[1])
        vt_ref[b, :, 0, LANES:, :] = jnp.ones((heads, VT_PAD, vt.shape[1]), BF16)


def _attn_proj(hs, mod_l, gains, w, layer, tabs, scale, quarter, nctx_tiles):
    split = isinstance(hs, tuple)
    streams = hs if split else (hs,)
    bsz, _, d = streams[-1].shape
    L = sum(a.shape[1] for a in streams)
    nlat = L // TM - nctx_tiles
    body = functools.partial(_attn_proj_kernel, scale=scale, quarter=quarter)
    return pl.pallas_call(
        _split_input(body, nctx_tiles) if split else body,
        grid=(L // TM,),
        in_specs=_stream_specs(hs, nctx_tiles)
        + [_mod_spec(d, bsz, nctx_tiles), _resident(gains), _resident(w, layer)]
        + _rope_specs(*tabs, nctx_tiles),
        out_specs=[
            pl.BlockSpec((bsz, d, TM), lambda i: (0, 0, jnp.where(i < nctx_tiles, i + nlat, i - nctx_tiles))),
            _tok_spec(bsz, d),
            pl.BlockSpec((bsz, DA_HEADS, 1, LANES + VT_PAD, TM), lambda i: (0, 0, i, 0, 0)),
        ],
        out_shape=[
            jax.ShapeDtypeStruct((bsz, d, L), BF16),
            jax.ShapeDtypeStruct((bsz, L, d), BF16),
            jax.ShapeDtypeStruct((bsz, DA_HEADS, L // TM, LANES + VT_PAD, TM), BF16),
        ],
        compiler_params=_cparams("parallel"),
    )(*streams, _mod_rows(mod_l, bsz), gains, w, *tabs)


def _attn_kernel(lam_ref, sub_ref, qt_ref, k_ref, vt_ref, o_ref,
                 s_c, s_a, s_b, mb_scr, m_scr, acc_scr, *, nb, lam_init):
    ctx = s_c.shape[2]
    tk = s_a.shape[2]
    kv = vt_ref.shape[4]
    qt = qt_ref[0]
    row = lax.broadcasted_iota(jnp.int32, qt.shape, 0)
    zero = jnp.zeros_like(qt)
    qts = (jnp.where(row < LANES // 2, qt, zero), jnp.where(row >= LANES // 2, qt, zero))

    slot = {id(s_c): 0, id(s_a): 1, id(s_b): 2}

    nqb = qt.shape[1] // MXU_COLS

    def scores(kb, s_out):
        for part in range(2):
            for j in range(nqb):
                s = _dot(kb, qts[part][:, j * MXU_COLS:(j + 1) * MXU_COLS])
                s_out[part, j] = s
                mb_scr[slot[id(s_out)], part, j] = jnp.max(s, axis=0, keepdims=True)

    def stage(s_in, vt0, nxt):
        if nxt is not None:
            scores(*nxt)
        for part in range(2):
            for j in range(nqb):
                m = m_scr[part, j]
                m_new = jnp.maximum(m, mb_scr[slot[id(s_in)], part, j])
                alpha = jnp.exp2(m - m_new)
                pv = None
                for n in range(s_in.shape[2] // kv):
                    p = jnp.exp2(s_in[part, j, n * kv:(n + 1) * kv, :] - m_new).astype(BF16)
                    pv_n = _dot(vt_ref[0, 0, vt0 + n], p)
                    pv = pv_n if pv is None else pv + pv_n
                m_scr[part, j] = m_new
                acc_scr[part, j] = alpha * acc_scr[part, j] + pv

    m_scr[...] = jnp.full(m_scr.shape, -1e30, F32)
    acc_scr[...] = jnp.zeros(acc_scr.shape, F32)

    bufs = (s_b, s_a)

    def keys(t):
        return k_ref[0, pl.ds(pl.multiple_of(ctx + (t - 1) * tk, LANES), tk), :]

    def vt_index(t):
        return ctx // kv + (t - 1) * (tk // kv)

    scores(k_ref[0, 0:ctx, :], s_c)
    stage(s_c, 0, (keys(1), s_a) if nb >= 1 else None)
    n_pairs = max(0, (nb - 1) // 2 if nb % 2 else (nb - 2) // 2)

    def pair(u, carry):
        t = 2 * u + 1
        stage(s_a, vt_index(t), (keys(t + 1), s_b))
        stage(s_b, vt_index(t + 1), (keys(t + 2), s_a))
        return carry

    if n_pairs > 0:
        lax.fori_loop(0, n_pairs, pair, 0)
    for t in range(2 * n_pairs + 1, nb + 1):
        stage(bufs[t % 2], vt_index(t), (keys(t + 1), bufs[(t + 1) % 2]) if t < nb else None)

    lv = lam_ref[...]
    lam = (jnp.exp(jnp.sum(lv[0:1] * lv[1:2], axis=-1, keepdims=True))
           - jnp.exp(jnp.sum(lv[2:3] * lv[3:4], axis=-1, keepdims=True)) + lam_init)
    a1, a2 = (jnp.concatenate([acc_scr[part, j] for j in range(nqb)], axis=1) for part in range(2))
    o = (a1[:LANES] / a1[LANES:LANES + 1] - lam * (a2[:LANES] / a2[LANES:LANES + 1])).T
    o = _rms(o, sub_ref[...]) * (1.0 - lam_init)
    o_ref[0] = o.astype(BF16)


def _attention_call(q_t, k, v_t, lam_vec, subln, ctx, tk, nb, lam_init, tq, q_off, n_out):
    bsz, d, _ = q_t.shape
    kv = v_t.shape[4]
    k_rows = ctx + nb * tk
    nqb = tq // MXU_COLS
    return pl.pallas_call(
        functools.partial(_attn_kernel, nb=nb, lam_init=lam_init),
        grid=(bsz, DA_HEADS, n_out // tq),
        in_specs=[
            pl.BlockSpec(lam_vec.shape, lambda b, h, i: (0, 0)),
            pl.BlockSpec((1, LANES), lambda b, h, i: (0, 0)),
            pl.BlockSpec((1, LANES, tq), lambda b, h, i: (b, h, i + q_off // tq)),
            pl.BlockSpec((1, k_rows, LANES), lambda b, h, i: (b, 0, h)),
            pl.BlockSpec((1, 1, k_rows // kv) + v_t.shape[3:], lambda b, h, i: (b, h, 0, 0, 0)),
        ],
        out_specs=pl.BlockSpec((1, tq, LANES), lambda b, h, i: (b, i, h)),
        out_shape=jax.ShapeDtypeStruct((bsz, n_out, d), BF16),
        scratch_shapes=[pltpu.VMEM((2, nqb, ctx, MXU_COLS), F32), pltpu.VMEM((2, nqb, tk, MXU_COLS), F32),
                        pltpu.VMEM((2, nqb, tk, MXU_COLS), F32), pltpu.VMEM((3, 2, nqb, 1, MXU_COLS), F32),
                        pltpu.VMEM((2, nqb, 1, MXU_COLS), F32),
                        pltpu.VMEM((2, nqb, LANES + VT_PAD, MXU_COLS), F32)],
        compiler_params=_cparams("parallel", "parallel", "arbitrary"),
    )(lam_vec, subln.reshape(1, LANES), q_t, k, v_t)


def _attention(q_t, k, v_t, lam_vec, subln, ctx, lam_init, with_ctx):
    L = k.shape[1]
    seq = L - ctx
    tk = min(1024, seq)
    nb = seq // tk
    tq = min(ATTN_TQ, seq)
    o_lat = _attention_call(q_t, k, v_t, lam_vec, subln, ctx, tk, nb, lam_init, tq, 0, seq)
    if not with_ctx:
        return o_lat
    o_ctx = _attention_call(q_t, k, v_t, lam_vec, subln, ctx, tk, 0, lam_init, TM, seq, ctx)
    return jnp.concatenate([o_ctx, o_lat], axis=1)


def _ret_kernel(lg_ref, q_ref, k_ref, v_ref, o_ref, r_scr):
    d = pl.program_id(1)
    c = pl.program_id(2)
    C = q_ref.shape[1]
    heads, dk, dv = r_scr.shape

    @pl.when(c == 0)
    def _():
        r_scr[...] = jnp.zeros_like(r_scr)

    fwd = d == 0
    ii = lax.broadcasted_iota(jnp.int32, (C, C), 0)
    jj = lax.broadcasted_iota(jnp.int32, (C, C), 1)
    dd = jnp.where(fwd, ii - jj, jj - ii)
    causal = dd >= 0
    lag = jnp.maximum(dd, 0).astype(F32)
    idx = lax.broadcasted_iota(jnp.int32, (C, 1), 0)
    q_lag = jnp.where(fwd, idx + 1, C - idx).astype(F32)
    k_lag = jnp.where(fwd, C - 1 - idx, idx).astype(F32)

    for h in range(heads):
        lg = lg_ref[0, h:h + 1, 0:1]
        q = q_ref[0, :, h * dk:(h + 1) * dk]
        k = k_ref[0, :, h * dk:(h + 1) * dk]
        v = v_ref[0, :, h * dv:(h + 1) * dv]
        r = r_scr[h]
        s = _dot_nt(q, k) * jnp.where(causal, jnp.exp(lg * lag), 0.0)
        inner = _dot(s.astype(BF16), v)
        cross = _dot((q.astype(F32) * jnp.exp(lg * q_lag)).astype(BF16), r.astype(BF16))
        o_ref[0, 0, :, h * dv:(h + 1) * dv] = (inner + cross).astype(BF16)
        r_scr[h] = jnp.exp(lg * float(C)) * r + _dot_tn((k.astype(F32) * jnp.exp(lg * k_lag)).astype(BF16), v)


def _retention(p, log_g, ctx):
    bsz, L, n = p.shape
    d = n // 6
    C = RET_CHUNK
    nc, ncc = L // C, ctx // C

    def chunk(dr, c):
        back = jnp.where(c < ncc, ncc - 1 - c, nc + ncc - 1 - c)
        return jnp.where(dr == 0, c, back)

    lg = jnp.broadcast_to(log_g.reshape(2, RET_HEADS, 1), (2, RET_HEADS, LANES))
    return pl.pallas_call(
        _ret_kernel,
        grid=(bsz, 2, nc),
        in_specs=[
            pl.BlockSpec((1, RET_HEADS, LANES), lambda b, dr, c: (dr, 0, 0)),
            pl.BlockSpec((1, C, d), lambda b, dr, c: (b, chunk(dr, c), 0)),
            pl.BlockSpec((1, C, d), lambda b, dr, c: (b, chunk(dr, c), 1)),
            pl.BlockSpec((1, C, 2 * d), lambda b, dr, c: (b, chunk(dr, c), 1)),
        ],
        out_specs=pl.BlockSpec((1, 1, C, 2 * d), lambda b, dr, c: (dr, b, chunk(dr, c), 0)),
        out_shape=jax.ShapeDtypeStruct((2, bsz, L, 2 * d), BF16),
        scratch_shapes=[pltpu.VMEM((RET_HEADS, d // RET_HEADS, 2 * d // RET_HEADS), F32)],
        compiler_params=_cparams("parallel", "arbitrary", "arbitrary"),
    )(lg, p, p, p)


def _transpose_pieces(tiles):
    hg = SSM_GROUP
    per = len(tiles)
    piece = lax.broadcasted_iota(jnp.int32, tiles[0].shape, 1) // hg
    k = per // 2
    while k >= 1:
        keep = (piece & k) == 0
        out = list(tiles)
        for r in range(per):
            if r & k == 0:
                x, y = tiles[r], tiles[r + k]
                out[r] = jnp.where(keep, x, pltpu.roll(y, hg * k, 1))
                out[r + k] = jnp.where(keep, pltpu.roll(x, LANES - hg * k, 1), y)
        tiles = out
        k //= 2
    return tiles


def _to_group_major(u_scr, o_ref, b):
    n, hg = SSM_CHUNK, SSM_GROUP
    per = LANES // hg
    _, ntile, tm, _ = u_scr.shape
    rows = tm // n
    piece = lax.broadcasted_iota(jnp.int32, (rows, LANES), 1) // hg
    for lt in range(ntile):
        for th in range(n * hg // LANES):
            src = [u_scr[b, lt, pl.ds(th * per + tp, rows, stride=n), :] for tp in range(per)]
            for j in range(per):
                acc = None
                for tp in range(per):
                    shift = (hg * (tp - j)) % LANES
                    moved = src[tp] if shift == 0 else pltpu.roll(src[tp], shift, 1)
                    acc = moved if acc is None else jnp.where(piece == tp, moved, acc)
                o_ref[b, lt * per + j, :, th * LANES:(th + 1) * LANES] = acc.astype(o_ref.dtype)


def _from_group_major(y_ref, o_scr, b):
    n, hg = SSM_CHUNK, SSM_GROUP
    per = LANES // hg
    _, ntile, tm, _ = o_scr.shape
    rows = tm // n
    for lt in range(ntile):
        for th in range(n * hg // LANES):
            src = [y_ref[b, lt * per + j, :, th * LANES:(th + 1) * LANES].astype(F32) for j in range(per)]
            for tp, tile in enumerate(_transpose_pieces(src)):
                o_scr[b, lt, pl.ds(th * per + tp, rows, stride=n), :] = tile


def _normmod_kernel(x_ref, mod_ref, g_ref, o_ref, u_scr):
    for b in range(x_ref.shape[0]):
        mod = mod_ref[b]
        u = _rms(x_ref[b], g_ref[0:1, :]) * (1.0 + mod[1:2, :]) + mod[0:1, :]
        for lt in range(u_scr.shape[1]):
            u_scr[b, lt] = u[:, lt * LANES:(lt + 1) * LANES]
        _to_group_major(u_scr, o_ref, b)


def _normmod(hs, mod_l, gains, nctx_tiles):
    bsz, L, d = hs.shape
    G, rows, width = d // SSM_GROUP, TM // SSM_CHUNK, SSM_CHUNK * SSM_GROUP
    return pl.pallas_call(
        _normmod_kernel,
        grid=(L // TM,),
        in_specs=[_tok_spec(bsz, d), _mod_spec(d, bsz, nctx_tiles), _resident(gains)],
        out_specs=pl.BlockSpec((bsz, G, rows, width), lambda i: (0, 0, i, 0)),
        out_shape=jax.ShapeDtypeStruct((bsz, G, L // SSM_CHUNK, width), BF16),
        scratch_shapes=[pltpu.VMEM((bsz, d // LANES, TM, LANES), F32)],
        compiler_params=_cparams("parallel"),
    )(hs, _mod_rows(mod_l, bsz), gains)


def _s5_kernel(x_ref, t_ref, wb_ref, wc_ref, a_ref, y_ref, st_scr, *, nctx):
    gb = x_ref.shape[1]
    nch = x_ref.shape[2]
    for gi in range(gb):
        x = x_ref[0, gi]
        for dr in range(2):
            s = _dot(x, wb_ref[gi, dr])
            st_scr[2 * dr, pl.ds(gi, nch, stride=gb), :] = s[:, :LANES]
            st_scr[2 * dr + 1, pl.ds(gi, nch, stride=gb), :] = s[:, LANES:]

    a = a_ref[...]

    def step(dr, row, h_re, h_im):
        s_re = st_scr[2 * dr, pl.ds(row, gb), :]
        s_im = st_scr[2 * dr + 1, pl.ds(row, gb), :]
        st_scr[2 * dr, pl.ds(row, gb), :] = h_re
        st_scr[2 * dr + 1, pl.ds(row, gb), :] = h_im
        a_re, a_im = a[dr, 0], a[dr, 1]
        return a_re * h_re - a_im * h_im + s_re, a_re * h_im + a_im * h_re + s_im

    def body(t, carry):
        f_re, f_im, b_re, b_im = carry
        cb = jnp.where(t < nctx, nctx - 1 - t, nch + nctx - 1 - t)
        f_re, f_im = step(0, pl.multiple_of(t * gb, gb), f_re, f_im)
        b_re, b_im = step(1, pl.multiple_of(cb * gb, gb), b_re, b_im)
        return f_re, f_im, b_re, b_im

    z = jnp.zeros((gb, LANES), F32)
    lax.fori_loop(0, nch, body, (z, z, z, z))

    for gi in range(gb):
        y = _dot(x_ref[0, gi], t_ref[gi])
        for dr in range(2):
            h = jnp.concatenate([st_scr[2 * dr, pl.ds(gi, nch, stride=gb), :],
                                 st_scr[2 * dr + 1, pl.ds(gi, nch, stride=gb), :]], axis=1)
            y = y + _dot(h.astype(BF16), wc_ref[gi, dr])
        y_ref[0, gi] = y.astype(y_ref.dtype)


def _s5_weights(lam_re, lam_im, log_dt, b_re, b_im, c_re, c_im):
    n = SSM_CHUNK
    lam = lax.complex(lam_re.astype(F32), lam_im.astype(F32))
    ldt = lam * jnp.exp(log_dt.astype(F32))[..., None]
    lam_bar = jnp.exp(ldt)
    b_bar = ((lam_bar - 1.0) / lam)[..., None] * lax.complex(b_re.astype(F32), b_im.astype(F32))
    c_mat = lax.complex(c_re.astype(F32), c_im.astype(F32))
    steps = jnp.arange(n + 1, dtype=F32)
    pw = jnp.exp(ldt[None] * steps[:, None, None, None])
    G, P, Hg = b_bar.shape[1:]

    kern = jnp.real(jnp.einsum('kdgp,dgjp,dgpi->dkgij', pw[:n], c_mat, b_bar))
    lag = jnp.arange(n)[None, :] - jnp.arange(n)[:, None]
    k_idx = jnp.arange(n)[:, None, None]
    pick = jnp.stack([lag[None] == k_idx, -lag[None] == k_idx]).astype(F32)
    t_mat = jnp.einsum('dkst,dkgij->gsitj', pick, kern).reshape(G, n * Hg, n * Hg)

    zpad = jnp.zeros((G, n * Hg, LANES - P), F32)
    zrow = jnp.zeros((G, LANES - P, n * Hg), F32)
    wbs, wcs = [], []
    for dr in range(2):
        e_in = pw[:n][::-1, dr] if dr == 0 else pw[:n, dr]
        wb = (e_in[:, :, :, None] * b_bar[dr][None]).transpose(1, 0, 3, 2).reshape(G, n * Hg, P)
        wbs.append(jnp.concatenate([jnp.real(wb), zpad, jnp.imag(wb), zpad], axis=2))
        e_out = pw[1:, dr] if dr == 0 else pw[1:][::-1, dr]
        wc = (c_mat[dr][None] * e_out[:, :, None, :]).transpose(1, 3, 0, 2).reshape(G, P, n * Hg)
        wcs.append(jnp.concatenate([jnp.real(wc), zrow, -jnp.imag(wc), zrow], axis=1))
    w_b = jnp.stack(wbs, axis=1)
    w_c = jnp.stack(wcs, axis=1)
    a_n = pw[n]
    pad = jnp.zeros((2, G, LANES - P), F32)
    a = jnp.stack([jnp.concatenate([jnp.real(a_n), pad], -1),
                   jnp.concatenate([jnp.imag(a_n), pad], -1)], axis=1)
    return t_mat.astype(BF16), w_b.astype(BF16), w_c.astype(BF16), a


def _s5_core(x, weights, ctx):
    bsz, G, nch, width = x.shape
    n, gb = SSM_CHUNK, SSM_GROUP_BLOCK
    t_mat, w_b, w_c, a = weights
    return pl.pallas_call(
        functools.partial(_s5_kernel, nctx=ctx // n),
        grid=(bsz, G // gb),
        in_specs=[
            pl.BlockSpec((1, gb, nch, width), lambda b, g: (b, g, 0, 0)),
            pl.BlockSpec((gb, width, width), lambda b, g: (g, 0, 0)),
            pl.BlockSpec((gb, 2, width, 2 * LANES), lambda b, g: (g, 0, 0, 0)),
            pl.BlockSpec((gb, 2, 2 * LANES, width), lambda b, g: (g, 0, 0, 0)),
            pl.BlockSpec((2, 2, gb, LANES), lambda b, g: (0, 0, g, 0)),
        ],
        out_specs=pl.BlockSpec((1, gb, nch, width), lambda b, g: (b, g, 0, 0)),
        out_shape=jax.ShapeDtypeStruct((bsz, G, nch, width), BF16),
        scratch_shapes=[pltpu.VMEM((4, nch * gb, LANES), F32)],
        compiler_params=_cparams("parallel", "parallel"),
    )(x, t_mat, w_b, w_c, a)


def _finish(h, y, mod, g_ref, o_ref, b):
    o_ref[b] = h + mod[2:3, :] * _rms(y, g_ref[1:2, :])


def _attn_out_kernel(h_ref, mod_ref, g_ref, o_in_ref, w_ref, o_ref):
    for b in range(h_ref.shape[0]):
        _finish(h_ref[b], _dot(o_in_ref[b], w_ref[...]), mod_ref[b], g_ref, o_ref, b)


def _ret_out_kernel(h_ref, mod_ref, g_ref, o_in_ref, gate_ref, w_ref, o_ref):
    for b in range(h_ref.shape[0]):
        o = o_in_ref[0, b].astype(F32) + o_in_ref[1, b].astype(F32)
        dv = o.shape[1] // RET_HEADS
        parts = []
        for hd in range(RET_HEADS):
            oh = o[:, hd * dv:(hd + 1) * dv]
            parts.append(oh * lax.rsqrt(jnp.mean(oh * oh, axis=-1, keepdims=True) + NORM_EPS))
        o = jnp.concatenate(parts, axis=1)
        y = _dot((_silu(gate_ref[b].astype(F32)) * o).astype(BF16), w_ref[...])
        _finish(h_ref[b], y, mod_ref[b], g_ref, o_ref, b)


def _s5_out_kernel(h_ref, mod_ref, g_ref, y_ref, dskip_ref, w_ref, o_ref, y_scr):
    for b in range(h_ref.shape[0]):
        h = h_ref[b]
        mod = mod_ref[b]
        d = h.shape[1]
        _from_group_major(y_ref, y_scr, b)
        u = _rms(h, g_ref[0:1, :]) * (1.0 + mod[1:2, :]) + mod[0:1, :]
        y_ssm = jnp.concatenate([y_scr[b, lt] for lt in range(y_scr.shape[1])], axis=1)
        z = y_ssm + dskip_ref[...] * u
        gl = 0.5 * z * (1.0 + jnp.tanh(math.sqrt(2.0 / math.pi) * (z + 0.044715 * (z * z * z))))
        gl = gl.astype(BF16)
        y = _dot(gl, w_ref[:, :d]) * jax.nn.sigmoid(_dot(gl, w_ref[:, d:]))
        _finish(h, y, mod, g_ref, o_ref, b)


def _mixer_out(kernel, hs, mod_l, gains, nctx_tiles, extra, extra_specs, scratch=(), skip_ctx=False):
    split = isinstance(hs, tuple)
    streams = hs if split else (hs,)
    bsz, _, d = streams[-1].shape
    L = sum(a.shape[1] for a in streams)
    off = nctx_tiles if skip_ctx else 0
    n_tiles = L // TM - off
    return pl.pallas_call(
        _split_input(kernel, nctx_tiles - off) if split else kernel,
        grid=(n_tiles,),
        in_specs=_stream_specs(hs, nctx_tiles, off)
        + [_mod_spec(d, bsz, nctx_tiles - off), _resident(gains)] + extra_specs,
        out_specs=_tok_spec(bsz, d),
        out_shape=jax.ShapeDtypeStruct((bsz, n_tiles * TM, d), F32),
        scratch_shapes=list(scratch),
        compiler_params=_cparams("parallel"),
    )(*streams, _mod_rows(mod_l, bsz), gains, *extra)


def _ffn_kernel(h_ref, mod_ref, g_ref, wg_ref, wu_ref, wd_ref, o_ref):
    for b in range(h_ref.shape[0]):
        h = h_ref[b]
        mod = mod_ref[b]
        v = (_rms(h, g_ref[2:3, :]) * (1.0 + mod[4:5, :]) + mod[3:4, :]).astype(BF16)
        act = (_silu(_dot(v, wg_ref[...])) * _dot(v, wu_ref[...])).astype(BF16)
        y = _dot(act, wd_ref[...])
        o_ref[b] = h + mod[5:6, :] * _rms(y, g_ref[3:4, :])


def _ffn(hs, mod_l, gains, wg, wu, wd, layer, nctx_tiles):
    bsz, L, d = hs.shape
    return pl.pallas_call(
        _ffn_kernel,
        grid=(L // TM,),
        in_specs=[_tok_spec(bsz, d), _mod_spec(d, bsz, nctx_tiles), _resident(gains),
                  _resident(wg, layer), _resident(wu, layer), _resident(wd, layer)],
        out_specs=_tok_spec(bsz, d),
        out_shape=jax.ShapeDtypeStruct((bsz, L, d), F32),
        compiler_params=_cparams("parallel"),
    )(hs, _mod_rows(mod_l, bsz), gains, wg, wu, wd)


def kernel(x, c, ctx, c_ctx, mod_w, mod_b, norm_g, attn_w_in, attn_w_out, attn_lambda, attn_subln,
           ret_w_in, ret_w_out, ret_decay_logit, ssm_lambda_re, ssm_lambda_im, ssm_log_dt,
           ssm_b_re, ssm_b_im, ssm_c_re, ssm_c_im, ssm_d, ssm_w_glu, ffn_w_gate, ffn_w_up, ffn_w_down):
    bsz, seq, d = x.shape
    nctx = ctx.shape[1]
    depth = mod_w.shape[0]
    assert nctx % TM == 0 and seq % TM == 0 and seq % GRID_W == 0
    nct = nctx // TM
    L = nctx + seq

    mods = _modulation(c, c_ctx, mod_w, mod_b)
    (attn_w_in, attn_w_out, ret_w_in, ret_w_out, ssm_w_glu, ffn_w_gate, ffn_w_up, ffn_w_down) = (
        w.astype(BF16) for w in (attn_w_in, attn_w_out, ret_w_in, ret_w_out, ssm_w_glu,
                                 ffn_w_gate, ffn_w_up, ffn_w_down))
    hs = (ctx, x) if depth > 1 else jnp.concatenate([ctx, x], axis=1)
    da_dim = d // (2 * DA_HEADS)
    dk = d // RET_HEADS
    tab_a = _rope_tables(seq, nctx, da_dim)
    tab_r = _rope_tables(seq, nctx, dk)

    for l in range(depth):
        kind, j = l % 3, l // 3
        last = l == depth - 1
        mod_l, gains = mods[l], norm_g[l]
        if kind == 0:
            lam_init = 0.8 - 0.6 * math.exp(-0.3 * l)
            q_t, k, v_t = _attn_proj(hs, mod_l, gains, attn_w_in, j, tab_a,
                                     da_dim ** -0.5 * math.log2(math.e), da_dim // 4, nct)
            o = _attention(q_t, k, v_t, attn_lambda[j], attn_subln[j], nctx, lam_init, not last)
            hs = _mixer_out(_attn_out_kernel, hs, mod_l, gains, nct, (o, attn_w_out),
                            [_tok_spec(bsz, d), _resident(attn_w_out, j)], skip_ctx=last)
            nct = 0 if last else nct
        elif kind == 1:
            chunks = ((0, d, 1.0, True), (d, d, dk ** -0.5, True)) + tuple(
                (c0, d, 1.0, False) for c0 in range(2 * d, 6 * d, d))
            p = _proj_in(hs, mod_l, gains, ret_w_in, j, tab_r, chunks, dk // 4, nct)
            log_g = jax.nn.log_sigmoid(ret_decay_logit[j].astype(F32))
            o = _retention(p, log_g, nctx)
            hs = _mixer_out(_ret_out_kernel, hs, mod_l, gains, nct, (o, p, ret_w_out),
                            [pl.BlockSpec((2, bsz, TM, 2 * d), lambda i: (0, 0, i, 0)),
                             _tok_spec(bsz, 2 * d, 2), _resident(ret_w_out, j)])
        else:
            u = _normmod(hs, mod_l, gains, nct)
            weights = _s5_weights(ssm_lambda_re[j], ssm_lambda_im[j], ssm_log_dt[j], ssm_b_re[j],
                                  ssm_b_im[j], ssm_c_re[j], ssm_c_im[j])
            y = _s5_core(u, weights, nctx)
            dskip = ssm_d[j].astype(F32).reshape(1, d)
            y_spec = pl.BlockSpec(y.shape[0:2] + (TM // SSM_CHUNK,) + y.shape[3:], lambda i: (0, 0, i, 0))
            hs = _mixer_out(_s5_out_kernel, hs, mod_l, gains, nct, (y, dskip, ssm_w_glu),
                            [y_spec, _resident(dskip), _resident(ssm_w_glu, j)],
                            scratch=[pltpu.VMEM((bsz, d // LANES, TM, LANES), F32)])
        hs = _ffn(hs, mod_l, gains, ffn_w_gate, ffn_w_up, ffn_w_down, l, nct)
    return hs if hs.shape[1] == seq else hs[:, nctx:, :]
```
